```python
import jax, jax.numpy as jnp
from jax import lax
import numpy as np

D_MODEL = 2048
BATCH = 2
SEQ = 4096
DEPTH = 2

D_MIX = 2048
A_HEADS = 6
A_HEAD_DIM = 128
A_WIDTH = A_HEADS * A_HEAD_DIM
Q_BLOCK = 128
B_HEADS = 4
B_HEAD_DIM = 192
B_WIDTH = B_HEADS * B_HEAD_DIM
B_CHUNK = 64
B_CONV = 4
C_GROUPS = 4
C_WIDTH = 512
C_CONV = 31
LN_EPS = 1e-5
P_HEADS = 8
P_QDIM = 256
P_HALF = P_QDIM // 2
N_KEYS = 128
N_EXPERTS = N_KEYS * N_KEYS
P_TOPK = 16
P_TOKEN_BLOCK = 64
EPS = 1e-6

SPLIT_SIZES = (A_WIDTH, A_WIDTH, A_WIDTH, A_HEADS, 2 * B_WIDTH, B_WIDTH, B_WIDTH, B_HEADS, B_HEADS, C_WIDTH, C_WIDTH)
SPLIT_POINTS = tuple(int(p) for p in np.cumsum(SPLIT_SIZES)[:-1])
N_IN = int(sum(SPLIT_SIZES))

kernel_name = 'hybrid_fox_mlstm_conformer_peer'


def rmsnorm(x, g):
    xf = x.astype(jnp.float32)
    y = xf * lax.rsqrt(jnp.mean(xf * xf, axis=-1, keepdims=True) + EPS)
    return (y * g.astype(jnp.float32)).astype(x.dtype)


def head_rmsnorm(h, g, n_heads):
    b, s, w = h.shape
    hf = h.astype(jnp.float32).reshape(b, s, n_heads, w // n_heads)
    hf = hf * lax.rsqrt(jnp.mean(hf * hf, axis=-1, keepdims=True) + EPS)
    return (hf.reshape(b, s, w) * g.astype(jnp.float32)).astype(h.dtype)


def layernorm(x, g, bias):
    xf = x.astype(jnp.float32)
    mu = jnp.mean(xf, axis=-1, keepdims=True)
    xc = xf - mu
    y = xc * lax.rsqrt(jnp.mean(xc * xc, axis=-1, keepdims=True) + LN_EPS)
    return (y * g.astype(jnp.float32) + bias.astype(jnp.float32)).astype(x.dtype)


def causal_dwconv(x, w):
    k = w.shape[0]
    return lax.conv_general_dilated(
        x, w[:, None, :].astype(x.dtype), window_strides=(1,), padding=[(k - 1, 0)],
        dimension_numbers=('NWC', 'WIO', 'NWC'), feature_group_count=x.shape[-1])


def fox_attention(q, k, v, log_f):
    b, s, h, dh = q.shape
    nb = s // Q_BLOCK
    scale = dh ** -0.5
    c = jnp.cumsum(log_f, axis=1).transpose(0, 2, 1)
    kpos = jnp.arange(s)
    q_blocks = q.reshape(b, nb, Q_BLOCK, h, dh).transpose(1, 0, 2, 3, 4)
    c_blocks = c.reshape(b, h, nb, Q_BLOCK).transpose(2, 0, 1, 3)

    def one_block(args):
        qi, ci, bi = args
        sc = jnp.einsum('bqhd,bkhd->bhqk', qi, k).astype(jnp.float32) * scale
        sc = sc + ci[..., :, None] - c[:, :, None, :]
        qpos = bi * Q_BLOCK + jnp.arange(Q_BLOCK)
        sc = jnp.where(kpos[None, :] <= qpos[:, None], sc, -jnp.inf)
        p = jax.nn.softmax(sc, axis=-1).astype(v.dtype)
        return jnp.einsum('bhqk,bkhd->bqhd', p, v)

    o = lax.map(one_block, (q_blocks, c_blocks, jnp.arange(nb)))
    return o.transpose(1, 0, 2, 3, 4).reshape(b, s, h * dh)


def mlstm_chunkwise(q, k, v, log_i, log_f):
    b, s, h, dk = q.shape
    dv = v.shape[-1]
    nc = s // B_CHUNK
    L = B_CHUNK

    def to_chunks(t):
        return t.astype(jnp.float32).reshape(b, nc, L, h, -1).transpose(1, 0, 3, 2, 4)

    def gate_chunks(g):
        return g.reshape(b, nc, L, h).transpose(1, 0, 3, 2)

    qc = to_chunks(q) * (dk ** -0.5)
    kc = to_chunks(k)
    vc = to_chunks(v)
    ic = gate_chunks(log_i)
    fc = gate_chunks(log_f)
    causal = jnp.tril(jnp.ones((L, L), dtype=bool))

    def step(carry, inp):
        cm, nm, m = carry
        qq, kk, vv, li, lf = inp
        bcum = jnp.cumsum(lf, axis=-1)
        d = jnp.where(causal, bcum[..., :, None] - bcum[..., None, :] + li[..., None, :], -jnp.inf)
        inter = bcum + m[..., None]
        m_t = jnp.maximum(inter, jnp.max(d, axis=-1))
        w_inter = jnp.exp(inter - m_t)
        sc = jnp.einsum('bhtd,bhsd->bhts', qq, kk) * jnp.exp(d - m_t[..., None])
        num = w_inter[..., None] * jnp.einsum('bhtd,bhde->bhte', qq, cm) + jnp.einsum('bhts,bhse->bhte', sc, vv)
        den = w_inter * jnp.einsum('bhtd,bhd->bht', qq, nm) + jnp.sum(sc, axis=-1)
        hh = num / jnp.maximum(jnp.abs(den), jnp.exp(-m_t))[..., None]
        b_last = bcum[..., -1]
        dec = b_last[..., None] - bcum + li
        m_new = jnp.maximum(b_last + m, jnp.max(dec, axis=-1))
        a = jnp.exp(b_last + m - m_new)
        wk = jnp.exp(dec - m_new[..., None])[..., None] * kk
        cm = a[..., None, None] * cm + jnp.einsum('bhsd,bhse->bhde', wk, vv)
        nm = a[..., None] * nm + jnp.sum(wk, axis=2)
        return (cm, nm, m_new), hh

    init = (jnp.zeros((b, h, dk, dv), jnp.float32), jnp.zeros((b, h, dk), jnp.float32), jnp.zeros((b, h), jnp.float32))
    _, hs = lax.scan(step, init, (qc, kc, vc, ic, fc))
    return hs.transpose(1, 0, 3, 2, 4).reshape(b, s, h * dv).astype(q.dtype)


def hybrid_mixer(xn, w_in, fox_f_bias, mlstm_conv_w, mlstm_i_bias, mlstm_f_bias,
                 conf_dw_w, conf_dw_b, conf_ln_g, conf_ln_b, conf_pw_w, conf_pw_b,
                 out_norm_g, w_out):
    b, s, _ = xn.shape
    z = xn @ w_in
    aq, ak, av, af, bqk, bv, bo, bi, bf, ca, cg = jnp.split(z, SPLIT_POINTS, axis=-1)

    log_fa = jax.nn.log_sigmoid((af + fox_f_bias).astype(jnp.float32))
    ya = fox_attention(aq.reshape(b, s, A_HEADS, A_HEAD_DIM), ak.reshape(b, s, A_HEADS, A_HEAD_DIM),
                       av.reshape(b, s, A_HEADS, A_HEAD_DIM), log_fa)

    qk = jax.nn.silu(causal_dwconv(bqk, mlstm_conv_w))
    bq = qk[..., :B_WIDTH].reshape(b, s, B_HEADS, B_HEAD_DIM)
    bk = qk[..., B_WIDTH:].reshape(b, s, B_HEADS, B_HEAD_DIM)
    log_ib = (bi + mlstm_i_bias).astype(jnp.float32)
    log_fb = jax.nn.log_sigmoid((bf + mlstm_f_bias).astype(jnp.float32))
    readout = mlstm_chunkwise(bq, bk, bv.reshape(b, s, B_HEADS, B_HEAD_DIM), log_ib, log_fb)
    yb = jax.nn.sigmoid(bo) * readout

    yc = ca * jax.nn.sigmoid(cg)
    yc = causal_dwconv(yc, conf_dw_w) + conf_dw_b
    yc = jax.nn.silu(layernorm(yc, conf_ln_g, conf_ln_b))
    yc = yc @ conf_pw_w + conf_pw_b

    g_a = out_norm_g[:A_WIDTH]
    g_b = out_norm_g[A_WIDTH:A_WIDTH + B_WIDTH]
    g_c = out_norm_g[A_WIDTH + B_WIDTH:]
    y = jnp.concatenate([head_rmsnorm(ya, g_a, A_HEADS), head_rmsnorm(yb, g_b, B_HEADS),
                         head_rmsnorm(yc, g_c, C_GROUPS)], axis=-1)
    return y @ w_out


def peer_ffn(xn, peer_wq, peer_keys, peer_u, peer_v):
    b, s, d = xn.shape
    t = b * s
    xt = xn.reshape(t, d)
    q = (xt @ peer_wq).reshape(t, P_HEADS, 2, P_HALF)
    sc = jnp.einsum('thcd,hcnd->thcn', q, peer_keys).astype(jnp.float32)
    vals, idx = lax.top_k(sc, P_TOPK)
    cand = vals[:, :, 0, :, None] + vals[:, :, 1, None, :]
    cand_ids = idx[:, :, 0, :, None] * N_KEYS + idx[:, :, 1, None, :]
    top_s, top_pos = lax.top_k(cand.reshape(t, P_HEADS, P_TOPK * P_TOPK), P_TOPK)
    experts = jnp.take_along_axis(cand_ids.reshape(t, P_HEADS, P_TOPK * P_TOPK), top_pos, axis=-1)
    gates = jax.nn.softmax(top_s, axis=-1)

    nb = t // P_TOKEN_BLOCK
    xb = xt.reshape(nb, P_TOKEN_BLOCK, d)
    eb = experts.reshape(nb, P_TOKEN_BLOCK, P_HEADS * P_TOPK)
    gb = gates.reshape(nb, P_TOKEN_BLOCK, P_HEADS * P_TOPK)

    def one_block(args):
        xi, ei, gi = args
        ue = jnp.take(peer_u, ei, axis=0)
        ve = jnp.take(peer_v, ei, axis=0)
        act = jax.nn.gelu(jnp.einsum('td,ted->te', xi, ue).astype(jnp.float32), approximate=False) * gi
        return jnp.einsum('te,ted->td', act.astype(ve.dtype), ve)

    out = lax.map(one_block, (xb, eb, gb))
    return out.reshape(b, s, d)


def setup_inputs(seed: int = 0) -> dict:
    key = jax.random.key(seed)
    ks = jax.random.split(key, 22)
    f32 = jnp.float32

    def nrm(k, shape, scale):
        return jax.random.normal(k, shape, f32) * scale

    return {
        'x': nrm(ks[0], (BATCH, SEQ, D_MODEL), 1.0),
        'w_in': nrm(ks[1], (DEPTH, D_MODEL, N_IN), D_MODEL ** -0.5),
        'fox_f_bias': jax.random.uniform(ks[2], (DEPTH, A_HEADS), f32, 1.0, 4.0),
        'mlstm_conv_w': nrm(ks[3], (DEPTH, B_CONV, 2 * B_WIDTH), B_CONV ** -0.5),
        'mlstm_i_bias': nrm(ks[4], (DEPTH, B_HEADS), 0.1),
        'mlstm_f_bias': jax.random.uniform(ks[5], (DEPTH, B_HEADS), f32, 3.0, 6.0),
        'conf_dw_w': nrm(ks[6], (DEPTH, C_CONV, C_WIDTH), C_CONV ** -0.5),
        'conf_dw_b': nrm(ks[7], (DEPTH, C_WIDTH), 0.02),
        'conf_ln_g': 1.0 + nrm(ks[8], (DEPTH, C_WIDTH), 0.02),
        'conf_ln_b': nrm(ks[9], (DEPTH, C_WIDTH), 0.02),
        'conf_pw_w': nrm(ks[10], (DEPTH, C_WIDTH, C_WIDTH), C_WIDTH ** -0.5),
        'conf_pw_b': nrm(ks[11], (DEPTH, C_WIDTH), 0.02),
        'out_norm_g': 1.0 + nrm(ks[12], (DEPTH, D_MIX), 0.02),
        'w_out': nrm(ks[13], (DEPTH, D_MIX, D_MODEL), D_MIX ** -0.5),
        'norm1_g': 1.0 + nrm(ks[14], (DEPTH, D_MODEL), 0.02),
        'norm2_g': 1.0 + nrm(ks[15], (DEPTH, D_MODEL), 0.02),
        'peer_wq': nrm(ks[16], (DEPTH, D_MODEL, P_HEADS * P_QDIM), D_MODEL ** -0.5),
        'peer_keys': nrm(ks[17], (DEPTH, P_HEADS, 2, N_KEYS, P_HALF), P_HALF ** -0.5),
        'peer_u': nrm(ks[18], (DEPTH, N_EXPERTS, D_MODEL), D_MODEL ** -0.5),
        'peer_v': nrm(ks[19], (DEPTH, N_EXPERTS, D_MODEL), P_HEADS ** -0.5),
        'final_g': 1.0 + nrm(ks[20], (D_MODEL,), 0.02),
    }


def reference(x, w_in, fox_f_bias, mlstm_conv_w, mlstm_i_bias, mlstm_f_bias,
              conf_dw_w, conf_dw_b, conf_ln_g, conf_ln_b, conf_pw_w, conf_pw_b,
              out_norm_g, w_out, norm1_g, norm2_g, peer_wq, peer_keys, peer_u, peer_v, final_g):
    for l in range(DEPTH):
        h = rmsnorm(x, norm1_g[l])
        x = x + hybrid_mixer(h, w_in[l], fox_f_bias[l], mlstm_conv_w[l], mlstm_i_bias[l], mlstm_f_bias[l],
                             conf_dw_w[l], conf_dw_b[l], conf_ln_g[l], conf_ln_b[l], conf_pw_w[l], conf_pw_b[l],
                             out_norm_g[l], w_out[l])
        h = rmsnorm(x, norm2_g[l])
        x = x + peer_ffn(h, peer_wq[l], peer_keys[l], peer_u[l], peer_v[l])
    return rmsnorm(x, final_g)
```

```python
import functools

import jax
import jax.numpy as jnp
from jax import lax
from jax.experimental import pallas as pl
from jax.experimental.pallas import tpu as pltpu

F32 = jnp.float32
BF16 = jnp.bfloat16

EPS = 1e-6
LN_EPS = 1e-5

A_HEADS = 6
A_HEAD_DIM = 128
A_WIDTH = A_HEADS * A_HEAD_DIM
B_HEADS = 4
B_HEAD_DIM = 192
B_HEAD_PAD = 256
B_WIDTH = B_HEADS * B_HEAD_DIM
B_PAD_WIDTH = B_HEADS * B_HEAD_PAD
B_CHUNK = 64
B_CONV = 4
C_GROUPS = 4
C_WIDTH = 512
C_CONV = 31
P_HEADS = 8
P_HALF = 128
N_KEYS = 128
P_TOPK = 16
GATE_LANES = 128

COL_BQ = 0
COL_BK = B_PAD_WIDTH
COL_BV = 2 * B_PAD_WIDTH
COL_BO = 3 * B_PAD_WIDTH
COL_CA = 4 * B_PAD_WIDTH
COL_CG = COL_CA + C_WIDTH
COL_AQ = COL_CG + C_WIDTH
COL_AK = COL_AQ + A_WIDTH
COL_AV = COL_AK + A_WIDTH
Z_USED = COL_AV + A_WIDTH
Z_WIDTH = 7680
G_AF = 0
G_BI = A_HEADS
G_BF = A_HEADS + B_HEADS

VMEM_LIMIT = 60 * 1024 * 1024


def _cparams(sem):
    return pltpu.CompilerParams(dimension_semantics=sem, vmem_limit_bytes=VMEM_LIMIT)


def _split_bf16(a):
    hi = a.astype(BF16)
    lo = (a - hi.astype(F32)).astype(BF16)
    return hi, lo


def _dot(a, b):
    return jnp.dot(a, b, preferred_element_type=F32)


def _dot_nt(a, b):
    return lax.dot_general(a, b, (((1,), (1,)), ((), ())), preferred_element_type=F32)


def _dot_tn(a, b):
    return lax.dot_general(a, b, (((0,), (0,)), ((), ())), preferred_element_type=F32)


def _dot3(ah, al, bh, bl, dot):
    return dot(ah, bh) + dot(ah, bl) + dot(al, bh)


def _sigmoid(x):
    return 1.0 / (1.0 + jnp.exp(-x))


def _inproj_body(x_ref, g_ref, w_ref, wgh_ref, wgl_ref, z_ref, zg_ref, xn_ref):
    @pl.when(pl.program_id(1) == 0)
    def _():
        xf = x_ref[...]
        ms = jnp.mean(xf * xf, axis=-1, keepdims=True)
        xn = xf * lax.rsqrt(ms + EPS) * g_ref[...]
        hi, lo = _split_bf16(xn)
        xn_ref[...] = hi
        zg_ref[...] = _dot3(hi, lo, wgh_ref[...], wgl_ref[...], _dot)

    z_ref[...] = _dot(xn_ref[...], w_ref[...])


def _in_proj(x2d, g, w, wgh, wgl, *, tm, tn):
    t, d = x2d.shape
    n = w.shape[1]
    return pl.pallas_call(
        _inproj_body,
        grid=(t // tm, n // tn),
        in_specs=[
            pl.BlockSpec((tm, d), lambda i, j: (i, 0)),
            pl.BlockSpec((1, d), lambda i, j: (0, 0)),
            pl.BlockSpec((d, tn), lambda i, j: (0, j)),
            pl.BlockSpec((d, GATE_LANES), lambda i, j: (0, 0)),
            pl.BlockSpec((d, GATE_LANES), lambda i, j: (0, 0)),
        ],
        out_specs=[
            pl.BlockSpec((tm, tn), lambda i, j: (i, j)),
            pl.BlockSpec((tm, GATE_LANES), lambda i, j: (i, 0)),
        ],
        out_shape=[
            jax.ShapeDtypeStruct((t, n), F32),
            jax.ShapeDtypeStruct((t, GATE_LANES), F32),
        ],
        scratch_shapes=[pltpu.VMEM((tm, d), BF16)],
        compiler_params=_cparams(("parallel", "arbitrary")),
        name="in_proj",
    )(x2d, g, w, wgh, wgl)


def _gates_body(zg_ref, bias_ref, col_ref, row_ref, carry_ref, *, tb):
    @pl.when(pl.program_id(1) == 0)
    def _():
        carry_ref[...] = jnp.zeros_like(carry_ref)

    v = zg_ref[0] + bias_ref[...]
    ls = jnp.minimum(v, 0.0) - jnp.log1p(jnp.exp(-jnp.abs(v)))
    r = lax.broadcasted_iota(jnp.int32, (tb, tb), 0)
    c = lax.broadcasted_iota(jnp.int32, (tb, tb), 1)
    tri = (c <= r).astype(F32)
    tri_chunk = jnp.where((r // B_CHUNK) == (c // B_CHUNK), tri, 0.0)
    csum = jnp.dot(tri, ls, precision=lax.Precision.HIGHEST, preferred_element_type=F32) + carry_ref[...]
    bcum = jnp.dot(tri_chunk, ls, precision=lax.Precision.HIGHEST, preferred_element_type=F32)
    carry_ref[...] = csum[tb - 1:tb, :]
    lane = lax.broadcasted_iota(jnp.int32, (tb, GATE_LANES), 1)
    out = jnp.where(lane < G_BI, csum, jnp.where(lane < G_BF, v, bcum))
    col_ref[0] = out
    row_ref[0] = out.T[0:16, :]


def _gates(zg3, bias, *, tb):
    b, s, _ = zg3.shape
    return pl.pallas_call(
        functools.partial(_gates_body, tb=tb),
        grid=(b, s // tb),
        in_specs=[
            pl.BlockSpec((1, tb, GATE_LANES), lambda i, j: (i, j, 0)),
            pl.BlockSpec((1, GATE_LANES), lambda i, j: (0, 0)),
        ],
        out_specs=[
            pl.BlockSpec((1, tb, GATE_LANES), lambda i, j: (i, j, 0)),
            pl.BlockSpec((1, 16, tb), lambda i, j: (i, 0, j)),
        ],
        out_shape=[
            jax.ShapeDtypeStruct((b, s, GATE_LANES), F32),
            jax.ShapeDtypeStruct((b, 16, s), F32),
        ],
        scratch_shapes=[pltpu.VMEM((1, GATE_LANES), F32)],
        compiler_params=_cparams(("parallel", "arbitrary")),
        name="gates",
    )(zg3, bias)


def _attn_body(q_ref, k_ref, v_ref, ccol_ref, crow_ref, g_ref, o_ref, m_ref, l_ref, acc_ref, *, tq):
    h = pl.program_id(1)
    qi = pl.program_id(2)
    scale = A_HEAD_DIM ** -0.5
    q = q_ref[...].astype(BF16)
    lane = lax.broadcasted_iota(jnp.int32, (tq, GATE_LANES), 1)
    ct = jnp.sum(jnp.where(lane == h, ccol_ref[0], 0.0), axis=-1, keepdims=True)
    m_ref[...] = jnp.full(m_ref.shape, -jnp.inf, F32)
    l_ref[...] = jnp.zeros_like(l_ref)
    acc_ref[...] = jnp.zeros_like(acc_ref)

    def step(j, masked):
        k0 = pl.multiple_of(j * tq, tq)
        k = k_ref[pl.ds(k0, tq), :].astype(BF16)
        v = v_ref[pl.ds(k0, tq), :].astype(BF16)
        s = _dot_nt(q, k) * scale
        cs = crow_ref[0, pl.ds(h, 1), pl.ds(k0, tq)]
        s = s + (ct - cs)
        if masked:
            r = lax.broadcasted_iota(jnp.int32, (tq, tq), 0)
            c = lax.broadcasted_iota(jnp.int32, (tq, tq), 1)
            s = jnp.where(c <= r, s, -jnp.inf)
        m_prev = m_ref[...]
        m_new = jnp.maximum(m_prev, jnp.max(s, axis=-1, keepdims=True))
        p = jnp.exp(s - m_new)
        alpha = jnp.exp(m_prev - m_new)
        l_ref[...] = alpha * l_ref[...] + jnp.sum(p, axis=-1, keepdims=True)
        acc_ref[...] = alpha * acc_ref[...] + _dot(p.astype(BF16), v)
        m_ref[...] = m_new

    def body(j, carry):
        step(j, False)
        return carry

    lax.fori_loop(0, qi, body, 0)
    step(qi, True)
    o = acc_ref[...] / l_ref[...]
    o = o * lax.rsqrt(jnp.mean(o * o, axis=-1, keepdims=True) + EPS) * g_ref[...]
    o_ref[...] = o.astype(BF16)


def _attention(z2d, gcol, grow, g_a, *, batch, seq, tq):
    nq = seq // tq
    qc, kc, vc = COL_AQ // 128, COL_AK // 128, COL_AV // 128
    return pl.pallas_call(
        functools.partial(_attn_body, tq=tq),
        grid=(batch, A_HEADS, nq),
        in_specs=[
            pl.BlockSpec((tq, 128), lambda b, h, i: (b * nq + i, qc + h)),
            pl.BlockSpec((seq, 128), lambda b, h, i: (b, kc + h)),
            pl.BlockSpec((seq, 128), lambda b, h, i: (b, vc + h)),
            pl.BlockSpec((1, tq, GATE_LANES), lambda b, h, i: (b, i, 0)),
            pl.BlockSpec((1, 16, seq), lambda b, h, i: (b, 0, 0)),
            pl.BlockSpec((1, 128), lambda b, h, i: (0, h)),
        ],
        out_specs=pl.BlockSpec((tq, 128), lambda b, h, i: (b * nq + i, h)),
        out_shape=jax.ShapeDtypeStruct((batch * seq, A_WIDTH), BF16),
        scratch_shapes=[
            pltpu.VMEM((tq, 1), F32),
            pltpu.VMEM((tq, 1), F32),
            pltpu.VMEM((tq, 128), F32),
        ],
        compiler_params=_cparams(("parallel", "parallel", "arbitrary")),
        name="fox_attention",
    )(z2d, z2d, z2d, gcol, grow, g_a)


def _mlstm_body(zb_ref, gcol_ref, grow_ref, cw_ref, g_ref, y_ref, ext_ref, c_ref, m_ref, *, nb):
    k = pl.program_id(0)
    ln = B_CHUNK
    hp = B_HEAD_PAD

    @pl.when(k == 0)
    def _():
        ext_ref[:, 0:8, :] = jnp.zeros((nb, 8, 2 * B_PAD_WIDTH), F32)
        c_ref[...] = jnp.zeros_like(c_ref)
        m_ref[...] = jnp.zeros_like(m_ref)

    first_half = (k % 2) == 0
    r = lax.broadcasted_iota(jnp.int32, (ln, ln), 0)
    c = lax.broadcasted_iota(jnp.int32, (ln, ln), 1)
    causal = c <= r
    lane = lax.broadcasted_iota(jnp.int32, (ln, hp), 1)
    qscale = B_HEAD_DIM ** -0.5

    for b in range(nb):
        ext_ref[b, 8:8 + ln, :] = zb_ref[b, :, 0:2 * B_PAD_WIDTH]
        conv = cw_ref[0:1, :] * ext_ref[b, 5:5 + ln, :]
        for j in range(1, B_CONV):
            conv = conv + cw_ref[j:j + 1, :] * ext_ref[b, 5 + j:5 + j + ln, :]
        ext_ref[b, 0:8, :] = ext_ref[b, ln:ln + 8, :]
        qk = conv * _sigmoid(conv)
        grow = grow_ref[b]
        grow = jnp.where(first_half, grow[:, 0:ln], grow[:, ln:2 * ln])
        gcol = gcol_ref[b]
        for h in range(B_HEADS):
            st = b * B_HEADS + h
            qq = (qk[:, COL_BQ + h * hp:COL_BQ + (h + 1) * hp] * qscale).astype(BF16)
            kk = qk[:, COL_BK + h * hp:COL_BK + (h + 1) * hp]
            vv = zb_ref[b, :, COL_BV + h * hp:COL_BV + (h + 1) * hp]
            vv = jnp.where(lane == B_HEAD_DIM, 1.0, vv).astype(BF16)
            og = zb_ref[b, :, COL_BO + h * hp:COL_BO + (h + 1) * hp]
            bc_col = gcol[:, G_BF + h:G_BF + h + 1]
            li_col = gcol[:, G_BI + h:G_BI + h + 1]
            bc_row = grow[G_BF + h:G_BF + h + 1, :]
            li_row = grow[G_BI + h:G_BI + h + 1, :]
            m = m_ref[st, 0:1, 0:1]

            d = jnp.where(causal, bc_col - bc_row + li_row, -jnp.inf)
            inter = bc_col + m
            m_t = jnp.maximum(inter, jnp.max(d, axis=-1, keepdims=True))
            w_inter = jnp.exp(inter - m_t)
            sc = _dot_nt(qq, kk.astype(BF16)) * jnp.exp(d - m_t)
            cm = c_ref[st]
            num = w_inter * _dot(qq, cm.astype(BF16)) + _dot(sc.astype(BF16), vv)
            den = num[:, B_HEAD_DIM:B_HEAD_DIM + 1]
            hh = num / jnp.maximum(jnp.abs(den), jnp.exp(-m_t))

            b_last = bc_col[ln - 1:ln, :]
            dec = b_last - bc_col + li_col
            m_new = jnp.maximum(b_last + m, jnp.max(dec, axis=0, keepdims=True))
            a = jnp.exp(b_last + m - m_new)
            wk = (jnp.exp(dec - m_new) * kk).astype(BF16)
            c_ref[st] = a * cm + _dot_tn(wk, vv)
            m_ref[st] = jnp.broadcast_to(m_new, (8, 128))

            y = jnp.where(lane < B_HEAD_DIM, _sigmoid(og) * hh, 0.0)
            ms = jnp.sum(y * y, axis=-1, keepdims=True) * (1.0 / B_HEAD_DIM)
            y = y * lax.rsqrt(ms + EPS) * g_ref[:, h * hp:(h + 1) * hp]
            y_ref[b, :, h * hp:(h + 1) * hp] = y.astype(BF16)


def _mlstm(z3, gcol, grow, cw, g_b):
    nb, seq, _ = z3.shape
    nc = seq // B_CHUNK
    return pl.pallas_call(
        functools.partial(_mlstm_body, nb=nb),
        grid=(nc,),
        in_specs=[
            pl.BlockSpec((nb, B_CHUNK, 4 * B_PAD_WIDTH), lambda k: (0, k, 0)),
            pl.BlockSpec((nb, B_CHUNK, GATE_LANES), lambda k: (0, k, 0)),
            pl.BlockSpec((nb, 16, 2 * B_CHUNK), lambda k: (0, 0, k // 2)),
            pl.BlockSpec((B_CONV, 2 * B_PAD_WIDTH), lambda k: (0, 0)),
            pl.BlockSpec((1, B_PAD_WIDTH), lambda k: (0, 0)),
        ],
        out_specs=pl.BlockSpec((nb, B_CHUNK, B_PAD_WIDTH), lambda k: (0, k, 0)),
        out_shape=jax.ShapeDtypeStruct((nb, seq, B_PAD_WIDTH), BF16),
        scratch_shapes=[
            pltpu.VMEM((nb, B_CHUNK + 8, 2 * B_PAD_WIDTH), F32),
            pltpu.VMEM((nb * B_HEADS, B_HEAD_PAD, B_HEAD_PAD), F32),
            pltpu.VMEM((nb * B_HEADS, 8, 128), F32),
        ],
        compiler_params=_cparams(("arbitrary",)),
        name="mlstm",
    )(z3, gcol, grow, cw, g_b)


def _conf_body(ca_ref, cg_ref, dw_ref, db_ref, lg_ref, lb_ref, pw_ref, pb_ref, g_ref, y_ref, ext_ref, *, ts):
    halo = 32

    @pl.when(pl.program_id(1) == 0)
    def _():
        ext_ref[0:halo, :] = jnp.zeros((halo, C_WIDTH), F32)

    ext_ref[halo:halo + ts, :] = ca_ref[0] * _sigmoid(cg_ref[0])
    off = halo - (C_CONV - 1)
    acc = dw_ref[0:1, :] * ext_ref[off:off + ts, :]
    for j in range(1, C_CONV):
        acc = acc + dw_ref[j:j + 1, :] * ext_ref[off + j:off + j + ts, :]
    ext_ref[0:halo, :] = ext_ref[ts:ts + halo, :]
    y = acc + db_ref[...]
    mu = jnp.mean(y, axis=-1, keepdims=True)
    yc = y - mu
    y = yc * lax.rsqrt(jnp.mean(yc * yc, axis=-1, keepdims=True) + LN_EPS)
    y = y * lg_ref[...] + lb_ref[...]
    y = y * _sigmoid(y)
    y = _dot(y.astype(BF16), pw_ref[...]) + pb_ref[...]
    gw = C_WIDTH // C_GROUPS
    for gi in range(C_GROUPS):
        yg = y[:, gi * gw:(gi + 1) * gw]
        yg = yg * lax.rsqrt(jnp.mean(yg * yg, axis=-1, keepdims=True) + EPS) * g_ref[:, gi * gw:(gi + 1) * gw]
        y_ref[0, :, gi * gw:(gi + 1) * gw] = yg.astype(BF16)


def _conformer(z3, dw, db, lg, lb, pw, pb, g_c, *, ts):
    nb, seq, _ = z3.shape
    ca, cg = COL_CA // C_WIDTH, COL_CG // C_WIDTH
    vec = pl.BlockSpec((1, C_WIDTH), lambda b, i: (0, 0))
    return pl.pallas_call(
        functools.partial(_conf_body, ts=ts),
        grid=(nb, seq // ts),
        in_specs=[
            pl.BlockSpec((1, ts, C_WIDTH), lambda b, i: (b, i, ca)),
            pl.BlockSpec((1, ts, C_WIDTH), lambda b, i: (b, i, cg)),
            pl.BlockSpec((C_CONV, C_WIDTH), lambda b, i: (0, 0)),
            vec, vec, vec,
            pl.BlockSpec((C_WIDTH, C_WIDTH), lambda b, i: (0, 0)),
            vec, vec,
        ],
        out_specs=pl.BlockSpec((1, ts, C_WIDTH), lambda b, i: (b, i, 0)),
        out_shape=jax.ShapeDtypeStruct((nb, seq, C_WIDTH), BF16),
        scratch_shapes=[pltpu.VMEM((ts + 32, C_WIDTH), F32)],
        compiler_params=_cparams(("parallel", "arbitrary")),
        name="conformer",
    )(z3, z3, dw, db, lg, lb, pw, pb, g_c)


def _outproj_body(x_ref, ya_ref, yb_ref, yc_ref, wa_ref, wb_ref, wc_ref, o_ref):
    o_ref[...] = (x_ref[...] + _dot(ya_ref[...], wa_ref[...]) + _dot(yb_ref[...], wb_ref[...])
                  + _dot(yc_ref[...], wc_ref[...]))


def _out_proj(x2d, ya, yb, yc, wa, wb, wc, *, tm, tn):
    t, d = x2d.shape
    return pl.pallas_call(
        _outproj_body,
        grid=(t // tm, d // tn),
        in_specs=[
            pl.BlockSpec((tm, tn), lambda i, j: (i, j)),
            pl.BlockSpec((tm, ya.shape[1]), lambda i, j: (i, 0)),
            pl.BlockSpec((tm, yb.shape[1]), lambda i, j: (i, 0)),
            pl.BlockSpec((tm, yc.shape[1]), lambda i, j: (i, 0)),
            pl.BlockSpec((wa.shape[0], tn), lambda i, j: (0, j)),
            pl.BlockSpec((wb.shape[0], tn), lambda i, j: (0, j)),
            pl.BlockSpec((wc.shape[0], tn), lambda i, j: (0, j)),
        ],
        out_specs=pl.BlockSpec((tm, tn), lambda i, j: (i, j)),
        out_shape=jax.ShapeDtypeStruct((t, d), F32),
        compiler_params=_cparams(("parallel", "arbitrary")),
        name="out_proj",
    )(x2d, ya, yb, yc, wa, wb, wc)


def _peerq_body(x_ref, g_ref, wh_ref, wl_ref, q_ref, xn_ref, hi_ref, lo_ref):
    @pl.when(pl.program_id(1) == 0)
    def _():
        xf = x_ref[...]
        ms = jnp.mean(xf * xf, axis=-1, keepdims=True)
        xn = xf * lax.rsqrt(ms + EPS) * g_ref[...]
        hi, lo = _split_bf16(xn)
        hi_ref[...] = hi
        lo_ref[...] = lo
        xn_ref[...] = hi

    q_ref[...] = _dot3(hi_ref[...], lo_ref[...], wh_ref[...], wl_ref[...], _dot)


def _peer_q(x2d, g, wh, wl, *, tm, tn):
    t, d = x2d.shape
    n = wh.shape[1]
    return pl.pallas_call(
        _peerq_body,
        grid=(t // tm, n // tn),
        in_specs=[
            pl.BlockSpec((tm, d), lambda i, j: (i, 0)),
            pl.BlockSpec((1, d), lambda i, j: (0, 0)),
            pl.BlockSpec((d, tn), lambda i, j: (0, j)),
            pl.BlockSpec((d, tn), lambda i, j: (0, j)),
        ],
        out_specs=[
            pl.BlockSpec((tm, tn), lambda i, j: (i, j)),
            pl.BlockSpec((tm, d), lambda i, j: (i, 0)),
        ],
        out_shape=[
            jax.ShapeDtypeStruct((t, n), F32),
            jax.ShapeDtypeStruct((t, d), BF16),
        ],
        scratch_shapes=[pltpu.VMEM((tm, d), BF16), pltpu.VMEM((tm, d), BF16)],
        compiler_params=_cparams(("parallel", "arbitrary")),
        name="peer_q",
    )(x2d, g, wh, wl)


def _extract_top(s, rowid, n_out):
    nrow = s.shape[0]
    vals = []
    for _ in range(n_out):
        m = jnp.max(s, axis=0, keepdims=True)
        idx = jnp.min(jnp.where(s == m, rowid, nrow), axis=0, keepdims=True)
        vals.append(m)
        s = jnp.where(rowid == idx, -jnp.inf, s)
    return vals


def _topk_body(q_ref, keys_ref, s1_ref, s2_ref, tc_ref, *, tt):
    rowid = lax.broadcasted_iota(jnp.int32, (N_KEYS, tt), 0)
    rowid2 = lax.broadcasted_iota(jnp.int32, (P_TOPK * P_TOPK, tt), 0)

    def per_head(h, carry):
        tops = []
        for c, out_ref in ((0, s1_ref), (1, s2_ref)):
            kh, kl = _split_bf16(keys_ref[2 * h + c])
            col = pl.multiple_of((2 * h + c) * P_HALF, P_HALF)
            qh, ql = _split_bf16(q_ref[:, pl.ds(col, P_HALF)])
            st = _dot3(kh, kl, qh, ql, _dot_nt)
            out_ref[h] = st
            tops.append(jnp.concatenate(_extract_top(st, rowid, P_TOPK), axis=0))
        v1, v2 = tops
        cand = jnp.concatenate([v1[a:a + 1, :] + v2 for a in range(P_TOPK)], axis=0)
        best = _extract_top(cand, rowid2, P_TOPK)
        smax = best[0]
        zsum = jnp.zeros_like(smax)
        for bv in best:
            zsum = zsum + jnp.exp(bv - smax)
        tc_ref[pl.ds(h, 1), :] = best[P_TOPK - 1]
        tc_ref[pl.ds(P_HEADS + h, 1), :] = smax + jnp.log(zsum)
        return carry

    lax.fori_loop(0, P_HEADS, per_head, 0)


def _peer_topk(q, keys16, *, tt):
    t = q.shape[0]
    st_shape = jax.ShapeDtypeStruct((P_HEADS, N_KEYS, t), F32)
    st_spec = pl.BlockSpec((P_HEADS, N_KEYS, tt), lambda i: (0, 0, i))
    return pl.pallas_call(
        functools.partial(_topk_body, tt=tt),
        grid=(t // tt,),
        in_specs=[
            pl.BlockSpec((tt, q.shape[1]), lambda i: (i, 0)),
            pl.BlockSpec(keys16.shape, lambda i: (0, 0, 0)),
        ],
        out_specs=[st_spec, st_spec, pl.BlockSpec((2 * P_HEADS, tt), lambda i: (0, i))],
        out_shape=[st_shape, st_shape, jax.ShapeDtypeStruct((2 * P_HEADS, t), F32)],
        compiler_params=_cparams(("parallel",)),
        name="peer_topk",
    )(q, keys16)


def _peer_dense_body(xn_ref, u_ref, vt_ref, s1_ref, s2_ref, tc_ref, o_ref, h_ref, hg_ref, *, tm, te, lc):
    e = pl.program_id(1)

    @pl.when(e == 0)
    def _():
        o_ref[...] = jnp.zeros_like(o_ref)

    h_ref[...] = _dot_nt(u_ref[...], xn_ref[...])
    ng = te // N_KEYS
    nl = tm // lc
    row_base = (e % (8 // ng)) * ng

    def blk(idx, carry):
        ii = idx // nl
        r0 = pl.multiple_of(ii * N_KEYS, N_KEYS)
        c0 = pl.multiple_of((idx % nl) * lc, lc)
        ht = h_ref[pl.ds(r0, N_KEYS), pl.ds(c0, lc)]
        act = 0.5 * ht * (1.0 + lax.erf(ht * (2.0 ** -0.5)))
        gate = jnp.zeros((N_KEYS, lc), F32)
        for h in range(P_HEADS):
            s1row = s1_ref[h, pl.ds(row_base + ii, 1), pl.ds(c0, lc)]
            sm = s2_ref[h, :, pl.ds(c0, lc)] + s1row
            tau = tc_ref[h:h + 1, pl.ds(c0, lc)]
            cz = tc_ref[P_HEADS + h:P_HEADS + h + 1, pl.ds(c0, lc)]
            gate = gate + jnp.where(sm >= tau, jnp.exp(sm - cz), 0.0)
        hg_ref[pl.ds(r0, N_KEYS), pl.ds(c0, lc)] = (act * gate).astype(BF16)
        return carry

    lax.fori_loop(0, ng * nl, blk, 0)
    o_ref[...] += _dot(vt_ref[...], hg_ref[...])


def _peer_dense(xn, u, vt, s1t, s2t, tc, *, tm, te, lc):
    t, d = xn.shape
    ne = u.shape[0]
    ng = te // N_KEYS
    return pl.pallas_call(
        functools.partial(_peer_dense_body, tm=tm, te=te, lc=lc),
        grid=(t // tm, ne // te),
        in_specs=[
            pl.BlockSpec((tm, d), lambda i, e: (i, 0)),
            pl.BlockSpec((te, d), lambda i, e: (e, 0)),
            pl.BlockSpec((d, te), lambda i, e: (0, e)),
            pl.BlockSpec((P_HEADS, 8, tm), lambda i, e: (0, e // (8 // ng), i)),
            pl.BlockSpec((P_HEADS, N_KEYS, tm), lambda i, e: (0, 0, i)),
            pl.BlockSpec((2 * P_HEADS, tm), lambda i, e: (0, i)),
        ],
        out_specs=pl.BlockSpec((d, tm), lambda i, e: (0, i)),
        out_shape=jax.ShapeDtypeStruct((d, t), F32),
        scratch_shapes=[pltpu.VMEM((te, tm), F32), pltpu.VMEM((te, tm), BF16)],
        compiler_params=_cparams(("parallel", "arbitrary")),
        name="peer_dense",
    )(xn, u, vt, s1t, s2t, tc)


def _combine_body(x_ref, pt_ref, g_ref, o_ref, *, final):
    y = x_ref[...] + pt_ref[...].T
    if final:
        y = y * lax.rsqrt(jnp.mean(y * y, axis=-1, keepdims=True) + EPS) * g_ref[...]
    o_ref[...] = y


def _combine(x2d, pt, g, *, tm, final):
    t, d = x2d.shape
    return pl.pallas_call(
        functools.partial(_combine_body, final=final),
        grid=(t // tm,),
        in_specs=[
            pl.BlockSpec((tm, d), lambda i: (i, 0)),
            pl.BlockSpec((d, tm), lambda i: (0, i)),
            pl.BlockSpec((1, d), lambda i: (0, 0)),
        ],
        out_specs=pl.BlockSpec((tm, d), lambda i: (i, 0)),
        out_shape=jax.ShapeDtypeStruct((t, d), F32),
        compiler_params=_cparams(("parallel",)),
        name="combine",
    )(x2d, pt, g)


def _pad_heads(w):
    lead = w.shape[:-1]
    w = w.reshape(lead + (B_HEADS, B_HEAD_DIM))
    w = jnp.pad(w, [(0, 0)] * len(lead) + [(0, 0), (0, B_HEAD_PAD - B_HEAD_DIM)])
    return w.reshape(lead + (B_PAD_WIDTH,))


def _regroup_w_in(w):
    sizes = (A_WIDTH, A_WIDTH, A_WIDTH, A_HEADS, 2 * B_WIDTH, B_WIDTH, B_WIDTH, B_HEADS, B_HEADS, C_WIDTH, C_WIDTH)
    parts, p = [], 0
    for sz in sizes:
        parts.append(w[:, p:p + sz])
        p += sz
    aq, ak, av, af, bqk, bv, bo, bi, bf, ca, cg = parts
    d = w.shape[0]
    main = jnp.concatenate(
        [_pad_heads(bqk[:, :B_WIDTH]), _pad_heads(bqk[:, B_WIDTH:]), _pad_heads(bv), _pad_heads(bo),
         ca, cg, aq, ak, av, jnp.zeros((d, Z_WIDTH - Z_USED), w.dtype)], axis=1)
    gates = jnp.concatenate([af, bi, bf, jnp.zeros((d, GATE_LANES - A_HEADS - 2 * B_HEADS), w.dtype)], axis=1)
    return main.astype(BF16), gates


def _tile_sizes(t, seq):
    return dict(
        tm_in=min(512, t), tn_in=1280,
        tb_gate=min(256, seq),
        tq=min(512, seq),
        ts_conf=min(256, seq),
        tm_out=min(512, t), tn_out=1024,
        tm_q=min(512, t), tn_q=1024,
        tt=min(256, t),
        tm_pd=min(1024, t), te=512, lc=min(256, t),
        tm_cb=min(256, t),
    )


def kernel(x, w_in, fox_f_bias, mlstm_conv_w, mlstm_i_bias, mlstm_f_bias, conf_dw_w, conf_dw_b, conf_ln_g, conf_ln_b, conf_pw_w, conf_pw_b, out_norm_g, w_out, norm1_g, norm2_g, peer_wq, peer_keys, peer_u, peer_v, final_g):
    batch, seq, d = x.shape
    t = batch * seq
    depth = w_in.shape[0]
    ts = _tile_sizes(t, seq)
    xf = x.reshape(t, d)
    row = lambda v: v.reshape(1, -1)

    for l in range(depth):
        w_main, w_gate = _regroup_w_in(w_in[l])
        wgh, wgl = _split_bf16(w_gate)
        z, zg = _in_proj(xf, row(norm1_g[l]), w_main, wgh, wgl, tm=ts["tm_in"], tn=ts["tn_in"])
        bias = jnp.concatenate([fox_f_bias[l], mlstm_i_bias[l], mlstm_f_bias[l],
                                jnp.zeros((GATE_LANES - A_HEADS - 2 * B_HEADS,), F32)]).reshape(1, GATE_LANES)
        gcol, grow = _gates(zg.reshape(batch, seq, GATE_LANES), bias, tb=ts["tb_gate"])
        g_a = row(out_norm_g[l, :A_WIDTH])
        g_b = row(_pad_heads(out_norm_g[l, A_WIDTH:A_WIDTH + B_WIDTH]))
        g_c = row(out_norm_g[l, A_WIDTH + B_WIDTH:])
        z3 = z.reshape(batch, seq, Z_WIDTH)
        ya = _attention(z, gcol, grow, g_a, batch=batch, seq=seq, tq=ts["tq"])
        cw = jnp.concatenate([_pad_heads(mlstm_conv_w[l, :, :B_WIDTH]), _pad_heads(mlstm_conv_w[l, :, B_WIDTH:])], axis=1)
        yb = _mlstm(z3, gcol, grow, cw, g_b)
        yc = _conformer(z3, conf_dw_w[l], row(conf_dw_b[l]), row(conf_ln_g[l]), row(conf_ln_b[l]),
                        conf_pw_w[l].astype(BF16), row(conf_pw_b[l]), g_c, ts=ts["ts_conf"])
        wo = w_out[l]
        wa = wo[:A_WIDTH].astype(BF16)
        wb = jnp.pad(wo[A_WIDTH:A_WIDTH + B_WIDTH].reshape(B_HEADS, B_HEAD_DIM, d),
                     ((0, 0), (0, B_HEAD_PAD - B_HEAD_DIM), (0, 0))).reshape(B_PAD_WIDTH, d).astype(BF16)
        wc = wo[A_WIDTH + B_WIDTH:].astype(BF16)
        xf = _out_proj(xf, ya, yb.reshape(t, B_PAD_WIDTH), yc.reshape(t, C_WIDTH), wa, wb, wc,
                       tm=ts["tm_out"], tn=ts["tn_out"])

        wqh, wql = _split_bf16(peer_wq[l])
        q, xn = _peer_q(xf, row(norm2_g[l]), wqh, wql, tm=ts["tm_q"], tn=ts["tn_q"])
        keys16 = peer_keys[l].reshape(2 * P_HEADS, N_KEYS, P_HALF)
        s1t, s2t, tc = _peer_topk(q, keys16, tt=ts["tt"])
        u = peer_u[l].astype(BF16)
        vt = peer_v[l].T.astype(BF16)
        pt = _peer_dense(xn, u, vt, s1t, s2t, tc, tm=ts["tm_pd"], te=ts["te"], lc=ts["lc"])
        xf = _combine(xf, pt, row(final_g), tm=ts["tm_cb"], final=(l == depth - 1))

    return xf.reshape(batch, seq, d)
```

```python
import functools

import jax
import jax.numpy as jnp
from jax import lax
from jax.experimental import pallas as pl
from jax.experimental.pallas import tpu as pltpu

F32 = jnp.float32
BF16 = jnp.bfloat16

EPS = 1e-6
LN_EPS = 1e-5

A_HEADS = 6
A_HEAD_DIM = 128
A_WIDTH = A_HEADS * A_HEAD_DIM
B_HEADS = 4
B_HEAD_DIM = 192
B_HEAD_PAD = 256
B_WIDTH = B_HEADS * B_HEAD_DIM
B_PAD_WIDTH = B_HEADS * B_HEAD_PAD
B_CHUNK = 64
B_CONV = 4
C_GROUPS = 4
C_WIDTH = 512
C_CONV = 31
P_HEADS = 8
P_HALF = 128
N_KEYS = 128
P_TOPK = 16
GATE_LANES = 128

COL_BQ = 0
COL_BK = B_PAD_WIDTH
COL_BV = 2 * B_PAD_WIDTH
COL_BO = 3 * B_PAD_WIDTH
COL_CA = 4 * B_PAD_WIDTH
COL_CG = COL_CA + C_WIDTH
COL_AQ = COL_CG + C_WIDTH
COL_AK = COL_AQ + A_WIDTH
COL_AV = COL_AK + A_WIDTH
Z_USED = COL_AV + A_WIDTH
Z_WIDTH = 7680
G_AF = 0
G_BI = A_HEADS
G_BF = A_HEADS + B_HEADS

VMEM_LIMIT = 60 * 1024 * 1024


def _cparams(sem):
    return pltpu.CompilerParams(dimension_semantics=sem, vmem_limit_bytes=VMEM_LIMIT)


def _split_bf16(a):
    hi = a.astype(BF16)
    lo = (a - hi.astype(F32)).astype(BF16)
    return hi, lo


def _dot(a, b):
    return jnp.dot(a, b, preferred_element_type=F32)


def _dot_nt(a, b):
    return lax.dot_general(a, b, (((1,), (1,)), ((), ())), preferred_element_type=F32)


def _dot_tn(a, b):
    return lax.dot_general(a, b, (((0,), (0,)), ((), ())), preferred_element_type=F32)


def _dot3(ah, al, bh, bl, dot):
    return dot(ah, bh) + dot(ah, bl) + dot(al, bh)


def _sigmoid(x):
    return 1.0 / (1.0 + jnp.exp(-x))


def _inproj_body(x_ref, g_ref, w_ref, wgh_ref, wgl_ref, z_ref, zg_ref, xn_ref):
    @pl.when(pl.program_id(1) == 0)
    def _():
        xf = x_ref[...]
        ms = jnp.mean(xf * xf, axis=-1, keepdims=True)
        xn = xf * lax.rsqrt(ms + EPS) * g_ref[...]
        hi, lo = _split_bf16(xn)
        xn_ref[...] = hi
        zg_ref[...] = _dot3(hi, lo, wgh_ref[...], wgl_ref[...], _dot)

    z_ref[...] = _dot(xn_ref[...], w_ref[...])


def _in_proj(x2d, g, w, wgh, wgl, *, tm, tn):
    t, d = x2d.shape
    n = w.shape[1]
    return pl.pallas_call(
        _inproj_body,
        grid=(t // tm, n // tn),
        in_specs=[
            pl.BlockSpec((tm, d), lambda i, j: (i, 0)),
            pl.BlockSpec((1, d), lambda i, j: (0, 0)),
            pl.BlockSpec((d, tn), lambda i, j: (0, j)),
            pl.BlockSpec((d, GATE_LANES), lambda i, j: (0, 0)),
            pl.BlockSpec((d, GATE_LANES), lambda i, j: (0, 0)),
        ],
        out_specs=[
            pl.BlockSpec((tm, tn), lambda i, j: (i, j)),
            pl.BlockSpec((tm, GATE_LANES), lambda i, j: (i, 0)),
        ],
        out_shape=[
            jax.ShapeDtypeStruct((t, n), F32),
            jax.ShapeDtypeStruct((t, GATE_LANES), F32),
        ],
        scratch_shapes=[pltpu.VMEM((tm, d), BF16)],
        compiler_params=_cparams(("parallel", "arbitrary")),
        name="in_proj",
    )(x2d, g, w, wgh, wgl)


def _gates_body(zg_ref, bias_ref, col_ref, row_ref, carry_ref, *, tb):
    @pl.when(pl.program_id(1) == 0)
    def _():
        carry_ref[...] = jnp.zeros_like(carry_ref)

    v = zg_ref[0] + bias_ref[...]
    ls = jnp.minimum(v, 0.0) - jnp.log1p(jnp.exp(-jnp.abs(v)))
    r = lax.broadcasted_iota(jnp.int32, (tb, tb), 0)
    c = lax.broadcasted_iota(jnp.int32, (tb, tb), 1)
    tri = (c <= r).astype(F32)
    tri_chunk = jnp.where((r // B_CHUNK) == (c // B_CHUNK), tri, 0.0)
    csum = jnp.dot(tri, ls, precision=lax.Precision.HIGHEST, preferred_element_type=F32) + carry_ref[...]
    bcum = jnp.dot(tri_chunk, ls, precision=lax.Precision.HIGHEST, preferred_element_type=F32)
    carry_ref[...] = csum[tb - 1:tb, :]
    lane = lax.broadcasted_iota(jnp.int32, (tb, GATE_LANES), 1)
    out = jnp.where(lane < G_BI, csum, jnp.where(lane < G_BF, v, bcum))
    col_ref[0] = out
    row_ref[0] = out.T[0:16, :]


def _gates(zg3, bias, *, tb):
    b, s, _ = zg3.shape
    return pl.pallas_call(
        functools.partial(_gates_body, tb=tb),
        grid=(b, s // tb),
        in_specs=[
            pl.BlockSpec((1, tb, GATE_LANES), lambda i, j: (i, j, 0)),
            pl.BlockSpec((1, GATE_LANES), lambda i, j: (0, 0)),
        ],
        out_specs=[
            pl.BlockSpec((1, tb, GATE_LANES), lambda i, j: (i, j, 0)),
            pl.BlockSpec((1, 16, tb), lambda i, j: (i, 0, j)),
        ],
        out_shape=[
            jax.ShapeDtypeStruct((b, s, GATE_LANES), F32),
            jax.ShapeDtypeStruct((b, 16, s), F32),
        ],
        scratch_shapes=[pltpu.VMEM((1, GATE_LANES), F32)],
        compiler_params=_cparams(("parallel", "arbitrary")),
        name="gates",
    )(zg3, bias)


def _attn_body(q_ref, k_ref, v_ref, ccol_ref, crow_ref, g_ref, o_ref, m_ref, l_ref, acc_ref, *, tq):
    h = pl.program_id(1)
    qi = pl.program_id(2)
    scale = A_HEAD_DIM ** -0.5
    q = q_ref[...].astype(BF16)
    lane = lax.broadcasted_iota(jnp.int32, (tq, GATE_LANES), 1)
    ct = jnp.sum(jnp.where(lane == h, ccol_ref[0], 0.0), axis=-1, keepdims=True)
    m_ref[...] = jnp.full(m_ref.shape, -jnp.inf, F32)
    l_ref[...] = jnp.zeros_like(l_ref)
    acc_ref[...] = jnp.zeros_like(acc_ref)

    def step(j, masked):
        k0 = pl.multiple_of(j * tq, tq)
        k = k_ref[pl.ds(k0, tq), :].astype(BF16)
        v = v_ref[pl.ds(k0, tq), :].astype(BF16)
        s = _dot_nt(q, k) * scale
        cs = crow_ref[0, pl.ds(h, 1), pl.ds(k0, tq)]
        s = s + (ct - cs)
        if masked:
            r = lax.broadcasted_iota(jnp.int32, (tq, tq), 0)
            c = lax.broadcasted_iota(jnp.int32, (tq, tq), 1)
            s = jnp.where(c <= r, s, -jnp.inf)
        m_prev = m_ref[...]
        m_new = jnp.maximum(m_prev, jnp.max(s, axis=-1, keepdims=True))
        p = jnp.exp(s - m_new)
        alpha = jnp.exp(m_prev - m_new)
        l_ref[...] = alpha * l_ref[...] + jnp.sum(p, axis=-1, keepdims=True)
        acc_ref[...] = alpha * acc_ref[...] + _dot(p.astype(BF16), v)
        m_ref[...] = m_new

    def body(j, carry):
        step(j, False)
        return carry

    lax.fori_loop(0, qi, body, 0)
    step(qi, True)
    o = acc_ref[...] / l_ref[...]
    o = o * lax.rsqrt(jnp.mean(o * o, axis=-1, keepdims=True) + EPS) * g_ref[...]
    o_ref[...] = o.astype(BF16)


def _attention(z2d, gcol, grow, g_a, *, batch, seq, tq):
    nq = seq // tq
    qc, kc, vc = COL_AQ // 128, COL_AK // 128, COL_AV // 128
    return pl.pallas_call(
        functools.partial(_attn_body, tq=tq),
        grid=(batch, A_HEADS, nq),
        in_specs=[
            pl.BlockSpec((tq, 128), lambda b, h, i: (b * nq + i, qc + h)),
            pl.BlockSpec((seq, 128), lambda b, h, i: (b, kc + h)),
            pl.BlockSpec((seq, 128), lambda b, h, i: (b, vc + h)),
            pl.BlockSpec((1, tq, GATE_LANES), lambda b, h, i: (b, i, 0)),
            pl.BlockSpec((1, 16, seq), lambda b, h, i: (b, 0, 0)),
            pl.BlockSpec((1, 128), lambda b, h, i: (0, h)),
        ],
        out_specs=pl.BlockSpec((tq, 128), lambda b, h, i: (b * nq + i, h)),
        out_shape=jax.ShapeDtypeStruct((batch * seq, A_WIDTH), BF16),
        scratch_shapes=[
            pltpu.VMEM((tq, 1), F32),
            pltpu.VMEM((tq, 1), F32),
            pltpu.VMEM((tq, 128), F32),
        ],
        compiler_params=_cparams(("parallel", "parallel", "arbitrary")),
        name="fox_attention",
    )(z2d, z2d, z2d, gcol, grow, g_a)


def _mlstm_body(zb_ref, gcol_ref, grow_ref, cw_ref, g_ref, y_ref, ext_ref, c_ref, m_ref, *, nb):
    k = pl.program_id(0)
    ln = B_CHUNK
    hp = B_HEAD_PAD

    @pl.when(k == 0)
    def _():
        ext_ref[:, 0:8, :] = jnp.zeros((nb, 8, 2 * B_PAD_WIDTH), F32)
        c_ref[...] = jnp.zeros_like(c_ref)
        m_ref[...] = jnp.zeros_like(m_ref)

    first_half = (k % 2) == 0
    r = lax.broadcasted_iota(jnp.int32, (ln, ln), 0)
    c = lax.broadcasted_iota(jnp.int32, (ln, ln), 1)
    causal = c <= r
    lane = lax.broadcasted_iota(jnp.int32, (ln, hp), 1)
    qscale = B_HEAD_DIM ** -0.5

    for b in range(nb):
        ext_ref[b, 8:8 + ln, :] = zb_ref[b, :, 0:2 * B_PAD_WIDTH]
        conv = cw_ref[0:1, :] * ext_ref[b, 5:5 + ln, :]
        for j in range(1, B_CONV):
            conv = conv + cw_ref[j:j + 1, :] * ext_ref[b, 5 + j:5 + j + ln, :]
        ext_ref[b, 0:8, :] = ext_ref[b, ln:ln + 8, :]
        qk = conv * _sigmoid(conv)
        grow = grow_ref[b]
        grow = jnp.where(first_half, grow[:, 0:ln], grow[:, ln:2 * ln])
        gcol = gcol_ref[b]
        for h in range(B_HEADS):
            st = b * B_HEADS + h
            qq = (qk[:, COL_BQ + h * hp:COL_BQ + (h + 1) * hp] * qscale).astype(BF16)
            kk = qk[:, COL_BK + h * hp:COL_BK + (h + 1) * hp]
            vv = zb_ref[b, :, COL_BV + h * hp:COL_BV + (h + 1) * hp]
            vv = jnp.where(lane == B_HEAD_DIM, 1.0, vv).astype(BF16)
            og = zb_ref[b, :, COL_BO + h * hp:COL_BO + (h + 1) * hp]
            bc_col = gcol[:, G_BF + h:G_BF + h + 1]
            li_col = gcol[:, G_BI + h:G_BI + h + 1]
            bc_row = grow[G_BF + h:G_BF + h + 1, :]
            li_row = grow[G_BI + h:G_BI + h + 1, :]
            m = m_ref[st, 0:1, 0:1]

            d = jnp.where(causal, bc_col - bc_row + li_row, -jnp.inf)
            inter = bc_col + m
            m_t = jnp.maximum(inter, jnp.max(d, axis=-1, keepdims=True))
            w_inter = jnp.exp(inter - m_t)
            sc = _dot_nt(qq, kk.astype(BF16)) * jnp.exp(d - m_t)
            cm = c_ref[st]
            num = w_inter * _dot(qq, cm.astype(BF16)) + _dot(sc.astype(BF16), vv)
            den = num[:, B_HEAD_DIM:B_HEAD_DIM + 1]
            hh = num / jnp.maximum(jnp.abs(den), jnp.exp(-m_t))

            b_last = bc_col[ln - 1:ln, :]
            dec = b_last - bc_col + li_col
            m_new = jnp.maximum(b_last + m, jnp.max(dec, axis=0, keepdims=True))
            a = jnp.exp(b_last + m - m_new)
            wk = (jnp.exp(dec - m_new) * kk).astype(BF16)
            c_ref[st] = a * cm + _dot_tn(wk, vv)
            m_ref[st] = jnp.broadcast_to(m_new, (8, 128))

            y = jnp.where(lane < B_HEAD_DIM, _sigmoid(og) * hh, 0.0)
            ms = jnp.sum(y * y, axis=-1, keepdims=True) * (1.0 / B_HEAD_DIM)
            y = y * lax.rsqrt(ms + EPS) * g_ref[:, h * hp:(h + 1) * hp]
            y_ref[b, :, h * hp:(h + 1) * hp] = y.astype(BF16)


def _mlstm(z3, gcol, grow, cw, g_b):
    nb, seq, _ = z3.shape
    nc = seq // B_CHUNK
    return pl.pallas_call(
        functools.partial(_mlstm_body, nb=nb),
        grid=(nc,),
        in_specs=[
            pl.BlockSpec((nb, B_CHUNK, 4 * B_PAD_WIDTH), lambda k: (0, k, 0)),
            pl.BlockSpec((nb, B_CHUNK, GATE_LANES), lambda k: (0, k, 0)),
            pl.BlockSpec((nb, 16, 2 * B_CHUNK), lambda k: (0, 0, k // 2)),
            pl.BlockSpec((B_CONV, 2 * B_PAD_WIDTH), lambda k: (0, 0)),
            pl.BlockSpec((1, B_PAD_WIDTH), lambda k: (0, 0)),
        ],
        out_specs=pl.BlockSpec((nb, B_CHUNK, B_PAD_WIDTH), lambda k: (0, k, 0)),
        out_shape=jax.ShapeDtypeStruct((nb, seq, B_PAD_WIDTH), BF16),
        scratch_shapes=[
            pltpu.VMEM((nb, B_CHUNK + 8, 2 * B_PAD_WIDTH), F32),
            pltpu.VMEM((nb * B_HEADS, B_HEAD_PAD, B_HEAD_PAD), F32),
            pltpu.VMEM((nb * B_HEADS, 8, 128), F32),
        ],
        compiler_params=_cparams(("arbitrary",)),
        name="mlstm",
    )(z3, gcol, grow, cw, g_b)


def _conf_body(ca_ref, cg_ref, dw_ref, db_ref, lg_ref, lb_ref, pw_ref, pb_ref, g_ref, y_ref, ext_ref, *, ts):
    halo = 32

    @pl.when(pl.program_id(1) == 0)
    def _():
        ext_ref[0:halo, :] = jnp.zeros((halo, C_WIDTH), F32)

    ext_ref[halo:halo + ts, :] = ca_ref[0] * _sigmoid(cg_ref[0])
    off = halo - (C_CONV - 1)
    acc = dw_ref[0:1, :] * ext_ref[off:off + ts, :]
    for j in range(1, C_CONV):
        acc = acc + dw_ref[j:j + 1, :] * ext_ref[off + j:off + j + ts, :]
    ext_ref[0:halo, :] = ext_ref[ts:ts + halo, :]
    y = acc + db_ref[...]
    mu = jnp.mean(y, axis=-1, keepdims=True)
    yc = y - mu
    y = yc * lax.rsqrt(jnp.mean(yc * yc, axis=-1, keepdims=True) + LN_EPS)
    y = y * lg_ref[...] + lb_ref[...]
    y = y * _sigmoid(y)
    y = _dot(y.astype(BF16), pw_ref[...]) + pb_ref[...]
    gw = C_WIDTH // C_GROUPS
    for gi in range(C_GROUPS):
        yg = y[:, gi * gw:(gi + 1) * gw]
        yg = yg * lax.rsqrt(jnp.mean(yg * yg, axis=-1, keepdims=True) + EPS) * g_ref[:, gi * gw:(gi + 1) * gw]
        y_ref[0, :, gi * gw:(gi + 1) * gw] = yg.astype(BF16)


def _conformer(z3, dw, db, lg, lb, pw, pb, g_c, *, ts):
    nb, seq, _ = z3.shape
    ca, cg = COL_CA // C_WIDTH, COL_CG // C_WIDTH
    vec = pl.BlockSpec((1, C_WIDTH), lambda b, i: (0, 0))
    return pl.pallas_call(
        functools.partial(_conf_body, ts=ts),
        grid=(nb, seq // ts),
        in_specs=[
            pl.BlockSpec((1, ts, C_WIDTH), lambda b, i: (b, i, ca)),
            pl.BlockSpec((1, ts, C_WIDTH), lambda b, i: (b, i, cg)),
            pl.BlockSpec((C_CONV, C_WIDTH), lambda b, i: (0, 0)),
            vec, vec, vec,
            pl.BlockSpec((C_WIDTH, C_WIDTH), lambda b, i: (0, 0)),
            vec, vec,
        ],
        out_specs=pl.BlockSpec((1, ts, C_WIDTH), lambda b, i: (b, i, 0)),
        out_shape=jax.ShapeDtypeStruct((nb, seq, C_WIDTH), BF16),
        scratch_shapes=[pltpu.VMEM((ts + 32, C_WIDTH), F32)],
        compiler_params=_cparams(("parallel", "arbitrary")),
        name="conformer",
    )(z3, z3, dw, db, lg, lb, pw, pb, g_c)


def _outproj_body(x_ref, ya_ref, yb_ref, yc_ref, wa_ref, wb_ref, wc_ref, o_ref):
    o_ref[...] = (x_ref[...] + _dot(ya_ref[...], wa_ref[...]) + _dot(yb_ref[...], wb_ref[...])
                  + _dot(yc_ref[...], wc_ref[...]))


def _out_proj(x2d, ya, yb, yc, wa, wb, wc, *, tm, tn):
    t, d = x2d.shape
    return pl.pallas_call(
        _outproj_body,
        grid=(t // tm, d // tn),
        in_specs=[
            pl.BlockSpec((tm, tn), lambda i, j: (i, j)),
            pl.BlockSpec((tm, ya.shape[1]), lambda i, j: (i, 0)),
            pl.BlockSpec((tm, yb.shape[1]), lambda i, j: (i, 0)),
            pl.BlockSpec((tm, yc.shape[1]), lambda i, j: (i, 0)),
            pl.BlockSpec((wa.shape[0], tn), lambda i, j: (0, j)),
            pl.BlockSpec((wb.shape[0], tn), lambda i, j: (0, j)),
            pl.BlockSpec((wc.shape[0], tn), lambda i, j: (0, j)),
        ],
        out_specs=pl.BlockSpec((tm, tn), lambda i, j: (i, j)),
        out_shape=jax.ShapeDtypeStruct((t, d), F32),
        compiler_params=_cparams(("parallel", "arbitrary")),
        name="out_proj",
    )(x2d, ya, yb, yc, wa, wb, wc)


def _peerq_body(x_ref, g_ref, wh_ref, wl_ref, q_ref, xn_ref, hi_ref, lo_ref):
    @pl.when(pl.program_id(1) == 0)
    def _():
        xf = x_ref[...]
        ms = jnp.mean(xf * xf, axis=-1, keepdims=True)
        xn = xf * lax.rsqrt(ms + EPS) * g_ref[...]
        hi, lo = _split_bf16(xn)
        hi_ref[...] = hi
        lo_ref[...] = lo
        xn_ref[...] = hi

    q_ref[...] = _dot3(hi_ref[...], lo_ref[...], wh_ref[...], wl_ref[...], _dot)


def _peer_q(x2d, g, wh, wl, *, tm, tn):
    t, d = x2d.shape
    n = wh.shape[1]
    return pl.pallas_call(
        _peerq_body,
        grid=(t // tm, n // tn),
        in_specs=[
            pl.BlockSpec((tm, d), lambda i, j: (i, 0)),
            pl.BlockSpec((1, d), lambda i, j: (0, 0)),
            pl.BlockSpec((d, tn), lambda i, j: (0, j)),
            pl.BlockSpec((d, tn), lambda i, j: (0, j)),
        ],
        out_specs=[
            pl.BlockSpec((tm, tn), lambda i, j: (i, j)),
            pl.BlockSpec((tm, d), lambda i, j: (i, 0)),
        ],
        out_shape=[
            jax.ShapeDtypeStruct((t, n), F32),
            jax.ShapeDtypeStruct((t, d), BF16),
        ],
        scratch_shapes=[pltpu.VMEM((tm, d), BF16), pltpu.VMEM((tm, d), BF16)],
        compiler_params=_cparams(("parallel", "arbitrary")),
        name="peer_q",
    )(x2d, g, wh, wl)


def _extract_top(s, rowid, n_out):
    nrow = s.shape[0]
    rank = jnp.full(s.shape, float(n_out), F32)
    vals = []
    for it in range(n_out):
        m = jnp.max(s, axis=0, keepdims=True)
        idx = jnp.min(jnp.where(s == m, rowid, float(nrow)), axis=0, keepdims=True)
        hit = rowid == idx
        vals.append(m)
        rank = jnp.where(hit, float(it), rank)
        s = jnp.where(hit, -jnp.inf, s)
    return vals, rank, s


def _topk_body(q_ref, keys_ref, c1_ref, w1_ref, r2_ref, e2_ref, *, tt):
    rowid = lax.broadcasted_iota(jnp.int32, (N_KEYS, tt), 0).astype(F32)
    rowid2 = lax.broadcasted_iota(jnp.int32, (P_TOPK * P_TOPK, tt), 0).astype(F32)

    def per_head(h, carry):
        scores, tops, ranks = [], [], []
        for c in range(2):
            kh, kl = _split_bf16(keys_ref[2 * h + c])
            col = pl.multiple_of((2 * h + c) * P_HALF, P_HALF)
            qh, ql = _split_bf16(q_ref[:, pl.ds(col, P_HALF)])
            st = _dot3(kh, kl, qh, ql, _dot_nt)
            vals, rank, _ = _extract_top(st, rowid, P_TOPK)
            scores.append(st)
            tops.append(jnp.concatenate(vals, axis=0))
            ranks.append(rank)
        v1, v2 = tops
        cand = jnp.concatenate([v1[a:a + 1, :] + v2 for a in range(P_TOPK)], axis=0)
        _, _, left = _extract_top(cand, rowid2, P_TOPK)
        picked = left == -jnp.inf
        ev1 = jnp.exp(v1 - v1[0:1, :])
        ev2 = jnp.exp(v2 - v2[0:1, :])
        prod = jnp.concatenate([ev1[a:a + 1, :] * ev2 for a in range(P_TOPK)], axis=0)
        zsum = jnp.sum(jnp.where(picked, prod, 0.0), axis=0, keepdims=True)
        npick = jnp.where(picked, 1.0, 0.0)
        cnt = jnp.zeros((N_KEYS, tt), F32)
        for a in range(P_TOPK):
            n_a = jnp.sum(npick[a * P_TOPK:(a + 1) * P_TOPK, :], axis=0, keepdims=True)
            cnt = jnp.where(ranks[0] == float(a), n_a, cnt)
        c1_ref[h] = cnt
        w1_ref[h] = jnp.exp(scores[0] - v1[0:1, :]) / zsum
        r2_ref[h] = ranks[1].astype(BF16)
        e2_ref[h] = jnp.exp(scores[1] - v2[0:1, :]).astype(BF16)
        return carry

    lax.fori_loop(0, P_HEADS, per_head, 0)


def _peer_topk(q, keys16, *, tt):
    t = q.shape[0]
    st_shape = jax.ShapeDtypeStruct((P_HEADS, N_KEYS, t), F32)
    st_shape_b = jax.ShapeDtypeStruct((P_HEADS, N_KEYS, t), BF16)
    st_spec = pl.BlockSpec((P_HEADS, N_KEYS, tt), lambda i: (0, 0, i))
    return pl.pallas_call(
        functools.partial(_topk_body, tt=tt),
        grid=(t // tt,),
        in_specs=[
            pl.BlockSpec((tt, q.shape[1]), lambda i: (i, 0)),
            pl.BlockSpec(keys16.shape, lambda i: (0, 0, 0)),
        ],
        out_specs=[st_spec] * 4,
        out_shape=[st_shape, st_shape, st_shape_b, st_shape_b],
        compiler_params=_cparams(("parallel",)),
        name="peer_topk",
    )(q, keys16)


def _peer_dense_body(xn_ref, u_ref, vt_ref, c1_ref, w1_ref, r2_ref, e2_ref, o_ref, c1s_ref, w1s_ref, *hg_refs, tm, te, tc, rb):
    e = pl.program_id(1)

    @pl.when(e == 0)
    def _():
        o_ref[...] = jnp.zeros_like(o_ref)

    ng = te // N_KEYS
    nchunk = tm // tc
    which = e % (8 // ng)

    for src, dst in ((c1_ref, c1s_ref), (w1_ref, w1s_ref)):
        rows = src[:, 0:ng, :]
        for g in range(1, 8 // ng):
            rows = jnp.where(which == g, src[:, g * ng:(g + 1) * ng, :], rows)
        dst[:, 0:ng, :] = rows

    def bf16_row(ref, h, ii, t0):
        return ref[h, ii:ii + 1, t0:t0 + tc].astype(BF16)

    def stage_a(ci):
        t0 = ci * tc
        return _dot_nt(u_ref[...], xn_ref[t0:t0 + tc, :])

    def stage_b(ci, hts):
        t0 = ci * tc
        for ii in range(ng):
            rows = [(bf16_row(c1s_ref, h, ii, t0), bf16_row(w1s_ref, h, ii, t0)) for h in range(P_HEADS)]
            for j0 in range(0, N_KEYS, rb):
                r0 = ii * N_KEYS + j0
                ht = hts[r0:r0 + rb, :]
                act = ((0.5 * ht) * (1.0 + lax.erf(ht * (2.0 ** -0.5)))).astype(BF16)
                gate = jnp.zeros((rb, tc), BF16)
                for h in range(P_HEADS):
                    cnt, w1 = rows[h]
                    gate = gate + jnp.where(r2_ref[h, j0:j0 + rb, t0:t0 + tc] < cnt,
                                            e2_ref[h, j0:j0 + rb, t0:t0 + tc] * w1, jnp.zeros((), BF16))
                hg_refs[ci][r0:r0 + rb, :] = act * gate

    def stage_c(ci):
        t0 = ci * tc
        o_ref[:, t0:t0 + tc] += _dot(vt_ref[...], hg_refs[ci][...])

    ahead = 2
    hts = {ci: stage_a(ci) for ci in range(min(ahead, nchunk))}
    for ci in range(nchunk):
        if ci + ahead < nchunk:
            hts[ci + ahead] = stage_a(ci + ahead)
        stage_b(ci, hts.pop(ci))
        stage_c(ci)


def _peer_dense(xn, u, vt, c1, w1, r2, e2, *, tm, te, tc, rb):
    t, d = xn.shape
    ne = u.shape[0]
    ng = te // N_KEYS
    once = pl.Buffered(1)
    row_spec = pl.BlockSpec((P_HEADS, 8, tm), lambda i, e: (0, e // (8 // ng), i))
    full_spec = pl.BlockSpec((P_HEADS, N_KEYS, tm), lambda i, e: (0, 0, i), pipeline_mode=once)
    return pl.pallas_call(
        functools.partial(_peer_dense_body, tm=tm, te=te, tc=tc, rb=rb),
        grid=(t // tm, ne // te),
        in_specs=[
            pl.BlockSpec((tm, d), lambda i, e: (i, 0), pipeline_mode=once),
            pl.BlockSpec((te, d), lambda i, e: (e, 0)),
            pl.BlockSpec((d, te), lambda i, e: (0, e)),
            row_spec, row_spec, full_spec, full_spec,
        ],
        out_specs=pl.BlockSpec((d, tm), lambda i, e: (0, i)),
        out_shape=jax.ShapeDtypeStruct((d, t), F32),
        scratch_shapes=[pltpu.VMEM((P_HEADS, 8, tm), F32), pltpu.VMEM((P_HEADS, 8, tm), F32)]
        + [pltpu.VMEM((te, tc), BF16)] * (tm // tc),
        compiler_params=_cparams(("parallel", "arbitrary")),
        name="peer_dense",
    )(xn, u, vt, c1, w1, r2, e2)


def _combine_body(x_ref, pt_ref, g_ref, o_ref, *, final):
    y = x_ref[...] + pt_ref[...].T
    if final:
        y = y * lax.rsqrt(jnp.mean(y * y, axis=-1, keepdims=True) + EPS) * g_ref[...]
    o_ref[...] = y


def _combine(x2d, pt, g, *, tm, final):
    t, d = x2d.shape
    return pl.pallas_call(
        functools.partial(_combine_body, final=final),
        grid=(t // tm,),
        in_specs=[
            pl.BlockSpec((tm, d), lambda i: (i, 0)),
            pl.BlockSpec((d, tm), lambda i: (0, i)),
            pl.BlockSpec((1, d), lambda i: (0, 0)),
        ],
        out_specs=pl.BlockSpec((tm, d), lambda i: (i, 0)),
        out_shape=jax.ShapeDtypeStruct((t, d), F32),
        compiler_params=_cparams(("parallel",)),
        name="combine",
    )(x2d, pt, g)


def _pad_heads(w):
    lead = w.shape[:-1]
    w = w.reshape(lead + (B_HEADS, B_HEAD_DIM))
    w = jnp.pad(w, [(0, 0)] * len(lead) + [(0, 0), (0, B_HEAD_PAD - B_HEAD_DIM)])
    return w.reshape(lead + (B_PAD_WIDTH,))


def _regroup_w_in(w):
    sizes = (A_WIDTH, A_WIDTH, A_WIDTH, A_HEADS, 2 * B_WIDTH, B_WIDTH, B_WIDTH, B_HEADS, B_HEADS, C_WIDTH, C_WIDTH)
    parts, p = [], 0
    for sz in sizes:
        parts.append(w[:, p:p + sz])
        p += sz
    aq, ak, av, af, bqk, bv, bo, bi, bf, ca, cg = parts
    d = w.shape[0]
    main = jnp.concatenate(
        [_pad_heads(bqk[:, :B_WIDTH]), _pad_heads(bqk[:, B_WIDTH:]), _pad_heads(bv), _pad_heads(bo),
         ca, cg, aq, ak, av, jnp.zeros((d, Z_WIDTH - Z_USED), w.dtype)], axis=1)
    gates = jnp.concatenate([af, bi, bf, jnp.zeros((d, GATE_LANES - A_HEADS - 2 * B_HEADS), w.dtype)], axis=1)
    return main.astype(BF16), gates


def _tile_sizes(t, seq):
    return dict(
        tm_in=min(512, t), tn_in=1280,
        tb_gate=min(256, seq),
        tq=min(512, seq),
        ts_conf=min(256, seq),
        tm_out=min(512, t), tn_out=1024,
        tm_q=min(512, t), tn_q=1024,
        tt=min(256, t),
        tm_pd=min(1024, t), te=512, tc_pd=min(256, t), rb_pd=32,
        tm_cb=min(256, t),
    )


def kernel(x, w_in, fox_f_bias, mlstm_conv_w, mlstm_i_bias, mlstm_f_bias, conf_dw_w, conf_dw_b, conf_ln_g, conf_ln_b, conf_pw_w, conf_pw_b, out_norm_g, w_out, norm1_g, norm2_g, peer_wq, peer_keys, peer_u, peer_v, final_g):
    batch, seq, d = x.shape
    t = batch * seq
    depth = w_in.shape[0]
    ts = _tile_sizes(t, seq)
    xf = x.reshape(t, d)
    row = lambda v: v.reshape(1, -1)

    for l in range(depth):
        w_main, w_gate = _regroup_w_in(w_in[l])
        wgh, wgl = _split_bf16(w_gate)
        z, zg = _in_proj(xf, row(norm1_g[l]), w_main, wgh, wgl, tm=ts["tm_in"], tn=ts["tn_in"])
        bias = jnp.concatenate([fox_f_bias[l], mlstm_i_bias[l], mlstm_f_bias[l],
                                jnp.zeros((GATE_LANES - A_HEADS - 2 * B_HEADS,), F32)]).reshape(1, GATE_LANES)
        gcol, grow = _gates(zg.reshape(batch, seq, GATE_LANES), bias, tb=ts["tb_gate"])
        g_a = row(out_norm_g[l, :A_WIDTH])
        g_b = row(_pad_heads(out_norm_g[l, A_WIDTH:A_WIDTH + B_WIDTH]))
        g_c = row(out_norm_g[l, A_WIDTH + B_WIDTH:])
        z3 = z.reshape(batch, seq, Z_WIDTH)
        ya = _attention(z, gcol, grow, g_a, batch=batch, seq=seq, tq=ts["tq"])
        cw = jnp.concatenate([_pad_heads(mlstm_conv_w[l, :, :B_WIDTH]), _pad_heads(mlstm_conv_w[l, :, B_WIDTH:])], axis=1)
        yb = _mlstm(z3, gcol, grow, cw, g_b)
        yc = _conformer(z3, conf_dw_w[l], row(conf_dw_b[l]), row(conf_ln_g[l]), row(conf_ln_b[l]),
                        conf_pw_w[l].astype(BF16), row(conf_pw_b[l]), g_c, ts=ts["ts_conf"])
        wo = w_out[l]
        wa = wo[:A_WIDTH].astype(BF16)
        wb = jnp.pad(wo[A_WIDTH:A_WIDTH + B_WIDTH].reshape(B_HEADS, B_HEAD_DIM, d),
                     ((0, 0), (0, B_HEAD_PAD - B_HEAD_DIM), (0, 0))).reshape(B_PAD_WIDTH, d).astype(BF16)
        wc = wo[A_WIDTH + B_WIDTH:].astype(BF16)
        xf = _out_proj(xf, ya, yb.reshape(t, B_PAD_WIDTH), yc.reshape(t, C_WIDTH), wa, wb, wc,
                       tm=ts["tm_out"], tn=ts["tn_out"])

        wqh, wql = _split_bf16(peer_wq[l])
        q, xn = _peer_q(xf, row(norm2_g[l]), wqh, wql, tm=ts["tm_q"], tn=ts["tn_q"])
        keys16 = peer_keys[l].reshape(2 * P_HEADS, N_KEYS, P_HALF)
        c1, w1, r2, e2 = _peer_topk(q, keys16, tt=ts["tt"])
        u = peer_u[l].astype(BF16)
        vt = peer_v[l].T.astype(BF16)
        pt = _peer_dense(xn, u, vt, c1, w1, r2, e2, tm=ts["tm_pd"], te=ts["te"], tc=ts["tc_pd"], rb=ts["rb_pd"])
        xf = _combine(xf, pt, row(final_g), tm=ts["tm_cb"], final=(l == depth - 1))

    return xf.reshape(batch, seq, d)
```

```python
import functools

import jax
import jax.numpy as jnp
from jax import lax
from jax.experimental import pallas as pl
from jax.experimental.pallas import tpu as pltpu

F32 = jnp.float32
BF16 = jnp.bfloat16

EPS = 1e-6
LN_EPS = 1e-5

A_HEADS = 6
A_HEAD_DIM = 128
A_WIDTH = A_HEADS * A_HEAD_DIM
B_HEADS = 4
B_HEAD_DIM = 192
B_HEAD_PAD = 256
B_WIDTH = B_HEADS * B_HEAD_DIM
B_PAD_WIDTH = B_HEADS * B_HEAD_PAD
B_CHUNK = 64
B_CONV = 4
C_GROUPS = 4
C_WIDTH = 512
C_CONV = 31
P_HEADS = 8
P_HALF = 128
N_KEYS = 128
P_TOPK = 16
GATE_LANES = 128

COL_BQ = 0
COL_BK = B_PAD_WIDTH
COL_BV = 2 * B_PAD_WIDTH
COL_BO = 3 * B_PAD_WIDTH
COL_CA = 4 * B_PAD_WIDTH
COL_CG = COL_CA + C_WIDTH
COL_AQ = COL_CG + C_WIDTH
COL_AK = COL_AQ + A_WIDTH
COL_AV = COL_AK + A_WIDTH
Z_USED = COL_AV + A_WIDTH
Z_WIDTH = 7680
G_AF = 0
G_BI = A_HEADS
G_BF = A_HEADS + B_HEADS

VMEM_LIMIT = 60 * 1024 * 1024


def _cparams(sem):
    return pltpu.CompilerParams(dimension_semantics=sem, vmem_limit_bytes=VMEM_LIMIT)


def _split_bf16(a):
    hi = a.astype(BF16)
    lo = (a - hi.astype(F32)).astype(BF16)
    return hi, lo


def _dot(a, b):
    return jnp.dot(a, b, preferred_element_type=F32)


def _dot_nt(a, b):
    return lax.dot_general(a, b, (((1,), (1,)), ((), ())), preferred_element_type=F32)


def _dot_tn(a, b):
    return lax.dot_general(a, b, (((0,), (0,)), ((), ())), preferred_element_type=F32)


def _dot3(ah, al, bh, bl, dot):
    return dot(ah, bh) + dot(ah, bl) + dot(al, bh)


def _sigmoid(x):
    return 1.0 / (1.0 + jnp.exp(-x))


def _inproj_body(x_ref, g_ref, w_ref, wgh_ref, wgl_ref, z_ref, zg_ref, xn_ref):
    @pl.when(pl.program_id(1) == 0)
    def _():
        xf = x_ref[...]
        ms = jnp.mean(xf * xf, axis=-1, keepdims=True)
        xn = xf * lax.rsqrt(ms + EPS) * g_ref[...]
        hi, lo = _split_bf16(xn)
        xn_ref[...] = hi
        zg_ref[...] = _dot3(hi, lo, wgh_ref[...], wgl_ref[...], _dot)

    z_ref[...] = _dot(xn_ref[...], w_ref[...])


def _in_proj(x2d, g, w, wgh, wgl, *, tm, tn):
    t, d = x2d.shape
    n = w.shape[1]
    return pl.pallas_call(
        _inproj_body,
        grid=(t // tm, n // tn),
        in_specs=[
            pl.BlockSpec((tm, d), lambda i, j: (i, 0)),
            pl.BlockSpec((1, d), lambda i, j: (0, 0)),
            pl.BlockSpec((d, tn), lambda i, j: (0, j)),
            pl.BlockSpec((d, GATE_LANES), lambda i, j: (0, 0)),
            pl.BlockSpec((d, GATE_LANES), lambda i, j: (0, 0)),
        ],
        out_specs=[
            pl.BlockSpec((tm, tn), lambda i, j: (i, j)),
            pl.BlockSpec((tm, GATE_LANES), lambda i, j: (i, 0)),
        ],
        out_shape=[
            jax.ShapeDtypeStruct((t, n), F32),
            jax.ShapeDtypeStruct((t, GATE_LANES), F32),
        ],
        scratch_shapes=[pltpu.VMEM((tm, d), BF16)],
        compiler_params=_cparams(("parallel", "arbitrary")),
        name="in_proj",
    )(x2d, g, w, wgh, wgl)


def _gates_body(zg_ref, bias_ref, col_ref, row_ref, carry_ref, *, tb):
    @pl.when(pl.program_id(1) == 0)
    def _():
        carry_ref[...] = jnp.zeros_like(carry_ref)

    v = zg_ref[0] + bias_ref[...]
    ls = jnp.minimum(v, 0.0) - jnp.log1p(jnp.exp(-jnp.abs(v)))
    r = lax.broadcasted_iota(jnp.int32, (tb, tb), 0)
    c = lax.broadcasted_iota(jnp.int32, (tb, tb), 1)
    tri = (c <= r).astype(F32)
    tri_chunk = jnp.where((r // B_CHUNK) == (c // B_CHUNK), tri, 0.0)
    csum = jnp.dot(tri, ls, precision=lax.Precision.HIGHEST, preferred_element_type=F32) + carry_ref[...]
    bcum = jnp.dot(tri_chunk, ls, precision=lax.Precision.HIGHEST, preferred_element_type=F32)
    carry_ref[...] = csum[tb - 1:tb, :]
    lane = lax.broadcasted_iota(jnp.int32, (tb, GATE_LANES), 1)
    out = jnp.where(lane < G_BI, csum, jnp.where(lane < G_BF, v, bcum))
    col_ref[0] = out
    row_ref[0] = out.T[0:16, :]


def _gates(zg3, bias, *, tb):
    b, s, _ = zg3.shape
    return pl.pallas_call(
        functools.partial(_gates_body, tb=tb),
        grid=(b, s // tb),
        in_specs=[
            pl.BlockSpec((1, tb, GATE_LANES), lambda i, j: (i, j, 0)),
            pl.BlockSpec((1, GATE_LANES), lambda i, j: (0, 0)),
        ],
        out_specs=[
            pl.BlockSpec((1, tb, GATE_LANES), lambda i, j: (i, j, 0)),
            pl.BlockSpec((1, 16, tb), lambda i, j: (i, 0, j)),
        ],
        out_shape=[
            jax.ShapeDtypeStruct((b, s, GATE_LANES), F32),
            jax.ShapeDtypeStruct((b, 16, s), F32),
        ],
        scratch_shapes=[pltpu.VMEM((1, GATE_LANES), F32)],
        compiler_params=_cparams(("parallel", "arbitrary")),
        name="gates",
    )(zg3, bias)


def _attn_body(q_ref, k_ref, v_ref, ccol_ref, crow_ref, g_ref, o_ref, kb_ref, vb_ref, m_ref, l_ref, acc_ref, *, tq):
    h = pl.program_id(1)
    qi = pl.program_id(2)
    log2e = 1.4426950408889634
    scale2 = (A_HEAD_DIM ** -0.5) * log2e

    @pl.when(qi == 0)
    def _():
        kb_ref[...] = k_ref[...].astype(BF16)
        vb_ref[...] = v_ref[...].astype(BF16)

    q = q_ref[...].astype(BF16)
    lane = lax.broadcasted_iota(jnp.int32, (tq, GATE_LANES), 1)
    ct2 = jnp.sum(jnp.where(lane == h, ccol_ref[0], 0.0), axis=-1, keepdims=True) * log2e
    m_ref[...] = jnp.full(m_ref.shape, -jnp.inf, F32)
    l_ref[...] = jnp.zeros_like(l_ref)
    acc_ref[...] = jnp.zeros_like(acc_ref)

    def logits(j):
        k0 = pl.multiple_of(j * tq, tq)
        cs2 = crow_ref[0, 0, :, pl.ds(k0, tq)] * log2e
        return _dot_nt(q, kb_ref[pl.ds(k0, tq), :]) * scale2 + (ct2 - cs2)

    def softmax(s):
        m_prev = m_ref[...]
        m_new = jnp.maximum(m_prev, jnp.max(s, axis=-1, keepdims=True))
        p = jnp.exp2(s - m_new)
        alpha = jnp.exp2(m_prev - m_new)
        l_ref[...] = alpha * l_ref[...] + jnp.sum(p, axis=-1, keepdims=True)
        m_ref[...] = m_new
        return p.astype(BF16), alpha

    def values(j, p, alpha):
        k0 = pl.multiple_of(j * tq, tq)
        acc_ref[...] = alpha * acc_ref[...] + _dot(p, vb_ref[pl.ds(k0, tq), :])

    def body(j, s_cur):
        s_next = logits(j + 1)
        p, alpha = softmax(s_cur)
        values(j, p, alpha)
        return s_next

    s_diag = lax.fori_loop(0, qi, body, logits(0))
    row = lax.broadcasted_iota(jnp.int32, (tq, tq), 0)
    col = lax.broadcasted_iota(jnp.int32, (tq, tq), 1)
    p, alpha = softmax(jnp.where(col <= row, s_diag, -jnp.inf))
    values(qi, p, alpha)
    o = acc_ref[...] / l_ref[...]
    o = o * lax.rsqrt(jnp.mean(o * o, axis=-1, keepdims=True) + EPS) * g_ref[...]
    o_ref[...] = o.astype(BF16)


def _attention(z2d, gcol, grow, g_a, *, batch, seq, tq):
    nq = seq // tq
    qc, kc, vc = COL_AQ // 128, COL_AK // 128, COL_AV // 128
    return pl.pallas_call(
        functools.partial(_attn_body, tq=tq),
        grid=(batch, A_HEADS, nq),
        in_specs=[
            pl.BlockSpec((tq, 128), lambda b, h, i: (b * nq + i, qc + h)),
            pl.BlockSpec((seq, 128), lambda b, h, i: (b, kc + h)),
            pl.BlockSpec((seq, 128), lambda b, h, i: (b, vc + h)),
            pl.BlockSpec((1, tq, GATE_LANES), lambda b, h, i: (b, i, 0)),
            pl.BlockSpec((1, 1, 1, seq), lambda b, h, i: (b, h, 0, 0)),
            pl.BlockSpec((1, 128), lambda b, h, i: (0, h)),
        ],
        out_specs=pl.BlockSpec((tq, 128), lambda b, h, i: (b * nq + i, h)),
        out_shape=jax.ShapeDtypeStruct((batch * seq, A_WIDTH), BF16),
        scratch_shapes=[pltpu.VMEM((seq, 128), BF16), pltpu.VMEM((seq, 128), BF16),
                        pltpu.VMEM((tq, 1), F32), pltpu.VMEM((tq, 1), F32), pltpu.VMEM((tq, 128), F32)],
        compiler_params=_cparams(("parallel", "parallel", "arbitrary")),
        name="fox_attention",
    )(z2d, z2d, z2d, gcol, grow, g_a)


def _mlstm_body(zb_ref, gcol_ref, grow_ref, cw_ref, g_ref, y_ref, ext_ref, c_ref, m_ref, *, nb):
    k = pl.program_id(0)
    ln = B_CHUNK
    hp = B_HEAD_PAD

    @pl.when(k == 0)
    def _():
        ext_ref[:, 0:8, :] = jnp.zeros((nb, 8, 2 * B_PAD_WIDTH), F32)
        c_ref[...] = jnp.zeros_like(c_ref)
        m_ref[...] = jnp.zeros_like(m_ref)

    first_half = (k % 2) == 0
    r = lax.broadcasted_iota(jnp.int32, (ln, ln), 0)
    c = lax.broadcasted_iota(jnp.int32, (ln, ln), 1)
    causal = c <= r
    lane = lax.broadcasted_iota(jnp.int32, (ln, hp), 1)
    qscale = B_HEAD_DIM ** -0.5

    for b in range(nb):
        ext_ref[b, 8:8 + ln, :] = zb_ref[b, :, 0:2 * B_PAD_WIDTH]
        conv = cw_ref[0:1, :] * ext_ref[b, 5:5 + ln, :]
        for j in range(1, B_CONV):
            conv = conv + cw_ref[j:j + 1, :] * ext_ref[b, 5 + j:5 + j + ln, :]
        ext_ref[b, 0:8, :] = ext_ref[b, ln:ln + 8, :]
        qk = conv * _sigmoid(conv)
        grow = grow_ref[b]
        grow = jnp.where(first_half, grow[:, 0:ln], grow[:, ln:2 * ln])
        gcol = gcol_ref[b]
        for h in range(B_HEADS):
            st = b * B_HEADS + h
            qq = (qk[:, COL_BQ + h * hp:COL_BQ + (h + 1) * hp] * qscale).astype(BF16)
            kk = qk[:, COL_BK + h * hp:COL_BK + (h + 1) * hp]
            vv = zb_ref[b, :, COL_BV + h * hp:COL_BV + (h + 1) * hp]
            vv = jnp.where(lane == B_HEAD_DIM, 1.0, vv).astype(BF16)
            og = zb_ref[b, :, COL_BO + h * hp:COL_BO + (h + 1) * hp]
            bc_col = gcol[:, G_BF + h:G_BF + h + 1]
            li_col = gcol[:, G_BI + h:G_BI + h + 1]
            bc_row = grow[G_BF + h:G_BF + h + 1, :]
            li_row = grow[G_BI + h:G_BI + h + 1, :]
            m = m_ref[st, 0:1, 0:1]

            d = jnp.where(causal, bc_col - bc_row + li_row, -jnp.inf)
            inter = bc_col + m
            m_t = jnp.maximum(inter, jnp.max(d, axis=-1, keepdims=True))
            w_inter = jnp.exp(inter - m_t)
            sc = _dot_nt(qq, kk.astype(BF16)) * jnp.exp(d - m_t)
            cm = c_ref[st]
            num = w_inter * _dot(qq, cm.astype(BF16)) + _dot(sc.astype(BF16), vv)
            den = num[:, B_HEAD_DIM:B_HEAD_DIM + 1]
            hh = num / jnp.maximum(jnp.abs(den), jnp.exp(-m_t))

            b_last = bc_col[ln - 1:ln, :]
            dec = b_last - bc_col + li_col
            m_new = jnp.maximum(b_last + m, jnp.max(dec, axis=0, keepdims=True))
            a = jnp.exp(b_last + m - m_new)
            wk = (jnp.exp(dec - m_new) * kk).astype(BF16)
            c_ref[st] = a * cm + _dot_tn(wk, vv)
            m_ref[st] = jnp.broadcast_to(m_new, (8, 128))

            y = jnp.where(lane < B_HEAD_DIM, _sigmoid(og) * hh, 0.0)
            ms = jnp.sum(y * y, axis=-1, keepdims=True) * (1.0 / B_HEAD_DIM)
            y = y * lax.rsqrt(ms + EPS) * g_ref[:, h * hp:(h + 1) * hp]
            y_ref[b, :, h * hp:(h + 1) * hp] = y.astype(BF16)


def _mlstm(z3, gcol, grow, cw, g_b):
    nb, seq, _ = z3.shape
    nc = seq // B_CHUNK
    return pl.pallas_call(
        functools.partial(_mlstm_body, nb=nb),
        grid=(nc,),
        in_specs=[
            pl.BlockSpec((nb, B_CHUNK, 4 * B_PAD_WIDTH), lambda k: (0, k, 0)),
            pl.BlockSpec((nb, B_CHUNK, GATE_LANES), lambda k: (0, k, 0)),
            pl.BlockSpec((nb, 16, 2 * B_CHUNK), lambda k: (0, 0, k // 2)),
            pl.BlockSpec((B_CONV, 2 * B_PAD_WIDTH), lambda k: (0, 0)),
            pl.BlockSpec((1, B_PAD_WIDTH), lambda k: (0, 0)),
        ],
        out_specs=pl.BlockSpec((nb, B_CHUNK, B_PAD_WIDTH), lambda k: (0, k, 0)),
        out_shape=jax.ShapeDtypeStruct((nb, seq, B_PAD_WIDTH), BF16),
        scratch_shapes=[
            pltpu.VMEM((nb, B_CHUNK + 8, 2 * B_PAD_WIDTH), F32),
            pltpu.VMEM((nb * B_HEADS, B_HEAD_PAD, B_HEAD_PAD), F32),
            pltpu.VMEM((nb * B_HEADS, 8, 128), F32),
        ],
        compiler_params=_cparams(("arbitrary",)),
        name="mlstm",
    )(z3, gcol, grow, cw, g_b)


def _conf_body(ca_ref, cg_ref, dw_ref, db_ref, lg_ref, lb_ref, pw_ref, pb_ref, g_ref, y_ref, ext_ref, *, ts):
    halo = 32

    @pl.when(pl.program_id(1) == 0)
    def _():
        ext_ref[0:halo, :] = jnp.zeros((halo, C_WIDTH), F32)

    ext_ref[halo:halo + ts, :] = ca_ref[0] * _sigmoid(cg_ref[0])
    off = halo - (C_CONV - 1)
    acc = dw_ref[0:1, :] * ext_ref[off:off + ts, :]
    for j in range(1, C_CONV):
        acc = acc + dw_ref[j:j + 1, :] * ext_ref[off + j:off + j + ts, :]
    ext_ref[0:halo, :] = ext_ref[ts:ts + halo, :]
    y = acc + db_ref[...]
    mu = jnp.mean(y, axis=-1, keepdims=True)
    yc = y - mu
    y = yc * lax.rsqrt(jnp.mean(yc * yc, axis=-1, keepdims=True) + LN_EPS)
    y = y * lg_ref[...] + lb_ref[...]
    y = y * _sigmoid(y)
    y = _dot(y.astype(BF16), pw_ref[...]) + pb_ref[...]
    gw = C_WIDTH // C_GROUPS
    for gi in range(C_GROUPS):
        yg = y[:, gi * gw:(gi + 1) * gw]
        yg = yg * lax.rsqrt(jnp.mean(yg * yg, axis=-1, keepdims=True) + EPS) * g_ref[:, gi * gw:(gi + 1) * gw]
        y_ref[0, :, gi * gw:(gi + 1) * gw] = yg.astype(BF16)


def _conformer(z3, dw, db, lg, lb, pw, pb, g_c, *, ts):
    nb, seq, _ = z3.shape
    ca, cg = COL_CA // C_WIDTH, COL_CG // C_WIDTH
    vec = pl.BlockSpec((1, C_WIDTH), lambda b, i: (0, 0))
    return pl.pallas_call(
        functools.partial(_conf_body, ts=ts),
        grid=(nb, seq // ts),
        in_specs=[
            pl.BlockSpec((1, ts, C_WIDTH), lambda b, i: (b, i, ca)),
            pl.BlockSpec((1, ts, C_WIDTH), lambda b, i: (b, i, cg)),
            pl.BlockSpec((C_CONV, C_WIDTH), lambda b, i: (0, 0)),
            vec, vec, vec,
            pl.BlockSpec((C_WIDTH, C_WIDTH), lambda b, i: (0, 0)),
            vec, vec,
        ],
        out_specs=pl.BlockSpec((1, ts, C_WIDTH), lambda b, i: (b, i, 0)),
        out_shape=jax.ShapeDtypeStruct((nb, seq, C_WIDTH), BF16),
        scratch_shapes=[pltpu.VMEM((ts + 32, C_WIDTH), F32)],
        compiler_params=_cparams(("parallel", "arbitrary")),
        name="conformer",
    )(z3, z3, dw, db, lg, lb, pw, pb, g_c)


def _outproj_body(x_ref, ya_ref, yb_ref, yc_ref, wa_ref, wb_ref, wc_ref, o_ref):
    o_ref[...] = (x_ref[...] + _dot(ya_ref[...], wa_ref[...]) + _dot(yb_ref[...], wb_ref[...])
                  + _dot(yc_ref[...], wc_ref[...]))


def _out_proj(x2d, ya, yb, yc, wa, wb, wc, *, tm, tn):
    t, d = x2d.shape
    return pl.pallas_call(
        _outproj_body,
        grid=(t // tm, d // tn),
        in_specs=[
            pl.BlockSpec((tm, tn), lambda i, j: (i, j)),
            pl.BlockSpec((tm, ya.shape[1]), lambda i, j: (i, 0)),
            pl.BlockSpec((tm, yb.shape[1]), lambda i, j: (i, 0)),
            pl.BlockSpec((tm, yc.shape[1]), lambda i, j: (i, 0)),
            pl.BlockSpec((wa.shape[0], tn), lambda i, j: (0, j)),
            pl.BlockSpec((wb.shape[0], tn), lambda i, j: (0, j)),
            pl.BlockSpec((wc.shape[0], tn), lambda i, j: (0, j)),
        ],
        out_specs=pl.BlockSpec((tm, tn), lambda i, j: (i, j)),
        out_shape=jax.ShapeDtypeStruct((t, d), F32),
        compiler_params=_cparams(("parallel", "arbitrary")),
        name="out_proj",
    )(x2d, ya, yb, yc, wa, wb, wc)


def _peerq_body(x_ref, g_ref, wh_ref, wl_ref, q_ref, xn_ref, hi_ref, lo_ref):
    @pl.when(pl.program_id(1) == 0)
    def _():
        xf = x_ref[...]
        ms = jnp.mean(xf * xf, axis=-1, keepdims=True)
        xn = xf * lax.rsqrt(ms + EPS) * g_ref[...]
        hi, lo = _split_bf16(xn)
        hi_ref[...] = hi
        lo_ref[...] = lo
        xn_ref[...] = hi

    q_ref[...] = _dot3(hi_ref[...], lo_ref[...], wh_ref[...], wl_ref[...], _dot)


def _peer_q(x2d, g, wh, wl, *, tm, tn):
    t, d = x2d.shape
    n = wh.shape[1]
    return pl.pallas_call(
        _peerq_body,
        grid=(t // tm, n // tn),
        in_specs=[
            pl.BlockSpec((tm, d), lambda i, j: (i, 0)),
            pl.BlockSpec((1, d), lambda i, j: (0, 0)),
            pl.BlockSpec((d, tn), lambda i, j: (0, j)),
            pl.BlockSpec((d, tn), lambda i, j: (0, j)),
        ],
        out_specs=[
            pl.BlockSpec((tm, tn), lambda i, j: (i, j)),
            pl.BlockSpec((tm, d), lambda i, j: (i, 0)),
        ],
        out_shape=[
            jax.ShapeDtypeStruct((t, n), F32),
            jax.ShapeDtypeStruct((t, d), BF16),
        ],
        scratch_shapes=[pltpu.VMEM((tm, d), BF16), pltpu.VMEM((tm, d), BF16)],
        compiler_params=_cparams(("parallel", "arbitrary")),
        name="peer_q",
    )(x2d, g, wh, wl)


def _extract_top(s, rowid, n_out):
    nrow = s.shape[0]
    rank = jnp.full(s.shape, float(n_out), F32)
    vals = []
    for it in range(n_out):
        m = jnp.max(s, axis=0, keepdims=True)
        idx = jnp.min(jnp.where(s == m, rowid, float(nrow)), axis=0, keepdims=True)
        hit = rowid == idx
        vals.append(m)
        rank = jnp.where(hit, float(it), rank)
        s = jnp.where(hit, -jnp.inf, s)
    return vals, rank, rank < float(n_out)


def _extract_top_distinct(s, n_out):
    rank = jnp.full(s.shape, float(n_out), F32)
    vals = []
    for it in range(n_out):
        m = jnp.max(s, axis=0, keepdims=True)
        hit = s == m
        vals.append(m)
        rank = jnp.where(hit, float(it), rank)
        s = jnp.where(hit, -jnp.inf, s)
    taken = rank < float(n_out)
    return vals, rank, taken, jnp.sum(jnp.where(taken, 1.0, 0.0), axis=0, keepdims=True)


_CAND_SHORT = 8


def _topk_body(q_ref, keys_ref, c1_ref, w1_ref, r2_ref, e2_ref, *, tt):
    rowid = lax.broadcasted_iota(jnp.int32, (N_KEYS, tt), 0).astype(F32)
    rowid2 = lax.broadcasted_iota(jnp.int32, (P_TOPK * P_TOPK, tt), 0).astype(F32)

    def pair_grid(x1, x2, op):
        parts = [op(x1[0:1, :], x2)]
        parts += [op(x1[a:a + 1, :], x2[0:_CAND_SHORT, :]) for a in range(1, _CAND_SHORT)]
        parts += [op(x1[_CAND_SHORT:, :], x2[0:1, :])]
        return jnp.concatenate(parts, axis=0)

    def finish(h, scores, v1, v2, ranks, counts, zsum):
        cnt = jnp.zeros((N_KEYS, tt), F32)
        for a in range(P_TOPK):
            cnt = jnp.where(ranks[0] == float(a), counts[a], cnt)
        c1_ref[h] = cnt
        w1_ref[h] = jnp.exp(scores[0] - v1[0:1, :]) / zsum
        r2_ref[h] = ranks[1].astype(BF16)
        e2_ref[h] = jnp.exp(scores[1] - v2[0:1, :]).astype(BF16)

    def per_head(h, carry):
        scores = []
        for c in range(2):
            kh, kl = _split_bf16(keys_ref[2 * h + c])
            col = pl.multiple_of((2 * h + c) * P_HALF, P_HALF)
            qh, ql = _split_bf16(q_ref[:, pl.ds(col, P_HALF)])
            scores.append(_dot3(kh, kl, qh, ql, _dot_nt))

        tops, ranks, clean = [], [], None
        for c in range(2):
            vals, rank, _, lost = _extract_top_distinct(scores[c], P_TOPK)
            tops.append(jnp.concatenate(vals, axis=0))
            ranks.append(rank)
            ok = lost == float(P_TOPK)
            clean = ok if clean is None else (clean & ok)
        v1, v2 = tops
        cand = pair_grid(v1, v2, lambda x, y: x + y)
        _, _, picked, lost = _extract_top_distinct(cand, P_TOPK)
        clean = clean & (lost == float(P_TOPK))
        ev1 = jnp.exp(v1 - v1[0:1, :])
        ev2 = jnp.exp(v2 - v2[0:1, :])
        prod = pair_grid(ev1, ev2, lambda x, y: x * y)
        zsum = jnp.sum(jnp.where(picked, prod, 0.0), axis=0, keepdims=True)
        npick = jnp.where(picked, 1.0, 0.0)
        counts = [jnp.sum(npick[0:P_TOPK, :], axis=0, keepdims=True)]
        for a in range(1, _CAND_SHORT):
            r0 = P_TOPK + (a - 1) * _CAND_SHORT
            counts.append(jnp.sum(npick[r0:r0 + _CAND_SHORT, :], axis=0, keepdims=True))
        r0 = P_TOPK + (_CAND_SHORT - 1) * _CAND_SHORT
        counts += [npick[r0 + a:r0 + a + 1, :] for a in range(P_TOPK - _CAND_SHORT)]
        finish(h, scores, v1, v2, ranks, counts, zsum)

        @pl.when(jnp.min(jnp.where(clean, 1.0, 0.0)) < 0.5)
        def _():
            tops, ranks = [], []
            for c in range(2):
                vals, rank, _ = _extract_top(scores[c], rowid, P_TOPK)
                tops.append(jnp.concatenate(vals, axis=0))
                ranks.append(rank)
            v1, v2 = tops
            cand = jnp.concatenate([v1[a:a + 1, :] + v2 for a in range(P_TOPK)], axis=0)
            _, _, picked = _extract_top(cand, rowid2, P_TOPK)
            ev1 = jnp.exp(v1 - v1[0:1, :])
            ev2 = jnp.exp(v2 - v2[0:1, :])
            prod = jnp.concatenate([ev1[a:a + 1, :] * ev2 for a in range(P_TOPK)], axis=0)
            zsum = jnp.sum(jnp.where(picked, prod, 0.0), axis=0, keepdims=True)
            npick = jnp.where(picked, 1.0, 0.0)
            counts = [jnp.sum(npick[a * P_TOPK:(a + 1) * P_TOPK, :], axis=0, keepdims=True) for a in range(P_TOPK)]
            finish(h, scores, v1, v2, ranks, counts, zsum)

        return carry

    lax.fori_loop(0, P_HEADS, per_head, 0)


def _peer_topk(q, keys16, *, tt):
    t = q.shape[0]
    st_shape = jax.ShapeDtypeStruct((P_HEADS, N_KEYS, t), F32)
    st_shape_b = jax.ShapeDtypeStruct((P_HEADS, N_KEYS, t), BF16)
    st_spec = pl.BlockSpec((P_HEADS, N_KEYS, tt), lambda i: (0, 0, i))
    return pl.pallas_call(
        functools.partial(_topk_body, tt=tt),
        grid=(t // tt,),
        in_specs=[
            pl.BlockSpec((tt, q.shape[1]), lambda i: (i, 0)),
            pl.BlockSpec(keys16.shape, lambda i: (0, 0, 0)),
        ],
        out_specs=[st_spec] * 4,
        out_shape=[st_shape, st_shape, st_shape_b, st_shape_b],
        compiler_params=_cparams(("parallel",)),
        name="peer_topk",
    )(q, keys16)


def _peer_dense_body(xn_ref, u_ref, vt_ref, c1_ref, w1_ref, r2_ref, e2_ref, o_ref, c1s_ref, w1s_ref, *hg_refs, tm, te, tc, rb):
    e = pl.program_id(1)

    @pl.when(e == 0)
    def _():
        o_ref[...] = jnp.zeros_like(o_ref)

    ng = te // N_KEYS
    nchunk = tm // tc
    which = e % (8 // ng)

    for src, dst in ((c1_ref, c1s_ref), (w1_ref, w1s_ref)):
        rows = src[:, 0:ng, :]
        for g in range(1, 8 // ng):
            rows = jnp.where(which == g, src[:, g * ng:(g + 1) * ng, :], rows)
        dst[:, 0:ng, :] = rows

    def bf16_row(ref, h, ii, t0):
        return ref[h, ii:ii + 1, t0:t0 + tc].astype(BF16)

    def stage_a(ci):
        t0 = ci * tc
        return _dot_nt(u_ref[...], xn_ref[t0:t0 + tc, :])

    def stage_b(ci, hts):
        t0 = ci * tc
        for ii in range(ng):
            rows = [(bf16_row(c1s_ref, h, ii, t0), bf16_row(w1s_ref, h, ii, t0)) for h in range(P_HEADS)]
            for j0 in range(0, N_KEYS, rb):
                r0 = ii * N_KEYS + j0
                ht = hts[r0:r0 + rb, :]
                act = ((0.5 * ht) * (1.0 + lax.erf(ht * (2.0 ** -0.5)))).astype(BF16)
                gate = jnp.zeros((rb, tc), BF16)
                for h in range(P_HEADS):
                    cnt, w1 = rows[h]
                    gate = gate + jnp.where(r2_ref[h, j0:j0 + rb, t0:t0 + tc] < cnt,
                                            e2_ref[h, j0:j0 + rb, t0:t0 + tc] * w1, jnp.zeros((), BF16))
                hg_refs[ci][r0:r0 + rb, :] = act * gate

    def stage_c(ci):
        t0 = ci * tc
        o_ref[:, t0:t0 + tc] += _dot(vt_ref[...], hg_refs[ci][...])

    ahead = 2
    hts = {ci: stage_a(ci) for ci in range(min(ahead, nchunk))}
    for ci in range(nchunk):
        if ci + ahead < nchunk:
            hts[ci + ahead] = stage_a(ci + ahead)
        stage_b(ci, hts.pop(ci))
        stage_c(ci)


def _peer_dense(xn, u, vt, c1, w1, r2, e2, *, tm, te, tc, rb):
    t, d = xn.shape
    ne = u.shape[0]
    ng = te // N_KEYS
    once = pl.Buffered(1)
    row_spec = pl.BlockSpec((P_HEADS, 8, tm), lambda i, e: (0, e // (8 // ng), i))
    full_spec = pl.BlockSpec((P_HEADS, N_KEYS, tm), lambda i, e: (0, 0, i), pipeline_mode=once)
    return pl.pallas_call(
        functools.partial(_peer_dense_body, tm=tm, te=te, tc=tc, rb=rb),
        grid=(t // tm, ne // te),
        in_specs=[
            pl.BlockSpec((tm, d), lambda i, e: (i, 0), pipeline_mode=once),
            pl.BlockSpec((te, d), lambda i, e: (e, 0)),
            pl.BlockSpec((d, te), lambda i, e: (0, e)),
            row_spec, row_spec, full_spec, full_spec,
        ],
        out_specs=pl.BlockSpec((d, tm), lambda i, e: (0, i)),
        out_shape=jax.ShapeDtypeStruct((d, t), F32),
        scratch_shapes=[pltpu.VMEM((P_HEADS, 8, tm), F32), pltpu.VMEM((P_HEADS, 8, tm), F32)]
        + [pltpu.VMEM((te, tc), BF16)] * (tm // tc),
        compiler_params=_cparams(("parallel", "arbitrary")),
        name="peer_dense",
    )(xn, u, vt, c1, w1, r2, e2)


def _combine_body(x_ref, pt_ref, g_ref, o_ref, *, final):
    y = x_ref[...] + pt_ref[...].T
    if final:
        y = y * lax.rsqrt(jnp.mean(y * y, axis=-1, keepdims=True) + EPS) * g_ref[...]
    o_ref[...] = y


def _combine(x2d, pt, g, *, tm, final):
    t, d = x2d.shape
    return pl.pallas_call(
        functools.partial(_combine_body, final=final),
        grid=(t // tm,),
        in_specs=[
            pl.BlockSpec((tm, d), lambda i: (i, 0)),
            pl.BlockSpec((d, tm), lambda i: (0, i)),
            pl.BlockSpec((1, d), lambda i: (0, 0)),
        ],
        out_specs=pl.BlockSpec((tm, d), lambda i: (i, 0)),
        out_shape=jax.ShapeDtypeStruct((t, d), F32),
        compiler_params=_cparams(("parallel",)),
        name="combine",
    )(x2d, pt, g)


def _pad_heads(w):
    lead = w.shape[:-1]
    w = w.reshape(lead + (B_HEADS, B_HEAD_DIM))
    w = jnp.pad(w, [(0, 0)] * len(lead) + [(0, 0), (0, B_HEAD_PAD - B_HEAD_DIM)])
    return w.reshape(lead + (B_PAD_WIDTH,))


def _regroup_w_in(w):
    sizes = (A_WIDTH, A_WIDTH, A_WIDTH, A_HEADS, 2 * B_WIDTH, B_WIDTH, B_WIDTH, B_HEADS, B_HEADS, C_WIDTH, C_WIDTH)
    parts, p = [], 0
    for sz in sizes:
        parts.append(w[:, p:p + sz])
        p += sz
    aq, ak, av, af, bqk, bv, bo, bi, bf, ca, cg = parts
    d = w.shape[0]
    main = jnp.concatenate(
        [_pad_heads(bqk[:, :B_WIDTH]), _pad_heads(bqk[:, B_WIDTH:]), _pad_heads(bv), _pad_heads(bo),
         ca, cg, aq, ak, av, jnp.zeros((d, Z_WIDTH - Z_USED), w.dtype)], axis=1)
    gates = jnp.concatenate([af, bi, bf, jnp.zeros((d, GATE_LANES - A_HEADS - 2 * B_HEADS), w.dtype)], axis=1)
    return main.astype(BF16), gates


def _tile_sizes(t, seq):
    return dict(
        tm_in=min(512, t), tn_in=1280,
        tb_gate=min(256, seq),
        tq=min(512, seq),
        ts_conf=min(256, seq),
        tm_out=min(512, t), tn_out=1024,
        tm_q=min(512, t), tn_q=1024,
        tt=min(256, t),
        tm_pd=min(1024, t), te=512, tc_pd=min(256, t), rb_pd=32,
        tm_cb=min(256, t),
    )


def kernel(x, w_in, fox_f_bias, mlstm_conv_w, mlstm_i_bias, mlstm_f_bias, conf_dw_w, conf_dw_b, conf_ln_g, conf_ln_b, conf_pw_w, conf_pw_b, out_norm_g, w_out, norm1_g, norm2_g, peer_wq, peer_keys, peer_u, peer_v, final_g):
    batch, seq, d = x.shape
    t = batch * seq
    depth = w_in.shape[0]
    ts = _tile_sizes(t, seq)
    xf = x.reshape(t, d)
    row = lambda v: v.reshape(1, -1)

    for l in range(depth):
        w_main, w_gate = _regroup_w_in(w_in[l])
        wgh, wgl = _split_bf16(w_gate)
        z, zg = _in_proj(xf, row(norm1_g[l]), w_main, wgh, wgl, tm=ts["tm_in"], tn=ts["tn_in"])
        bias = jnp.concatenate([fox_f_bias[l], mlstm_i_bias[l], mlstm_f_bias[l],
                                jnp.zeros((GATE_LANES - A_HEADS - 2 * B_HEADS,), F32)]).reshape(1, GATE_LANES)
        gcol, grow = _gates(zg.reshape(batch, seq, GATE_LANES), bias, tb=ts["tb_gate"])
        g_a = row(out_norm_g[l, :A_WIDTH])
        g_b = row(_pad_heads(out_norm_g[l, A_WIDTH:A_WIDTH + B_WIDTH]))
        g_c = row(out_norm_g[l, A_WIDTH + B_WIDTH:])
        z3 = z.reshape(batch, seq, Z_WIDTH)
        ya = _attention(z, gcol, grow.reshape(batch, 16, 1, seq), g_a, batch=batch, seq=seq, tq=ts["tq"])
        cw = jnp.concatenate([_pad_heads(mlstm_conv_w[l, :, :B_WIDTH]), _pad_heads(mlstm_conv_w[l, :, B_WIDTH:])], axis=1)
        yb = _mlstm(z3, gcol, grow, cw, g_b)
        yc = _conformer(z3, conf_dw_w[l], row(conf_dw_b[l]), row(conf_ln_g[l]), row(conf_ln_b[l]),
                        conf_pw_w[l].astype(BF16), row(conf_pw_b[l]), g_c, ts=ts["ts_conf"])
        wo = w_out[l]
        wa = wo[:A_WIDTH].astype(BF16)
        wb = jnp.pad(wo[A_WIDTH:A_WIDTH + B_WIDTH].reshape(B_HEADS, B_HEAD_DIM, d),
                     ((0, 0), (0, B_HEAD_PAD - B_HEAD_DIM), (0, 0))).reshape(B_PAD_WIDTH, d).astype(BF16)
        wc = wo[A_WIDTH + B_WIDTH:].astype(BF16)
        xf = _out_proj(xf, ya, yb.reshape(t, B_PAD_WIDTH), yc.reshape(t, C_WIDTH), wa, wb, wc,
                       tm=ts["tm_out"], tn=ts["tn_out"])

        wqh, wql = _split_bf16(peer_wq[l])
        q, xn = _peer_q(xf, row(norm2_g[l]), wqh, wql, tm=ts["tm_q"], tn=ts["tn_q"])
        keys16 = peer_keys[l].reshape(2 * P_HEADS, N_KEYS, P_HALF)
        c1, w1, r2, e2 = _peer_topk(q, keys16, tt=ts["tt"])
        u = peer_u[l].astype(BF16)
        vt = peer_v[l].T.astype(BF16)
        pt = _peer_dense(xn, u, vt, c1, w1, r2, e2, tm=ts["tm_pd"], te=ts["te"], tc=ts["tc_pd"], rb=ts["rb_pd"])
        xf = _combine(xf, pt, row(final_g), tm=ts["tm_cb"], final=(l == depth - 1))

    return xf.reshape(batch, seq, d)
```

```python
import functools

import jax
import jax.numpy as jnp
from jax import lax
from jax.experimental import pallas as pl
from jax.experimental.pallas import tpu as pltpu

F32 = jnp.float32
BF16 = jnp.bfloat16

EPS = 1e-6
LN_EPS = 1e-5

A_HEADS = 6
A_HEAD_DIM = 128
A_WIDTH = A_HEADS * A_HEAD_DIM
B_HEADS = 4
B_HEAD_DIM = 192
B_HEAD_PAD = 256
B_WIDTH = B_HEADS * B_HEAD_DIM
B_PAD_WIDTH = B_HEADS * B_HEAD_PAD
B_CHUNK = 64
B_CONV = 4
C_GROUPS = 4
C_WIDTH = 512
C_CONV = 31
P_HEADS = 8
P_HALF = 128
N_KEYS = 128
P_TOPK = 16
GATE_LANES = 128

COL_BQ = 0
COL_BK = B_PAD_WIDTH
COL_BV = 2 * B_PAD_WIDTH
COL_BO = 3 * B_PAD_WIDTH
COL_CA = 4 * B_PAD_WIDTH
COL_CG = COL_CA + C_WIDTH
COL_AQ = COL_CG + C_WIDTH
COL_AK = COL_AQ + A_WIDTH
COL_AV = COL_AK + A_WIDTH
Z_USED = COL_AV + A_WIDTH
Z_WIDTH = 7680
G_AF = 0
G_BI = A_HEADS
G_BF = A_HEADS + B_HEADS

VMEM_LIMIT = 60 * 1024 * 1024


def _cparams(sem):
    return pltpu.CompilerParams(dimension_semantics=sem, vmem_limit_bytes=VMEM_LIMIT)


def _split_bf16(a):
    hi = a.astype(BF16)
    lo = (a - hi.astype(F32)).astype(BF16)
    return hi, lo


def _dot(a, b):
    return jnp.dot(a, b, preferred_element_type=F32)


def _dot_nt(a, b):
    return lax.dot_general(a, b, (((1,), (1,)), ((), ())), preferred_element_type=F32)


def _dot_tn(a, b):
    return lax.dot_general(a, b, (((0,), (0,)), ((), ())), preferred_element_type=F32)


def _dot3(ah, al, bh, bl, dot):
    return dot(ah, bh) + dot(ah, bl) + dot(al, bh)


def _sigmoid(x):
    return 1.0 / (1.0 + jnp.exp(-x))


def _inproj_body(x_ref, g_ref, w_ref, wgh_ref, wgl_ref, z_ref, zg_ref, xn_ref):
    @pl.when(pl.program_id(1) == 0)
    def _():
        xf = x_ref[...]
        ms = jnp.mean(xf * xf, axis=-1, keepdims=True)
        xn = xf * lax.rsqrt(ms + EPS) * g_ref[...]
        hi, lo = _split_bf16(xn)
        xn_ref[...] = hi
        zg_ref[...] = _dot3(hi, lo, wgh_ref[...], wgl_ref[...], _dot)

    z_ref[...] = _dot(xn_ref[...], w_ref[...])


def _in_proj(x2d, g, w, wgh, wgl, *, tm, tn):
    t, d = x2d.shape
    n = w.shape[1]
    return pl.pallas_call(
        _inproj_body,
        grid=(t // tm, n // tn),
        in_specs=[
            pl.BlockSpec((tm, d), lambda i, j: (i, 0), pipeline_mode=pl.Buffered(1)),
            pl.BlockSpec((1, d), lambda i, j: (0, 0)),
            pl.BlockSpec((d, tn), lambda i, j: (0, j)),
            pl.BlockSpec((d, GATE_LANES), lambda i, j: (0, 0)),
            pl.BlockSpec((d, GATE_LANES), lambda i, j: (0, 0)),
        ],
        out_specs=[
            pl.BlockSpec((tm, tn), lambda i, j: (i, j)),
            pl.BlockSpec((tm, GATE_LANES), lambda i, j: (i, 0)),
        ],
        out_shape=[
            jax.ShapeDtypeStruct((t, n), F32),
            jax.ShapeDtypeStruct((t, GATE_LANES), F32),
        ],
        scratch_shapes=[pltpu.VMEM((tm, d), BF16)],
        compiler_params=_cparams(("parallel", "arbitrary")),
        name="in_proj",
    )(x2d, g, w, wgh, wgl)


def _gates_body(zg_ref, bias_ref, col_ref, row_ref, carry_ref, *, tb):
    @pl.when(pl.program_id(1) == 0)
    def _():
        carry_ref[...] = jnp.zeros_like(carry_ref)

    v = zg_ref[0] + bias_ref[...]
    ls = jnp.minimum(v, 0.0) - jnp.log1p(jnp.exp(-jnp.abs(v)))
    r = lax.broadcasted_iota(jnp.int32, (tb, tb), 0)
    c = lax.broadcasted_iota(jnp.int32, (tb, tb), 1)
    tri = (c <= r).astype(F32)
    tri_chunk = jnp.where((r // B_CHUNK) == (c // B_CHUNK), tri, 0.0)
    csum = jnp.dot(tri, ls, precision=lax.Precision.HIGHEST, preferred_element_type=F32) + carry_ref[...]
    bcum = jnp.dot(tri_chunk, ls, precision=lax.Precision.HIGHEST, preferred_element_type=F32)
    carry_ref[...] = csum[tb - 1:tb, :]
    lane = lax.broadcasted_iota(jnp.int32, (tb, GATE_LANES), 1)
    out = jnp.where(lane < G_BI, csum, jnp.where(lane < G_BF, v, bcum))
    col_ref[0] = out
    row_ref[0] = out.T[0:16, :]


def _gates(zg3, bias, *, tb):
    b, s, _ = zg3.shape
    return pl.pallas_call(
        functools.partial(_gates_body, tb=tb),
        grid=(b, s // tb),
        in_specs=[
            pl.BlockSpec((1, tb, GATE_LANES), lambda i, j: (i, j, 0)),
            pl.BlockSpec((1, GATE_LANES), lambda i, j: (0, 0)),
        ],
        out_specs=[
            pl.BlockSpec((1, tb, GATE_LANES), lambda i, j: (i, j, 0)),
            pl.BlockSpec((1, 16, tb), lambda i, j: (i, 0, j)),
        ],
        out_shape=[
            jax.ShapeDtypeStruct((b, s, GATE_LANES), F32),
            jax.ShapeDtypeStruct((b, 16, s), F32),
        ],
        scratch_shapes=[pltpu.VMEM((1, GATE_LANES), F32)],
        compiler_params=_cparams(("parallel", "arbitrary")),
        name="gates",
    )(zg3, bias)


def _attn_body(q_ref, k_ref, v_ref, ccol_ref, crow_ref, g_ref, o_ref, kb_ref, vb_ref, m_ref, l_ref, acc_ref, *, tq):
    h = pl.program_id(1)
    qi = pl.program_id(2)
    log2e = 1.4426950408889634
    scale2 = (A_HEAD_DIM ** -0.5) * log2e

    @pl.when(qi == 0)
    def _():
        kb_ref[...] = k_ref[...].astype(BF16)
        vb_ref[...] = v_ref[...].astype(BF16)

    q = q_ref[...].astype(BF16)
    lane = lax.broadcasted_iota(jnp.int32, (tq, GATE_LANES), 1)
    ct2 = jnp.sum(jnp.where(lane == h, ccol_ref[0], 0.0), axis=-1, keepdims=True) * log2e
    m_ref[...] = jnp.full(m_ref.shape, -jnp.inf, F32)
    l_ref[...] = jnp.zeros_like(l_ref)
    acc_ref[...] = jnp.zeros_like(acc_ref)

    def logits(j):
        k0 = pl.multiple_of(j * tq, tq)
        cs2 = crow_ref[0, 0, :, pl.ds(k0, tq)] * log2e
        return _dot_nt(q, kb_ref[pl.ds(k0, tq), :]) * scale2 + (ct2 - cs2)

    def softmax(s):
        m_prev = m_ref[...]
        m_new = jnp.maximum(m_prev, jnp.max(s, axis=-1, keepdims=True))
        p = jnp.exp2(s - m_new)
        alpha = jnp.exp2(m_prev - m_new)
        l_ref[...] = alpha * l_ref[...] + jnp.sum(p, axis=-1, keepdims=True)
        m_ref[...] = m_new
        return p.astype(BF16), alpha

    def values(j, p, alpha):
        k0 = pl.multiple_of(j * tq, tq)
        acc_ref[...] = alpha * acc_ref[...] + _dot(p, vb_ref[pl.ds(k0, tq), :])

    def body(j, s_cur):
        s_next = logits(j + 1)
        p, alpha = softmax(s_cur)
        values(j, p, alpha)
        return s_next

    s_diag = lax.fori_loop(0, qi, body, logits(0))
    row = lax.broadcasted_iota(jnp.int32, (tq, tq), 0)
    col = lax.broadcasted_iota(jnp.int32, (tq, tq), 1)
    p, alpha = softmax(jnp.where(col <= row, s_diag, -jnp.inf))
    values(qi, p, alpha)
    o = acc_ref[...] / l_ref[...]
    o = o * lax.rsqrt(jnp.mean(o * o, axis=-1, keepdims=True) + EPS) * g_ref[...]
    o_ref[...] = o.astype(BF16)


def _attention(z2d, gcol, grow, g_a, *, batch, seq, tq):
    nq = seq // tq
    qc, kc, vc = COL_AQ // 128, COL_AK // 128, COL_AV // 128
    return pl.pallas_call(
        functools.partial(_attn_body, tq=tq),
        grid=(batch, A_HEADS, nq),
        in_specs=[
            pl.BlockSpec((tq, 128), lambda b, h, i: (b * nq + i, qc + h)),
            pl.BlockSpec((seq, 128), lambda b, h, i: (b, kc + h)),
            pl.BlockSpec((seq, 128), lambda b, h, i: (b, vc + h)),
            pl.BlockSpec((1, tq, GATE_LANES), lambda b, h, i: (b, i, 0)),
            pl.BlockSpec((1, 1, 1, seq), lambda b, h, i: (b, h, 0, 0)),
            pl.BlockSpec((1, 128), lambda b, h, i: (0, h)),
        ],
        out_specs=pl.BlockSpec((tq, 128), lambda b, h, i: (b * nq + i, h)),
        out_shape=jax.ShapeDtypeStruct((batch * seq, A_WIDTH), BF16),
        scratch_shapes=[pltpu.VMEM((seq, 128), BF16), pltpu.VMEM((seq, 128), BF16),
                        pltpu.VMEM((tq, 1), F32), pltpu.VMEM((tq, 1), F32), pltpu.VMEM((tq, 128), F32)],
        compiler_params=_cparams(("parallel", "parallel", "arbitrary")),
        name="fox_attention",
    )(z2d, z2d, z2d, gcol, grow, g_a)


def _mlstm_body(zb_ref, gcol_ref, grow_ref, cw_ref, g_ref, y_ref, ext_ref, c_ref, m_ref, *, nb):
    k = pl.program_id(0)
    ln = B_CHUNK
    hp = B_HEAD_PAD

    @pl.when(k == 0)
    def _():
        ext_ref[:, 0:8, :] = jnp.zeros((nb, 8, 2 * B_PAD_WIDTH), F32)
        c_ref[...] = jnp.zeros_like(c_ref)
        m_ref[...] = jnp.zeros_like(m_ref)

    first_half = (k % 2) == 0
    r = lax.broadcasted_iota(jnp.int32, (ln, ln), 0)
    c = lax.broadcasted_iota(jnp.int32, (ln, ln), 1)
    causal = c <= r
    lane = lax.broadcasted_iota(jnp.int32, (ln, hp), 1)
    qscale = B_HEAD_DIM ** -0.5

    for b in range(nb):
        ext_ref[b, 8:8 + ln, :] = zb_ref[b, :, 0:2 * B_PAD_WIDTH]
        conv = cw_ref[0:1, :] * ext_ref[b, 5:5 + ln, :]
        for j in range(1, B_CONV):
            conv = conv + cw_ref[j:j + 1, :] * ext_ref[b, 5 + j:5 + j + ln, :]
        ext_ref[b, 0:8, :] = ext_ref[b, ln:ln + 8, :]
        qk = conv * _sigmoid(conv)
        grow = grow_ref[b]
        grow = jnp.where(first_half, grow[:, 0:ln], grow[:, ln:2 * ln])
        gcol = gcol_ref[b]
        for h in range(B_HEADS):
            st = b * B_HEADS + h
            qq = (qk[:, COL_BQ + h * hp:COL_BQ + (h + 1) * hp] * qscale).astype(BF16)
            kk = qk[:, COL_BK + h * hp:COL_BK + (h + 1) * hp]
            vv = zb_ref[b, :, COL_BV + h * hp:COL_BV + (h + 1) * hp]
            vv = jnp.where(lane == B_HEAD_DIM, 1.0, vv).astype(BF16)
            og = zb_ref[b, :, COL_BO + h * hp:COL_BO + (h + 1) * hp]
            bc_col = gcol[:, G_BF + h:G_BF + h + 1]
            li_col = gcol[:, G_BI + h:G_BI + h + 1]
            bc_row = grow[G_BF + h:G_BF + h + 1, :]
            li_row = grow[G_BI + h:G_BI + h + 1, :]
            m = m_ref[st, 0:1, 0:1]

            d = jnp.where(causal, bc_col - bc_row + li_row, -jnp.inf)
            inter = bc_col + m
            m_t = jnp.maximum(inter, jnp.max(d, axis=-1, keepdims=True))
            w_inter = jnp.exp(inter - m_t)
            sc = _dot_nt(qq, kk.astype(BF16)) * jnp.exp(d - m_t)
            cm = c_ref[st]
            num = w_inter * _dot(qq, cm.astype(BF16)) + _dot(sc.astype(BF16), vv)
            den = num[:, B_HEAD_DIM:B_HEAD_DIM + 1]
            hh = num / jnp.maximum(jnp.abs(den), jnp.exp(-m_t))

            b_last = bc_col[ln - 1:ln, :]
            dec = b_last - bc_col + li_col
            m_new = jnp.maximum(b_last + m, jnp.max(dec, axis=0, keepdims=True))
            a = jnp.exp(b_last + m - m_new)
            wk = (jnp.exp(dec - m_new) * kk).astype(BF16)
            c_ref[st] = a * cm + _dot_tn(wk, vv)
            m_ref[st] = jnp.broadcast_to(m_new, (8, 128))

            y = jnp.where(lane < B_HEAD_DIM, _sigmoid(og) * hh, 0.0)
            ms = jnp.sum(y * y, axis=-1, keepdims=True) * (1.0 / B_HEAD_DIM)
            y = y * lax.rsqrt(ms + EPS) * g_ref[:, h * hp:(h + 1) * hp]
            y_ref[b, :, h * hp:(h + 1) * hp] = y.astype(BF16)


def _mlstm(z3, gcol, grow, cw, g_b):
    nb, seq, _ = z3.shape
    nc = seq // B_CHUNK
    return pl.pallas_call(
        functools.partial(_mlstm_body, nb=nb),
        grid=(nc,),
        in_specs=[
            pl.BlockSpec((nb, B_CHUNK, 4 * B_PAD_WIDTH), lambda k: (0, k, 0)),
            pl.BlockSpec((nb, B_CHUNK, GATE_LANES), lambda k: (0, k, 0)),
            pl.BlockSpec((nb, 16, 2 * B_CHUNK), lambda k: (0, 0, k // 2)),
            pl.BlockSpec((B_CONV, 2 * B_PAD_WIDTH), lambda k: (0, 0)),
            pl.BlockSpec((1, B_PAD_WIDTH), lambda k: (0, 0)),
        ],
        out_specs=pl.BlockSpec((nb, B_CHUNK, B_PAD_WIDTH), lambda k: (0, k, 0)),
        out_shape=jax.ShapeDtypeStruct((nb, seq, B_PAD_WIDTH), BF16),
        scratch_shapes=[
            pltpu.VMEM((nb, B_CHUNK + 8, 2 * B_PAD_WIDTH), F32),
            pltpu.VMEM((nb * B_HEADS, B_HEAD_PAD, B_HEAD_PAD), F32),
            pltpu.VMEM((nb * B_HEADS, 8, 128), F32),
        ],
        compiler_params=_cparams(("arbitrary",)),
        name="mlstm",
    )(z3, gcol, grow, cw, g_b)


def _conf_body(ca_ref, cg_ref, dw_ref, db_ref, lg_ref, lb_ref, pw_ref, pb_ref, g_ref, y_ref, ext_ref, *, ts):
    halo = 32

    @pl.when(pl.program_id(1) == 0)
    def _():
        ext_ref[0:halo, :] = jnp.zeros((halo, C_WIDTH), F32)

    ext_ref[halo:halo + ts, :] = ca_ref[0] * _sigmoid(cg_ref[0])
    off = halo - (C_CONV - 1)
    acc = dw_ref[0:1, :] * ext_ref[off:off + ts, :]
    for j in range(1, C_CONV):
        acc = acc + dw_ref[j:j + 1, :] * ext_ref[off + j:off + j + ts, :]
    ext_ref[0:halo, :] = ext_ref[ts:ts + halo, :]
    y = acc + db_ref[...]
    mu = jnp.mean(y, axis=-1, keepdims=True)
    yc = y - mu
    y = yc * lax.rsqrt(jnp.mean(yc * yc, axis=-1, keepdims=True) + LN_EPS)
    y = y * lg_ref[...] + lb_ref[...]
    y = y * _sigmoid(y)
    y = _dot(y.astype(BF16), pw_ref[...]) + pb_ref[...]
    gw = C_WIDTH // C_GROUPS
    for gi in range(C_GROUPS):
        yg = y[:, gi * gw:(gi + 1) * gw]
        yg = yg * lax.rsqrt(jnp.mean(yg * yg, axis=-1, keepdims=True) + EPS) * g_ref[:, gi * gw:(gi + 1) * gw]
        y_ref[0, :, gi * gw:(gi + 1) * gw] = yg.astype(BF16)


def _conformer(z3, dw, db, lg, lb, pw, pb, g_c, *, ts):
    nb, seq, _ = z3.shape
    ca, cg = COL_CA // C_WIDTH, COL_CG // C_WIDTH
    vec = pl.BlockSpec((1, C_WIDTH), lambda b, i: (0, 0))
    return pl.pallas_call(
        functools.partial(_conf_body, ts=ts),
        grid=(nb, seq // ts),
        in_specs=[
            pl.BlockSpec((1, ts, C_WIDTH), lambda b, i: (b, i, ca)),
            pl.BlockSpec((1, ts, C_WIDTH), lambda b, i: (b, i, cg)),
            pl.BlockSpec((C_CONV, C_WIDTH), lambda b, i: (0, 0)),
            vec, vec, vec,
            pl.BlockSpec((C_WIDTH, C_WIDTH), lambda b, i: (0, 0)),
            vec, vec,
        ],
        out_specs=pl.BlockSpec((1, ts, C_WIDTH), lambda b, i: (b, i, 0)),
        out_shape=jax.ShapeDtypeStruct((nb, seq, C_WIDTH), BF16),
        scratch_shapes=[pltpu.VMEM((ts + 32, C_WIDTH), F32)],
        compiler_params=_cparams(("parallel", "arbitrary")),
        name="conformer",
    )(z3, z3, dw, db, lg, lb, pw, pb, g_c)


def _outproj_body(x_ref, ya_ref, yb_ref, yc_ref, wa_ref, wb_ref, wc_ref, o_ref):
    o_ref[...] = (x_ref[...] + _dot(ya_ref[...], wa_ref[...]) + _dot(yb_ref[...], wb_ref[...])
                  + _dot(yc_ref[...], wc_ref[...]))


def _out_proj(x2d, ya, yb, yc, wa, wb, wc, *, tm):
    t, d = x2d.shape
    once = pl.Buffered(1)
    return pl.pallas_call(
        _outproj_body,
        grid=(t // tm,),
        in_specs=[
            pl.BlockSpec((tm, d), lambda i: (i, 0)),
            pl.BlockSpec((tm, ya.shape[1]), lambda i: (i, 0)),
            pl.BlockSpec((tm, yb.shape[1]), lambda i: (i, 0)),
            pl.BlockSpec((tm, yc.shape[1]), lambda i: (i, 0)),
            pl.BlockSpec(wa.shape, lambda i: (0, 0), pipeline_mode=once),
            pl.BlockSpec(wb.shape, lambda i: (0, 0), pipeline_mode=once),
            pl.BlockSpec(wc.shape, lambda i: (0, 0), pipeline_mode=once),
        ],
        out_specs=pl.BlockSpec((tm, d), lambda i: (i, 0)),
        out_shape=jax.ShapeDtypeStruct((t, d), F32),
        compiler_params=_cparams(("parallel",)),
        name="out_proj",
    )(x2d, ya, yb, yc, wa, wb, wc)


def _peerq_body(x_ref, g_ref, w_ref, q_ref, xn_ref):
    xf = x_ref[...]
    ms = jnp.mean(xf * xf, axis=-1, keepdims=True)
    xn = (xf * lax.rsqrt(ms + EPS) * g_ref[...]).astype(BF16)
    xn_ref[...] = xn
    q_ref[...] = _dot(xn, w_ref[...])


def _peer_q(x2d, g, w, *, tm):
    t, d = x2d.shape
    n = w.shape[1]
    return pl.pallas_call(
        _peerq_body,
        grid=(t // tm,),
        in_specs=[
            pl.BlockSpec((tm, d), lambda i: (i, 0)),
            pl.BlockSpec((1, d), lambda i: (0, 0)),
            pl.BlockSpec((d, n), lambda i: (0, 0), pipeline_mode=pl.Buffered(1)),
        ],
        out_specs=[
            pl.BlockSpec((tm, n), lambda i: (i, 0)),
            pl.BlockSpec((tm, d), lambda i: (i, 0)),
        ],
        out_shape=[
            jax.ShapeDtypeStruct((t, n), F32),
            jax.ShapeDtypeStruct((t, d), BF16),
        ],
        compiler_params=_cparams(("parallel",)),
        name="peer_q",
    )(x2d, g, w)


def _extract_top(s, rowid, n_out):
    nrow = s.shape[0]
    rank = jnp.full(s.shape, float(n_out), F32)
    vals = []
    for it in range(n_out):
        m = jnp.max(s, axis=0, keepdims=True)
        idx = jnp.min(jnp.where(s == m, rowid, float(nrow)), axis=0, keepdims=True)
        hit = rowid == idx
        vals.append(m)
        rank = jnp.where(hit, float(it), rank)
        s = jnp.where(hit, -jnp.inf, s)
    return vals, rank, rank < float(n_out)


def _extract_top_distinct(s, n_out):
    rank = jnp.full(s.shape, float(n_out), F32)
    vals = []
    for it in range(n_out):
        m = jnp.max(s, axis=0, keepdims=True)
        hit = s == m
        vals.append(m)
        rank = jnp.where(hit, float(it), rank)
        s = jnp.where(hit, -jnp.inf, s)
    taken = rank < float(n_out)
    return vals, rank, taken, jnp.sum(jnp.where(taken, 1.0, 0.0), axis=0, keepdims=True)


_CAND_SHORT = 8


def _topk_body(q_ref, keys_ref, c1_ref, w1_ref, r2_ref, e2_ref, *, tt):
    rowid = lax.broadcasted_iota(jnp.int32, (N_KEYS, tt), 0).astype(F32)
    rowid2 = lax.broadcasted_iota(jnp.int32, (P_TOPK * P_TOPK, tt), 0).astype(F32)

    def pair_grid(x1, x2, op):
        parts = [op(x1[0:1, :], x2)]
        parts += [op(x1[a:a + 1, :], x2[0:_CAND_SHORT, :]) for a in range(1, _CAND_SHORT)]
        parts += [op(x1[_CAND_SHORT:, :], x2[0:1, :])]
        return jnp.concatenate(parts, axis=0)

    def finish(h, scores, v1, v2, ranks, counts, zsum):
        cnt = jnp.zeros((N_KEYS, tt), F32)
        for a in range(P_TOPK):
            cnt = jnp.where(ranks[0] == float(a), counts[a], cnt)
        c1_ref[h] = cnt
        w1_ref[h] = jnp.exp(scores[0] - v1[0:1, :]) / zsum
        r2_ref[h] = ranks[1].astype(BF16)
        e2_ref[h] = jnp.exp(scores[1] - v2[0:1, :]).astype(BF16)

    def per_head(h, carry):
        scores = []
        for c in range(2):
            kh, kl = _split_bf16(keys_ref[2 * h + c])
            col = pl.multiple_of((2 * h + c) * P_HALF, P_HALF)
            qh, ql = _split_bf16(q_ref[:, pl.ds(col, P_HALF)])
            scores.append(_dot3(kh, kl, qh, ql, _dot_nt))

        tops, ranks, clean = [], [], None
        for c in range(2):
            vals, rank, _, lost = _extract_top_distinct(scores[c], P_TOPK)
            tops.append(jnp.concatenate(vals, axis=0))
            ranks.append(rank)
            ok = lost == float(P_TOPK)
            clean = ok if clean is None else (clean & ok)
        v1, v2 = tops
        cand = pair_grid(v1, v2, lambda x, y: x + y)
        _, _, picked, lost = _extract_top_distinct(cand, P_TOPK)
        clean = clean & (lost == float(P_TOPK))
        ev1 = jnp.exp(v1 - v1[0:1, :])
        ev2 = jnp.exp(v2 - v2[0:1, :])
        prod = pair_grid(ev1, ev2, lambda x, y: x * y)
        zsum = jnp.sum(jnp.where(picked, prod, 0.0), axis=0, keepdims=True)
        npick = jnp.where(picked, 1.0, 0.0)
        counts = [jnp.sum(npick[0:P_TOPK, :], axis=0, keepdims=True)]
        for a in range(1, _CAND_SHORT):
            r0 = P_TOPK + (a - 1) * _CAND_SHORT
            counts.append(jnp.sum(npick[r0:r0 + _CAND_SHORT, :], axis=0, keepdims=True))
        r0 = P_TOPK + (_CAND_SHORT - 1) * _CAND_SHORT
        counts += [npick[r0 + a:r0 + a + 1, :] for a in range(P_TOPK - _CAND_SHORT)]
        finish(h, scores, v1, v2, ranks, counts, zsum)

        @pl.when(jnp.min(jnp.where(clean, 1.0, 0.0)) < 0.5)
        def _():
            tops, ranks = [], []
            for c in range(2):
                vals, rank, _ = _extract_top(scores[c], rowid, P_TOPK)
                tops.append(jnp.concatenate(vals, axis=0))
                ranks.append(rank)
            v1, v2 = tops
            cand = jnp.concatenate([v1[a:a + 1, :] + v2 for a in range(P_TOPK)], axis=0)
            _, _, picked = _extract_top(cand, rowid2, P_TOPK)
            ev1 = jnp.exp(v1 - v1[0:1, :])
            ev2 = jnp.exp(v2 - v2[0:1, :])
            prod = jnp.concatenate([ev1[a:a + 1, :] * ev2 for a in range(P_TOPK)], axis=0)
            zsum = jnp.sum(jnp.where(picked, prod, 0.0), axis=0, keepdims=True)
            npick = jnp.where(picked, 1.0, 0.0)
            counts = [jnp.sum(npick[a * P_TOPK:(a + 1) * P_TOPK, :], axis=0, keepdims=True) for a in range(P_TOPK)]
            finish(h, scores, v1, v2, ranks, counts, zsum)

        return carry

    lax.fori_loop(0, P_HEADS, per_head, 0)


def _peer_topk(q, keys16, *, tt):
    t = q.shape[0]
    st_shape = jax.ShapeDtypeStruct((P_HEADS, N_KEYS, t), F32)
    st_shape_b = jax.ShapeDtypeStruct((P_HEADS, N_KEYS, t), BF16)
    st_spec = pl.BlockSpec((P_HEADS, N_KEYS, tt), lambda i: (0, 0, i))
    return pl.pallas_call(
        functools.partial(_topk_body, tt=tt),
        grid=(t // tt,),
        in_specs=[
            pl.BlockSpec((tt, q.shape[1]), lambda i: (i, 0)),
            pl.BlockSpec(keys16.shape, lambda i: (0, 0, 0)),
        ],
        out_specs=[st_spec] * 4,
        out_shape=[st_shape, st_shape, st_shape_b, st_shape_b],
        compiler_params=_cparams(("parallel",)),
        name="peer_topk",
    )(q, keys16)


def _peer_dense_body(xn_ref, u_ref, v_ref, c1_ref, w1_ref, r2_ref, e2_ref, x_ref, g_ref, o_ref, c1s_ref, w1s_ref, *hg_refs,
                     tm, te, tc, rb, final):
    e = pl.program_id(1)

    @pl.when(e == 0)
    def _():
        o_ref[...] = x_ref[...]

    ub = u_ref[...].astype(BF16)
    vb = v_ref[...].astype(BF16)

    ng = te // N_KEYS
    nchunk = tm // tc
    which = e % (8 // ng)

    for src, dst in ((c1_ref, c1s_ref), (w1_ref, w1s_ref)):
        rows = src[:, 0:ng, :]
        for g in range(1, 8 // ng):
            rows = jnp.where(which == g, src[:, g * ng:(g + 1) * ng, :], rows)
        dst[:, 0:ng, :] = rows

    def bf16_row(ref, h, ii, t0):
        return ref[h, ii:ii + 1, t0:t0 + tc].astype(BF16)

    def stage_a(ci):
        t0 = ci * tc
        return _dot_nt(ub, xn_ref[t0:t0 + tc, :])

    def stage_b(ci, hts):
        t0 = ci * tc
        for ii in range(ng):
            rows = [(bf16_row(c1s_ref, h, ii, t0), bf16_row(w1s_ref, h, ii, t0)) for h in range(P_HEADS)]
            for j0 in range(0, N_KEYS, rb):
                r0 = ii * N_KEYS + j0
                ht = hts[r0:r0 + rb, :]
                act = ((0.5 * ht) * (1.0 + lax.erf(ht * (2.0 ** -0.5)))).astype(BF16)
                gate = jnp.zeros((rb, tc), BF16)
                for h in range(P_HEADS):
                    cnt, w1 = rows[h]
                    gate = gate + jnp.where(r2_ref[h, j0:j0 + rb, t0:t0 + tc] < cnt,
                                            e2_ref[h, j0:j0 + rb, t0:t0 + tc] * w1, jnp.zeros((), BF16))
                hg_refs[ci][r0:r0 + rb, :] = act * gate

    def stage_c(ci):
        t0 = ci * tc
        o_ref[t0:t0 + tc, :] += _dot_tn(hg_refs[ci][...], vb)

    ahead = 2
    hts = {ci: stage_a(ci) for ci in range(min(ahead, nchunk))}
    for ci in range(nchunk):
        if ci + ahead < nchunk:
            hts[ci + ahead] = stage_a(ci + ahead)
        stage_b(ci, hts.pop(ci))
        stage_c(ci)

    if final:
        @pl.when(e == pl.num_programs(1) - 1)
        def _():
            y = o_ref[...]
            o_ref[...] = y * lax.rsqrt(jnp.mean(y * y, axis=-1, keepdims=True) + EPS) * g_ref[...]


def _peer_dense(xn, u, v, c1, w1, r2, e2, x2d, g, *, tm, te, tc, rb, final):
    t, d = xn.shape
    ne = u.shape[0]
    ng = te // N_KEYS
    once = pl.Buffered(1)
    row_spec = pl.BlockSpec((P_HEADS, 8, tm), lambda i, e: (0, e // (8 // ng), i))
    full_spec = pl.BlockSpec((P_HEADS, N_KEYS, tm), lambda i, e: (0, 0, i), pipeline_mode=once)
    tok_spec = pl.BlockSpec((tm, d), lambda i, e: (i, 0), pipeline_mode=once)
    return pl.pallas_call(
        functools.partial(_peer_dense_body, tm=tm, te=te, tc=tc, rb=rb, final=final),
        grid=(t // tm, ne // te),
        in_specs=[
            tok_spec,
            pl.BlockSpec((te, d), lambda i, e: (e, 0)),
            pl.BlockSpec((te, d), lambda i, e: (e, 0)),
            row_spec, row_spec, full_spec, full_spec,
            tok_spec,
            pl.BlockSpec((1, d), lambda i, e: (0, 0)),
        ],
        out_specs=pl.BlockSpec((tm, d), lambda i, e: (i, 0), pipeline_mode=once),
        out_shape=jax.ShapeDtypeStruct((t, d), F32),
        scratch_shapes=[pltpu.VMEM((P_HEADS, 8, tm), F32), pltpu.VMEM((P_HEADS, 8, tm), F32)]
        + [pltpu.VMEM((te, tc), BF16)] * (tm // tc),
        compiler_params=_cparams(("parallel", "arbitrary")),
        name="peer_dense",
    )(xn, u, v, c1, w1, r2, e2, x2d, g)


def _pad_heads(w):
    lead = w.shape[:-1]
    w = w.reshape(lead + (B_HEADS, B_HEAD_DIM))
    w = jnp.pad(w, [(0, 0)] * len(lead) + [(0, 0), (0, B_HEAD_PAD - B_HEAD_DIM)])
    return w.reshape(lead + (B_PAD_WIDTH,))


def _regroup_w_in(w):
    sizes = (A_WIDTH, A_WIDTH, A_WIDTH, A_HEADS, 2 * B_WIDTH, B_WIDTH, B_WIDTH, B_HEADS, B_HEADS, C_WIDTH, C_WIDTH)
    parts, p = [], 0
    for sz in sizes:
        parts.append(w[:, p:p + sz])
        p += sz
    aq, ak, av, af, bqk, bv, bo, bi, bf, ca, cg = parts
    d = w.shape[0]
    main = jnp.concatenate(
        [_pad_heads(bqk[:, :B_WIDTH]), _pad_heads(bqk[:, B_WIDTH:]), _pad_heads(bv), _pad_heads(bo),
         ca, cg, aq, ak, av, jnp.zeros((d, Z_WIDTH - Z_USED), w.dtype)], axis=1)
    gates = jnp.concatenate([af, bi, bf, jnp.zeros((d, GATE_LANES - A_HEADS - 2 * B_HEADS), w.dtype)], axis=1)
    return main.astype(BF16), gates


def _tile_sizes(t, seq):
    return dict(
        tm_in=min(1024, t), tn_in=1280,
        tb_gate=min(256, seq),
        tq=min(512, seq),
        ts_conf=min(256, seq),
        tm_out=min(512, t),
        tm_q=min(512, t),
        tt=min(256, t),
        tm_pd=min(1024, t), te=512, tc_pd=min(256, t), rb_pd=32,
    )


def kernel(x, w_in, fox_f_bias, mlstm_conv_w, mlstm_i_bias, mlstm_f_bias, conf_dw_w, conf_dw_b, conf_ln_g, conf_ln_b, conf_pw_w, conf_pw_b, out_norm_g, w_out, norm1_g, norm2_g, peer_wq, peer_keys, peer_u, peer_v, final_g):
    batch, seq, d = x.shape
    t = batch * seq
    depth = w_in.shape[0]
    ts = _tile_sizes(t, seq)
    xf = x.reshape(t, d)
    row = lambda v: v.reshape(1, -1)

    for l in range(depth):
        w_main, w_gate = _regroup_w_in(w_in[l])
        wgh, wgl = _split_bf16(w_gate)
        z, zg = _in_proj(xf, row(norm1_g[l]), w_main, wgh, wgl, tm=ts["tm_in"], tn=ts["tn_in"])
        bias = jnp.concatenate([fox_f_bias[l], mlstm_i_bias[l], mlstm_f_bias[l],
                                jnp.zeros((GATE_LANES - A_HEADS - 2 * B_HEADS,), F32)]).reshape(1, GATE_LANES)
        gcol, grow = _gates(zg.reshape(batch, seq, GATE_LANES), bias, tb=ts["tb_gate"])
        g_a = row(out_norm_g[l, :A_WIDTH])
        g_b = row(_pad_heads(out_norm_g[l, A_WIDTH:A_WIDTH + B_WIDTH]))
        g_c = row(out_norm_g[l, A_WIDTH + B_WIDTH:])
        z3 = z.reshape(batch, seq, Z_WIDTH)
        ya = _attention(z, gcol, grow.reshape(batch, 16, 1, seq), g_a, batch=batch, seq=seq, tq=ts["tq"])
        cw = jnp.concatenate([_pad_heads(mlstm_conv_w[l, :, :B_WIDTH]), _pad_heads(mlstm_conv_w[l, :, B_WIDTH:])], axis=1)
        yb = _mlstm(z3, gcol, grow, cw, g_b)
        yc = _conformer(z3, conf_dw_w[l], row(conf_dw_b[l]), row(conf_ln_g[l]), row(conf_ln_b[l]),
                        conf_pw_w[l].astype(BF16), row(conf_pw_b[l]), g_c, ts=ts["ts_conf"])
        wo = w_out[l]
        wa = wo[:A_WIDTH].astype(BF16)
        wb = jnp.pad(wo[A_WIDTH:A_WIDTH + B_WIDTH].reshape(B_HEADS, B_HEAD_DIM, d),
                     ((0, 0), (0, B_HEAD_PAD - B_HEAD_DIM), (0, 0))).reshape(B_PAD_WIDTH, d).astype(BF16)
        wc = wo[A_WIDTH + B_WIDTH:].astype(BF16)
        xf = _out_proj(xf, ya, yb.reshape(t, B_PAD_WIDTH), yc.reshape(t, C_WIDTH), wa, wb, wc,
                       tm=ts["tm_out"])

        q, xn = _peer_q(xf, row(norm2_g[l]), peer_wq[l].astype(BF16), tm=ts["tm_q"])
        keys16 = peer_keys[l].reshape(2 * P_HEADS, N_KEYS, P_HALF)
        c1, w1, r2, e2 = _peer_topk(q, keys16, tt=ts["tt"])
        xf = _peer_dense(xn, peer_u[l], peer_v[l], c1, w1, r2, e2, xf, row(final_g),
                         tm=ts["tm_pd"], te=ts["te"], tc=ts["tc_pd"], rb=ts["rb_pd"], final=(l == depth - 1))

    return xf.reshape(batch, seq, d)
```

```python
import functools

import jax
import jax.numpy as jnp
from jax import lax
from jax.experimental import pallas as pl
from jax.experimental.pallas import tpu as pltpu

F32 = jnp.float32
BF16 = jnp.bfloat16

EPS = 1e-6
LN_EPS = 1e-5

A_HEADS = 6
A_HEAD_DIM = 128
A_WIDTH = A_HEADS * A_HEAD_DIM
B_HEADS = 4
B_HEAD_DIM = 192
B_HEAD_PAD = 256
B_WIDTH = B_HEADS * B_HEAD_DIM
B_PAD_WIDTH = B_HEADS * B_HEAD_PAD
B_CHUNK = 64
B_CONV = 4
C_GROUPS = 4
C_WIDTH = 512
C_CONV = 31
P_HEADS = 8
P_HALF = 128
N_KEYS = 128
P_TOPK = 16
GATE_LANES = 128

COL_BQ = 0
COL_BK = B_PAD_WIDTH
COL_BV = 2 * B_PAD_WIDTH
COL_BO = 3 * B_PAD_WIDTH
COL_CA = 4 * B_PAD_WIDTH
COL_CG = COL_CA + C_WIDTH
COL_AQ = COL_CG + C_WIDTH
COL_AK = COL_AQ + A_WIDTH
COL_AV = COL_AK + A_WIDTH
Z_USED = COL_AV + A_WIDTH
Z_WIDTH = 7680
G_AF = 0
G_BI = A_HEADS
G_BF = A_HEADS + B_HEADS

VMEM_LIMIT = 60 * 1024 * 1024


def _cparams(sem):
    return pltpu.CompilerParams(dimension_semantics=sem, vmem_limit_bytes=VMEM_LIMIT)


def _split_bf16(a):
    hi = a.astype(BF16)
    lo = (a - hi.astype(F32)).astype(BF16)
    return hi, lo


def _dot(a, b):
    return jnp.dot(a, b, preferred_element_type=F32)


def _dot_nt(a, b):
    return lax.dot_general(a, b, (((1,), (1,)), ((), ())), preferred_element_type=F32)


def _dot_tn(a, b):
    return lax.dot_general(a, b, (((0,), (0,)), ((), ())), preferred_element_type=F32)


def _dot3(ah, al, bh, bl, dot):
    return dot(ah, bh) + dot(ah, bl) + dot(al, bh)


def _sigmoid(x):
    return 1.0 / (1.0 + jnp.exp(-x))


def _inproj_body(x_ref, g_ref, w_ref, wgh_ref, wgl_ref, z_ref, zg_ref, xn_ref):
    @pl.when(pl.program_id(1) == 0)
    def _():
        xf = x_ref[...]
        ms = jnp.mean(xf * xf, axis=-1, keepdims=True)
        xn = xf * lax.rsqrt(ms + EPS) * g_ref[...]
        hi, lo = _split_bf16(xn)
        xn_ref[...] = hi
        zg_ref[...] = _dot3(hi, lo, wgh_ref[...], wgl_ref[...], _dot)

    z_ref[...] = _dot(xn_ref[...], w_ref[...])


def _in_proj(x2d, g, w, wgh, wgl, *, tm, tn):
    t, d = x2d.shape
    n = w.shape[1]
    return pl.pallas_call(
        _inproj_body,
        grid=(t // tm, n // tn),
        in_specs=[
            pl.BlockSpec((tm, d), lambda i, j: (i, 0), pipeline_mode=pl.Buffered(1)),
            pl.BlockSpec((1, d), lambda i, j: (0, 0)),
            pl.BlockSpec((d, tn), lambda i, j: (0, j)),
            pl.BlockSpec((d, GATE_LANES), lambda i, j: (0, 0)),
            pl.BlockSpec((d, GATE_LANES), lambda i, j: (0, 0)),
        ],
        out_specs=[
            pl.BlockSpec((tm, tn), lambda i, j: (i, j)),
            pl.BlockSpec((tm, GATE_LANES), lambda i, j: (i, 0)),
        ],
        out_shape=[
            jax.ShapeDtypeStruct((t, n), F32),
            jax.ShapeDtypeStruct((t, GATE_LANES), F32),
        ],
        scratch_shapes=[pltpu.VMEM((tm, d), BF16)],
        compiler_params=_cparams(("parallel", "arbitrary")),
        name="in_proj",
    )(x2d, g, w, wgh, wgl)


def _gates_body(zg_ref, bias_ref, col_ref, row_ref, carry_ref, *, tb):
    @pl.when(pl.program_id(1) == 0)
    def _():
        carry_ref[...] = jnp.zeros_like(carry_ref)

    v = zg_ref[0] + bias_ref[...]
    ls = jnp.minimum(v, 0.0) - jnp.log1p(jnp.exp(-jnp.abs(v)))
    r = lax.broadcasted_iota(jnp.int32, (tb, tb), 0)
    c = lax.broadcasted_iota(jnp.int32, (tb, tb), 1)
    tri = (c <= r).astype(F32)
    tri_chunk = jnp.where((r // B_CHUNK) == (c // B_CHUNK), tri, 0.0)
    csum = jnp.dot(tri, ls, precision=lax.Precision.HIGHEST, preferred_element_type=F32) + carry_ref[...]
    bcum = jnp.dot(tri_chunk, ls, precision=lax.Precision.HIGHEST, preferred_element_type=F32)
    carry_ref[...] = csum[tb - 1:tb, :]
    lane = lax.broadcasted_iota(jnp.int32, (tb, GATE_LANES), 1)
    out = jnp.where(lane < G_BI, csum, jnp.where(lane < G_BF, v, bcum))
    col_ref[0] = out
    row_ref[0] = out.T[0:16, :]


def _gates(zg3, bias, *, tb):
    b, s, _ = zg3.shape
    return pl.pallas_call(
        functools.partial(_gates_body, tb=tb),
        grid=(b, s // tb),
        in_specs=[
            pl.BlockSpec((1, tb, GATE_LANES), lambda i, j: (i, j, 0)),
            pl.BlockSpec((1, GATE_LANES), lambda i, j: (0, 0)),
        ],
        out_specs=[
            pl.BlockSpec((1, tb, GATE_LANES), lambda i, j: (i, j, 0)),
            pl.BlockSpec((1, 16, tb), lambda i, j: (i, 0, j)),
        ],
        out_shape=[
            jax.ShapeDtypeStruct((b, s, GATE_LANES), F32),
            jax.ShapeDtypeStruct((b, 16, s), F32),
        ],
        scratch_shapes=[pltpu.VMEM((1, GATE_LANES), F32)],
        compiler_params=_cparams(("parallel", "arbitrary")),
        name="gates",
    )(zg3, bias)


def _attn_body(q_ref, k_ref, v_ref, ccol_ref, crow_ref, g_ref, o_ref, kb_ref, vb_ref, m_ref, l_ref, acc_ref, *, tq):
    h = pl.program_id(1)
    qi = pl.program_id(2)
    log2e = 1.4426950408889634
    scale2 = (A_HEAD_DIM ** -0.5) * log2e

    @pl.when(qi == 0)
    def _():
        kb_ref[...] = k_ref[...].astype(BF16)
        vb_ref[...] = v_ref[...].astype(BF16)

    q = q_ref[...].astype(BF16)
    lane = lax.broadcasted_iota(jnp.int32, (tq, GATE_LANES), 1)
    ct2 = jnp.sum(jnp.where(lane == h, ccol_ref[0], 0.0), axis=-1, keepdims=True) * log2e
    m_ref[...] = jnp.full(m_ref.shape, -jnp.inf, F32)
    l_ref[...] = jnp.zeros_like(l_ref)
    acc_ref[...] = jnp.zeros_like(acc_ref)

    def logits(j):
        k0 = pl.multiple_of(j * tq, tq)
        cs2 = crow_ref[0, 0, :, pl.ds(k0, tq)] * log2e
        return _dot_nt(q, kb_ref[pl.ds(k0, tq), :]) * scale2 + (ct2 - cs2)

    def softmax(s):
        m_prev = m_ref[...]
        m_new = jnp.maximum(m_prev, jnp.max(s, axis=-1, keepdims=True))
        p = jnp.exp2(s - m_new)
        alpha = jnp.exp2(m_prev - m_new)
        l_ref[...] = alpha * l_ref[...] + jnp.sum(p, axis=-1, keepdims=True)
        m_ref[...] = m_new
        return p.astype(BF16), alpha

    def values(j, p, alpha):
        k0 = pl.multiple_of(j * tq, tq)
        acc_ref[...] = alpha * acc_ref[...] + _dot(p, vb_ref[pl.ds(k0, tq), :])

    def body(j, s_cur):
        s_next = logits(j + 1)
        p, alpha = softmax(s_cur)
        values(j, p, alpha)
        return s_next

    s_diag = lax.fori_loop(0, qi, body, logits(0))
    row = lax.broadcasted_iota(jnp.int32, (tq, tq), 0)
    col = lax.broadcasted_iota(jnp.int32, (tq, tq), 1)
    p, alpha = softmax(jnp.where(col <= row, s_diag, -jnp.inf))
    values(qi, p, alpha)
    o = acc_ref[...] / l_ref[...]
    o = o * lax.rsqrt(jnp.mean(o * o, axis=-1, keepdims=True) + EPS) * g_ref[...]
    o_ref[...] = o.astype(BF16)


def _attention(z2d, gcol, grow, g_a, *, batch, seq, tq):
    nq = seq // tq
    qc, kc, vc = COL_AQ // 128, COL_AK // 128, COL_AV // 128
    return pl.pallas_call(
        functools.partial(_attn_body, tq=tq),
        grid=(batch, A_HEADS, nq),
        in_specs=[
            pl.BlockSpec((tq, 128), lambda b, h, i: (b * nq + i, qc + h)),
            pl.BlockSpec((seq, 128), lambda b, h, i: (b, kc + h)),
            pl.BlockSpec((seq, 128), lambda b, h, i: (b, vc + h)),
            pl.BlockSpec((1, tq, GATE_LANES), lambda b, h, i: (b, i, 0)),
            pl.BlockSpec((1, 1, 1, seq), lambda b, h, i: (b, h, 0, 0)),
            pl.BlockSpec((1, 128), lambda b, h, i: (0, h)),
        ],
        out_specs=pl.BlockSpec((tq, 128), lambda b, h, i: (b * nq + i, h)),
        out_shape=jax.ShapeDtypeStruct((batch * seq, A_WIDTH), BF16),
        scratch_shapes=[pltpu.VMEM((seq, 128), BF16), pltpu.VMEM((seq, 128), BF16),
                        pltpu.VMEM((tq, 1), F32), pltpu.VMEM((tq, 1), F32), pltpu.VMEM((tq, 128), F32)],
        compiler_params=_cparams(("parallel", "parallel", "arbitrary")),
        name="fox_attention",
    )(z2d, z2d, z2d, gcol, grow, g_a)


def _mlstm_body(zb_ref, gcol_ref, grow_ref, cw_ref, g_ref, y_ref, ext_ref, c_ref, m_ref, *, nb):
    k = pl.program_id(0)
    ln = B_CHUNK
    hp = B_HEAD_PAD

    @pl.when(k == 0)
    def _():
        ext_ref[:, 0:8, :] = jnp.zeros((nb, 8, 2 * B_PAD_WIDTH), F32)
        c_ref[...] = jnp.zeros_like(c_ref)
        m_ref[...] = jnp.zeros_like(m_ref)

    first_half = (k % 2) == 0
    r = lax.broadcasted_iota(jnp.int32, (ln, ln), 0)
    c = lax.broadcasted_iota(jnp.int32, (ln, ln), 1)
    causal = c <= r
    lane = lax.broadcasted_iota(jnp.int32, (ln, hp), 1)
    qscale = B_HEAD_DIM ** -0.5

    for b in range(nb):
        ext_ref[b, 8:8 + ln, :] = zb_ref[b, :, 0:2 * B_PAD_WIDTH]
        conv = cw_ref[0:1, :] * ext_ref[b, 5:5 + ln, :]
        for j in range(1, B_CONV):
            conv = conv + cw_ref[j:j + 1, :] * ext_ref[b, 5 + j:5 + j + ln, :]
        ext_ref[b, 0:8, :] = ext_ref[b, ln:ln + 8, :]
        qk = conv * _sigmoid(conv)
        grow = grow_ref[b]
        grow = jnp.where(first_half, grow[:, 0:ln], grow[:, ln:2 * ln])
        gcol = gcol_ref[b]
        for h in range(B_HEADS):
            st = b * B_HEADS + h
            qq = (qk[:, COL_BQ + h * hp:COL_BQ + (h + 1) * hp] * qscale).astype(BF16)
            kk = qk[:, COL_BK + h * hp:COL_BK + (h + 1) * hp]
            vv = zb_ref[b, :, COL_BV + h * hp:COL_BV + (h + 1) * hp]
            vv = jnp.where(lane == B_HEAD_DIM, 1.0, vv).astype(BF16)
            og = zb_ref[b, :, COL_BO + h * hp:COL_BO + (h + 1) * hp]
            bc_col = gcol[:, G_BF + h:G_BF + h + 1]
            li_col = gcol[:, G_BI + h:G_BI + h + 1]
            bc_row = grow[G_BF + h:G_BF + h + 1, :]
            li_row = grow[G_BI + h:G_BI + h + 1, :]
            m = m_ref[st, 0:1, 0:1]

            d = jnp.where(causal, bc_col - bc_row + li_row, -jnp.inf)
            inter = bc_col + m
            m_t = jnp.maximum(inter, jnp.max(d, axis=-1, keepdims=True))
            w_inter = jnp.exp(inter - m_t)
            sc = _dot_nt(qq, kk.astype(BF16)) * jnp.exp(d - m_t)
            cm = c_ref[st]
            num = w_inter * _dot(qq, cm.astype(BF16)) + _dot(sc.astype(BF16), vv)
            den = num[:, B_HEAD_DIM:B_HEAD_DIM + 1]
            hh = num / jnp.maximum(jnp.abs(den), jnp.exp(-m_t))

            b_last = bc_col[ln - 1:ln, :]
            dec = b_last - bc_col + li_col
            m_new = jnp.maximum(b_last + m, jnp.max(dec, axis=0, keepdims=True))
            a = jnp.exp(b_last + m - m_new)
            wk = (jnp.exp(dec - m_new) * kk).astype(BF16)
            c_ref[st] = a * cm + _dot_tn(wk, vv)
            m_ref[st] = jnp.broadcast_to(m_new, (8, 128))

            y = jnp.where(lane < B_HEAD_DIM, _sigmoid(og) * hh, 0.0)
            ms = jnp.sum(y * y, axis=-1, keepdims=True) * (1.0 / B_HEAD_DIM)
            y = y * lax.rsqrt(ms + EPS) * g_ref[:, h * hp:(h + 1) * hp]
            y_ref[b, :, h * hp:(h + 1) * hp] = y.astype(BF16)


def _mlstm(z3, gcol, grow, cw, g_b):
    nb, seq, _ = z3.shape
    nc = seq // B_CHUNK
    return pl.pallas_call(
        functools.partial(_mlstm_body, nb=nb),
        grid=(nc,),
        in_specs=[
            pl.BlockSpec((nb, B_CHUNK, 4 * B_PAD_WIDTH), lambda k: (0, k, 0)),
            pl.BlockSpec((nb, B_CHUNK, GATE_LANES), lambda k: (0, k, 0)),
            pl.BlockSpec((nb, 16, 2 * B_CHUNK), lambda k: (0, 0, k // 2)),
            pl.BlockSpec((B_CONV, 2 * B_PAD_WIDTH), lambda k: (0, 0)),
            pl.BlockSpec((1, B_PAD_WIDTH), lambda k: (0, 0)),
        ],
        out_specs=pl.BlockSpec((nb, B_CHUNK, B_PAD_WIDTH), lambda k: (0, k, 0)),
        out_shape=jax.ShapeDtypeStruct((nb, seq, B_PAD_WIDTH), BF16),
        scratch_shapes=[
            pltpu.VMEM((nb, B_CHUNK + 8, 2 * B_PAD_WIDTH), F32),
            pltpu.VMEM((nb * B_HEADS, B_HEAD_PAD, B_HEAD_PAD), F32),
            pltpu.VMEM((nb * B_HEADS, 8, 128), F32),
        ],
        compiler_params=_cparams(("arbitrary",)),
        name="mlstm",
    )(z3, gcol, grow, cw, g_b)


def _conf_body(ca_ref, cg_ref, dw_ref, db_ref, lg_ref, lb_ref, pw_ref, pb_ref, g_ref, y_ref, ext_ref, *, ts):
    halo = 32

    @pl.when(pl.program_id(1) == 0)
    def _():
        ext_ref[0:halo, :] = jnp.zeros((halo, C_WIDTH), F32)

    ext_ref[halo:halo + ts, :] = ca_ref[0] * _sigmoid(cg_ref[0])
    off = halo - (C_CONV - 1)
    acc = dw_ref[0:1, :] * ext_ref[off:off + ts, :]
    for j in range(1, C_CONV):
        acc = acc + dw_ref[j:j + 1, :] * ext_ref[off + j:off + j + ts, :]
    ext_ref[0:halo, :] = ext_ref[ts:ts + halo, :]
    y = acc + db_ref[...]
    mu = jnp.mean(y, axis=-1, keepdims=True)
    yc = y - mu
    y = yc * lax.rsqrt(jnp.mean(yc * yc, axis=-1, keepdims=True) + LN_EPS)
    y = y * lg_ref[...] + lb_ref[...]
    y = y * _sigmoid(y)
    y = _dot(y.astype(BF16), pw_ref[...]) + pb_ref[...]
    gw = C_WIDTH // C_GROUPS
    for gi in range(C_GROUPS):
        yg = y[:, gi * gw:(gi + 1) * gw]
        yg = yg * lax.rsqrt(jnp.mean(yg * yg, axis=-1, keepdims=True) + EPS) * g_ref[:, gi * gw:(gi + 1) * gw]
        y_ref[0, :, gi * gw:(gi + 1) * gw] = yg.astype(BF16)


def _conformer(z3, dw, db, lg, lb, pw, pb, g_c, *, ts):
    nb, seq, _ = z3.shape
    ca, cg = COL_CA // C_WIDTH, COL_CG // C_WIDTH
    vec = pl.BlockSpec((1, C_WIDTH), lambda b, i: (0, 0))
    return pl.pallas_call(
        functools.partial(_conf_body, ts=ts),
        grid=(nb, seq // ts),
        in_specs=[
            pl.BlockSpec((1, ts, C_WIDTH), lambda b, i: (b, i, ca)),
            pl.BlockSpec((1, ts, C_WIDTH), lambda b, i: (b, i, cg)),
            pl.BlockSpec((C_CONV, C_WIDTH), lambda b, i: (0, 0)),
            vec, vec, vec,
            pl.BlockSpec((C_WIDTH, C_WIDTH), lambda b, i: (0, 0)),
            vec, vec,
        ],
        out_specs=pl.BlockSpec((1, ts, C_WIDTH), lambda b, i: (b, i, 0)),
        out_shape=jax.ShapeDtypeStruct((nb, seq, C_WIDTH), BF16),
        scratch_shapes=[pltpu.VMEM((ts + 32, C_WIDTH), F32)],
        compiler_params=_cparams(("parallel", "arbitrary")),
        name="conformer",
    )(z3, z3, dw, db, lg, lb, pw, pb, g_c)


def _outproj_body(x_ref, ya_ref, yb_ref, yc_ref, wa_ref, wb_ref, wc_ref, o_ref):
    o_ref[...] = (x_ref[...] + _dot(ya_ref[...], wa_ref[...]) + _dot(yb_ref[...], wb_ref[...])
                  + _dot(yc_ref[...], wc_ref[...]))


def _out_proj(x2d, ya, yb, yc, wa, wb, wc, *, tm):
    t, d = x2d.shape
    once = pl.Buffered(1)
    return pl.pallas_call(
        _outproj_body,
        grid=(t // tm,),
        in_specs=[
            pl.BlockSpec((tm, d), lambda i: (i, 0)),
            pl.BlockSpec((tm, ya.shape[1]), lambda i: (i, 0)),
            pl.BlockSpec((tm, yb.shape[1]), lambda i: (i, 0)),
            pl.BlockSpec((tm, yc.shape[1]), lambda i: (i, 0)),
            pl.BlockSpec(wa.shape, lambda i: (0, 0), pipeline_mode=once),
            pl.BlockSpec(wb.shape, lambda i: (0, 0), pipeline_mode=once),
            pl.BlockSpec(wc.shape, lambda i: (0, 0), pipeline_mode=once),
        ],
        out_specs=pl.BlockSpec((tm, d), lambda i: (i, 0)),
        out_shape=jax.ShapeDtypeStruct((t, d), F32),
        compiler_params=_cparams(("parallel",)),
        name="out_proj",
    )(x2d, ya, yb, yc, wa, wb, wc)


def _peerq_body(x_ref, g_ref, w_ref, q_ref, xn_ref):
    xf = x_ref[...]
    ms = jnp.mean(xf * xf, axis=-1, keepdims=True)
    xn = (xf * lax.rsqrt(ms + EPS) * g_ref[...]).astype(BF16)
    xn_ref[...] = xn
    q_ref[...] = _dot(xn, w_ref[...])


def _peer_q(x2d, g, w, *, tm):
    t, d = x2d.shape
    n = w.shape[1]
    return pl.pallas_call(
        _peerq_body,
        grid=(t // tm,),
        in_specs=[
            pl.BlockSpec((tm, d), lambda i: (i, 0)),
            pl.BlockSpec((1, d), lambda i: (0, 0)),
            pl.BlockSpec((d, n), lambda i: (0, 0), pipeline_mode=pl.Buffered(1)),
        ],
        out_specs=[
            pl.BlockSpec((tm, n), lambda i: (i, 0)),
            pl.BlockSpec((tm, d), lambda i: (i, 0)),
        ],
        out_shape=[
            jax.ShapeDtypeStruct((t, n), F32),
            jax.ShapeDtypeStruct((t, d), BF16),
        ],
        compiler_params=_cparams(("parallel",)),
        name="peer_q",
    )(x2d, g, w)


def _extract_top(s, rowid, n_out):
    nrow = s.shape[0]
    rank = jnp.full(s.shape, float(n_out), F32)
    vals = []
    for it in range(n_out):
        m = jnp.max(s, axis=0, keepdims=True)
        idx = jnp.min(jnp.where(s == m, rowid, float(nrow)), axis=0, keepdims=True)
        hit = rowid == idx
        vals.append(m)
        rank = jnp.where(hit, float(it), rank)
        s = jnp.where(hit, -jnp.inf, s)
    return vals, rank, rank < float(n_out)


def _extract_top_distinct(s, n_out):
    rank = jnp.full(s.shape, float(n_out), F32)
    vals = []
    for it in range(n_out):
        m = jnp.max(s, axis=0, keepdims=True)
        hit = s == m
        vals.append(m)
        rank = jnp.where(hit, float(it), rank)
        s = jnp.where(hit, -jnp.inf, s)
    taken = rank < float(n_out)
    return vals, rank, taken, jnp.sum(jnp.where(taken, 1.0, 0.0), axis=0, keepdims=True)


_CAND_SHORT = 8


def _topk_body(q_ref, keys_ref, c1_ref, w1_ref, r2_ref, e2_ref, *, tt):
    rowid = lax.broadcasted_iota(jnp.int32, (N_KEYS, tt), 0).astype(F32)
    rowid2 = lax.broadcasted_iota(jnp.int32, (P_TOPK * P_TOPK, tt), 0).astype(F32)

    def pair_grid(x1, x2, op):
        parts = [op(x1[0:1, :], x2)]
        parts += [op(x1[a:a + 1, :], x2[0:_CAND_SHORT, :]) for a in range(1, _CAND_SHORT)]
        parts += [op(x1[_CAND_SHORT:, :], x2[0:1, :])]
        return jnp.concatenate(parts, axis=0)

    def finish(h, scores, v1, v2, ranks, counts, zsum):
        cnt = jnp.zeros((N_KEYS, tt), F32)
        for a in range(P_TOPK):
            cnt = jnp.where(ranks[0] == float(a), counts[a], cnt)
        c1_ref[h] = cnt
        w1_ref[h] = jnp.exp(scores[0] - v1[0:1, :]) / zsum
        r2_ref[h] = ranks[1].astype(BF16)
        e2_ref[h] = jnp.exp(scores[1] - v2[0:1, :]).astype(BF16)

    def per_head(h, carry):
        scores = []
        for c in range(2):
            kh, kl = _split_bf16(keys_ref[2 * h + c])
            col = pl.multiple_of((2 * h + c) * P_HALF, P_HALF)
            qh, ql = _split_bf16(q_ref[:, pl.ds(col, P_HALF)])
            scores.append(_dot3(kh, kl, qh, ql, _dot_nt))

        tops, ranks, clean = [], [], None
        for c in range(2):
            vals, rank, _, lost = _extract_top_distinct(scores[c], P_TOPK)
            tops.append(jnp.concatenate(vals, axis=0))
            ranks.append(rank)
            ok = lost == float(P_TOPK)
            clean = ok if clean is None else (clean & ok)
        v1, v2 = tops
        cand = pair_grid(v1, v2, lambda x, y: x + y)
        _, _, picked, lost = _extract_top_distinct(cand, P_TOPK)
        clean = clean & (lost == float(P_TOPK))
        ev1 = jnp.exp(v1 - v1[0:1, :])
        ev2 = jnp.exp(v2 - v2[0:1, :])
        prod = pair_grid(ev1, ev2, lambda x, y: x * y)
        zsum = jnp.sum(jnp.where(picked, prod, 0.0), axis=0, keepdims=True)
        npick = jnp.where(picked, 1.0, 0.0)
        counts = [jnp.sum(npick[0:P_TOPK, :], axis=0, keepdims=True)]
        for a in range(1, _CAND_SHORT):
            r0 = P_TOPK + (a - 1) * _CAND_SHORT
            counts.append(jnp.sum(npick[r0:r0 + _CAND_SHORT, :], axis=0, keepdims=True))
        r0 = P_TOPK + (_CAND_SHORT - 1) * _CAND_SHORT
        counts += [npick[r0 + a:r0 + a + 1, :] for a in range(P_TOPK - _CAND_SHORT)]
        finish(h, scores, v1, v2, ranks, counts, zsum)

        @pl.when(jnp.min(jnp.where(clean, 1.0, 0.0)) < 0.5)
        def _():
            tops, ranks = [], []
            for c in range(2):
                vals, rank, _ = _extract_top(scores[c], rowid, P_TOPK)
                tops.append(jnp.concatenate(vals, axis=0))
                ranks.append(rank)
            v1, v2 = tops
            cand = jnp.concatenate([v1[a:a + 1, :] + v2 for a in range(P_TOPK)], axis=0)
            _, _, picked = _extract_top(cand, rowid2, P_TOPK)
            ev1 = jnp.exp(v1 - v1[0:1, :])
            ev2 = jnp.exp(v2 - v2[0:1, :])
            prod = jnp.concatenate([ev1[a:a + 1, :] * ev2 for a in range(P_TOPK)], axis=0)
            zsum = jnp.sum(jnp.where(picked, prod, 0.0), axis=0, keepdims=True)
            npick = jnp.where(picked, 1.0, 0.0)
            counts = [jnp.sum(npick[a * P_TOPK:(a + 1) * P_TOPK, :], axis=0, keepdims=True) for a in range(P_TOPK)]
            finish(h, scores, v1, v2, ranks, counts, zsum)

        return carry

    lax.fori_loop(0, P_HEADS, per_head, 0)


def _peer_topk(q, keys16, *, tt):
    t = q.shape[0]
    st_shape = jax.ShapeDtypeStruct((P_HEADS, N_KEYS, t), F32)
    st_shape_b = jax.ShapeDtypeStruct((P_HEADS, N_KEYS, t), BF16)
    st_spec = pl.BlockSpec((P_HEADS, N_KEYS, tt), lambda i: (0, 0, i))
    return pl.pallas_call(
        functools.partial(_topk_body, tt=tt),
        grid=(t // tt,),
        in_specs=[
            pl.BlockSpec((tt, q.shape[1]), lambda i: (i, 0)),
            pl.BlockSpec(keys16.shape, lambda i: (0, 0, 0)),
        ],
        out_specs=[st_spec] * 4,
        out_shape=[st_shape, st_shape, st_shape_b, st_shape_b],
        compiler_params=_cparams(("parallel",)),
        name="peer_topk",
    )(q, keys16)


def _peer_dense_body(xn_ref, u_ref, v_ref, c1_ref, w1_ref, r2_ref, e2_ref, x_ref, g_ref, o_ref, c1s_ref, w1s_ref, *hg_refs,
                     tm, te, tc, rb, final):
    e = pl.program_id(1)

    @pl.when(e == 0)
    def _():
        o_ref[...] = x_ref[...]

    ub = u_ref[...].astype(BF16)
    vb = v_ref[...].astype(BF16)

    ng = te // N_KEYS
    nchunk = tm // tc
    which = e % (8 // ng)

    for src, dst in ((c1_ref, c1s_ref), (w1_ref, w1s_ref)):
        rows = src[:, 0:ng, :]
        for g in range(1, 8 // ng):
            rows = jnp.where(which == g, src[:, g * ng:(g + 1) * ng, :], rows)
        dst[:, 0:ng, :] = rows

    def bf16_row(ref, h, ii, t0):
        return ref[h, ii:ii + 1, t0:t0 + tc].astype(BF16)

    def stage_a(ci):
        t0 = ci * tc
        return _dot_nt(ub, xn_ref[t0:t0 + tc, :])

    def stage_b(ci, hts):
        t0 = ci * tc
        for ii in range(ng):
            rows = [(bf16_row(c1s_ref, h, ii, t0), bf16_row(w1s_ref, h, ii, t0)) for h in range(P_HEADS)]
            for j0 in range(0, N_KEYS, rb):
                r0 = ii * N_KEYS + j0
                ht = hts[r0:r0 + rb, :]
                act = ((0.5 * ht) * (1.0 + lax.erf(ht * (2.0 ** -0.5)))).astype(BF16)
                gate = jnp.zeros((rb, tc), BF16)
                for h in range(P_HEADS):
                    cnt, w1 = rows[h]
                    gate = gate + jnp.where(r2_ref[h, j0:j0 + rb, t0:t0 + tc] < cnt,
                                            e2_ref[h, j0:j0 + rb, t0:t0 + tc] * w1, jnp.zeros((), BF16))
                hg_refs[ci][r0:r0 + rb, :] = act * gate

    def stage_c(ci):
        t0 = ci * tc
        o_ref[t0:t0 + tc, :] += _dot_tn(hg_refs[ci][...], vb)

    ahead = 2
    hts = {ci: stage_a(ci) for ci in range(min(ahead, nchunk))}
    for ci in range(nchunk):
        if ci + ahead < nchunk:
            hts[ci + ahead] = stage_a(ci + ahead)
        stage_b(ci, hts.pop(ci))
        stage_c(ci)

    if final:
        @pl.when(e == pl.num_programs(1) - 1)
        def _():
            y = o_ref[...]
            o_ref[...] = y * lax.rsqrt(jnp.mean(y * y, axis=-1, keepdims=True) + EPS) * g_ref[...]


def _peer_dense(xn, u, v, layer, c1, w1, r2, e2, x2d, g, *, tm, te, tc, rb, final):
    t, d = xn.shape
    ne = u.shape[1]
    ng = te // N_KEYS
    once = pl.Buffered(1)
    row_spec = pl.BlockSpec((P_HEADS, 8, tm), lambda i, e: (0, e // (8 // ng), i))
    full_spec = pl.BlockSpec((P_HEADS, N_KEYS, tm), lambda i, e: (0, 0, i), pipeline_mode=once)
    tok_spec = pl.BlockSpec((tm, d), lambda i, e: (i, 0), pipeline_mode=once)
    return pl.pallas_call(
        functools.partial(_peer_dense_body, tm=tm, te=te, tc=tc, rb=rb, final=final),
        grid=(t // tm, ne // te),
        in_specs=[
            tok_spec,
            pl.BlockSpec((None, te, d), lambda i, e: (layer, e, 0)),
            pl.BlockSpec((None, te, d), lambda i, e: (layer, e, 0)),
            row_spec, row_spec, full_spec, full_spec,
            tok_spec,
            pl.BlockSpec((1, d), lambda i, e: (0, 0)),
        ],
        out_specs=pl.BlockSpec((tm, d), lambda i, e: (i, 0), pipeline_mode=once),
        out_shape=jax.ShapeDtypeStruct((t, d), F32),
        scratch_shapes=[pltpu.VMEM((P_HEADS, 8, tm), F32), pltpu.VMEM((P_HEADS, 8, tm), F32)]
        + [pltpu.VMEM((te, tc), BF16)] * (tm // tc),
        compiler_params=_cparams(("parallel", "arbitrary")),
        name="peer_dense",
    )(xn, u, v, c1, w1, r2, e2, x2d, g)


def _pad_heads(w):
    lead = w.shape[:-1]
    w = w.reshape(lead + (B_HEADS, B_HEAD_DIM))
    w = jnp.pad(w, [(0, 0)] * len(lead) + [(0, 0), (0, B_HEAD_PAD - B_HEAD_DIM)])
    return w.reshape(lead + (B_PAD_WIDTH,))


def _regroup_w_in(w):
    sizes = (A_WIDTH, A_WIDTH, A_WIDTH, A_HEADS, 2 * B_WIDTH, B_WIDTH, B_WIDTH, B_HEADS, B_HEADS, C_WIDTH, C_WIDTH)
    parts, p = [], 0
    for sz in sizes:
        parts.append(w[:, p:p + sz])
        p += sz
    aq, ak, av, af, bqk, bv, bo, bi, bf, ca, cg = parts
    d = w.shape[0]
    main = jnp.concatenate(
        [_pad_heads(bqk[:, :B_WIDTH]), _pad_heads(bqk[:, B_WIDTH:]), _pad_heads(bv), _pad_heads(bo),
         ca, cg, aq, ak, av, jnp.zeros((d, Z_WIDTH - Z_USED), w.dtype)], axis=1)
    gates = jnp.concatenate([af, bi, bf, jnp.zeros((d, GATE_LANES - A_HEADS - 2 * B_HEADS), w.dtype)], axis=1)
    return main.astype(BF16), gates


def _tile_sizes(t, seq):
    return dict(
        tm_in=min(1024, t), tn_in=1280,
        tb_gate=min(256, seq),
        tq=min(512, seq),
        ts_conf=min(256, seq),
        tm_out=min(512, t),
        tm_q=min(512, t),
        tt=min(256, t),
        tm_pd=min(1024, t), te=512, tc_pd=min(256, t), rb_pd=32,
    )


def kernel(x, w_in, fox_f_bias, mlstm_conv_w, mlstm_i_bias, mlstm_f_bias, conf_dw_w, conf_dw_b, conf_ln_g, conf_ln_b, conf_pw_w, conf_pw_b, out_norm_g, w_out, norm1_g, norm2_g, peer_wq, peer_keys, peer_u, peer_v, final_g):
    batch, seq, d = x.shape
    t = batch * seq
    depth = w_in.shape[0]
    ts = _tile_sizes(t, seq)
    xf = x.reshape(t, d)
    row = lambda v: v.reshape(1, -1)

    for l in range(depth):
        w_main, w_gate = _regroup_w_in(w_in[l])
        wgh, wgl = _split_bf16(w_gate)
        z, zg = _in_proj(xf, row(norm1_g[l]), w_main, wgh, wgl, tm=ts["tm_in"], tn=ts["tn_in"])
        bias = jnp.concatenate([fox_f_bias[l], mlstm_i_bias[l], mlstm_f_bias[l],
                                jnp.zeros((GATE_LANES - A_HEADS - 2 * B_HEADS,), F32)]).reshape(1, GATE_LANES)
        gcol, grow = _gates(zg.reshape(batch, seq, GATE_LANES), bias, tb=ts["tb_gate"])
        g_a = row(out_norm_g[l, :A_WIDTH])
        g_b = row(_pad_heads(out_norm_g[l, A_WIDTH:A_WIDTH + B_WIDTH]))
        g_c = row(out_norm_g[l, A_WIDTH + B_WIDTH:])
        z3 = z.reshape(batch, seq, Z_WIDTH)
        ya = _attention(z, gcol, grow.reshape(batch, 16, 1, seq), g_a, batch=batch, seq=seq, tq=ts["tq"])
        cw = jnp.concatenate([_pad_heads(mlstm_conv_w[l, :, :B_WIDTH]), _pad_heads(mlstm_conv_w[l, :, B_WIDTH:])], axis=1)
        yb = _mlstm(z3, gcol, grow, cw, g_b)
        yc = _conformer(z3, conf_dw_w[l], row(conf_dw_b[l]), row(conf_ln_g[l]), row(conf_ln_b[l]),
                        conf_pw_w[l].astype(BF16), row(conf_pw_b[l]), g_c, ts=ts["ts_conf"])
        wo = w_out[l]
        wa = wo[:A_WIDTH].astype(BF16)
        wb = jnp.pad(wo[A_WIDTH:A_WIDTH + B_WIDTH].reshape(B_HEADS, B_HEAD_DIM, d),
                     ((0, 0), (0, B_HEAD_PAD - B_HEAD_DIM), (0, 0))).reshape(B_PAD_WIDTH, d).astype(BF16)
        wc = wo[A_WIDTH + B_WIDTH:].astype(BF16)
        xf = _out_proj(xf, ya, yb.reshape(t, B_PAD_WIDTH), yc.reshape(t, C_WIDTH), wa, wb, wc,
                       tm=ts["tm_out"])

        q, xn = _peer_q(xf, row(norm2_g[l]), peer_wq[l].astype(BF16), tm=ts["tm_q"])
        keys16 = peer_keys[l].reshape(2 * P_HEADS, N_KEYS, P_HALF)
        c1, w1, r2, e2 = _peer_topk(q, keys16, tt=ts["tt"])
        xf = _peer_dense(xn, peer_u, peer_v, l, c1, w1, r2, e2, xf, row(final_g),
                         tm=ts["tm_pd"], te=ts["te"], tc=ts["tc_pd"], rb=ts["rb_pd"], final=(l == depth - 1))

    return xf.reshape(batch, seq, d)
```

```python
import functools

import jax
import jax.numpy as jnp
from jax import lax
from jax.experimental import pallas as pl
from jax.experimental.pallas import tpu as pltpu

F32 = jnp.float32
BF16 = jnp.bfloat16

EPS = 1e-6
LN_EPS = 1e-5

A_HEADS = 6
A_HEAD_DIM = 128
A_WIDTH = A_HEADS * A_HEAD_DIM
B_HEADS = 4
B_HEAD_DIM = 192
B_HEAD_PAD = 256
B_WIDTH = B_HEADS * B_HEAD_DIM
B_PAD_WIDTH = B_HEADS * B_HEAD_PAD
B_CHUNK = 64
B_CONV = 4
C_GROUPS = 4
C_WIDTH = 512
C_CONV = 31
P_HEADS = 8
P_HALF = 128
N_KEYS = 128
P_TOPK = 16
GATE_LANES = 128

COL_BQ = 0
COL_BK = B_PAD_WIDTH
COL_BV = 2 * B_PAD_WIDTH
COL_BO = 3 * B_PAD_WIDTH
COL_CA = 4 * B_PAD_WIDTH
COL_CG = COL_CA + C_WIDTH
COL_AQ = COL_CG + C_WIDTH
COL_AK = COL_AQ + A_WIDTH
COL_AV = COL_AK + A_WIDTH
Z_USED = COL_AV + A_WIDTH
Z_WIDTH = 7680
G_AF = 0
G_BI = A_HEADS
G_BF = A_HEADS + B_HEADS

VMEM_LIMIT = 60 * 1024 * 1024


def _cparams(sem):
    return pltpu.CompilerParams(dimension_semantics=sem, vmem_limit_bytes=VMEM_LIMIT)


def _split_bf16(a):
    hi = a.astype(BF16)
    lo = (a - hi.astype(F32)).astype(BF16)
    return hi, lo


def _dot(a, b):
    return jnp.dot(a, b, preferred_element_type=F32)


def _dot_nt(a, b):
    return lax.dot_general(a, b, (((1,), (1,)), ((), ())), preferred_element_type=F32)


def _dot_tn(a, b):
    return lax.dot_general(a, b, (((0,), (0,)), ((), ())), preferred_element_type=F32)


def _dot3(ah, al, bh, bl, dot):
    return dot(ah, bh) + dot(ah, bl) + dot(al, bh)


def _sigmoid(x):
    return 1.0 / (1.0 + jnp.exp(-x))


def _inproj_body(x_ref, g_ref, w_ref, wgh_ref, wgl_ref, z_ref, zg_ref, xn_ref):
    @pl.when(pl.program_id(1) == 0)
    def _():
        xf = x_ref[...]
        ms = jnp.mean(xf * xf, axis=-1, keepdims=True)
        xn = xf * lax.rsqrt(ms + EPS) * g_ref[...]
        hi, lo = _split_bf16(xn)
        xn_ref[...] = hi
        zg_ref[...] = _dot3(hi, lo, wgh_ref[...], wgl_ref[...], _dot)

    z_ref[...] = _dot(xn_ref[...], w_ref[...])


def _in_proj(x2d, g, w, wgh, wgl, *, tm, tn):
    t, d = x2d.shape
    n = w.shape[1]
    return pl.pallas_call(
        _inproj_body,
        grid=(t // tm, n // tn),
        in_specs=[
            pl.BlockSpec((tm, d), lambda i, j: (i, 0), pipeline_mode=pl.Buffered(1)),
            pl.BlockSpec((1, d), lambda i, j: (0, 0)),
            pl.BlockSpec((d, tn), lambda i, j: (0, j)),
            pl.BlockSpec((d, GATE_LANES), lambda i, j: (0, 0)),
            pl.BlockSpec((d, GATE_LANES), lambda i, j: (0, 0)),
        ],
        out_specs=[
            pl.BlockSpec((tm, tn), lambda i, j: (i, j)),
            pl.BlockSpec((tm, GATE_LANES), lambda i, j: (i, 0)),
        ],
        out_shape=[
            jax.ShapeDtypeStruct((t, n), F32),
            jax.ShapeDtypeStruct((t, GATE_LANES), F32),
        ],
        scratch_shapes=[pltpu.VMEM((tm, d), BF16)],
        compiler_params=_cparams(("parallel", "arbitrary")),
        name="in_proj",
    )(x2d, g, w, wgh, wgl)


def _gates_body(zg_ref, bias_ref, col_ref, row_ref, carry_ref, *, tb):
    @pl.when(pl.program_id(1) == 0)
    def _():
        carry_ref[...] = jnp.zeros_like(carry_ref)

    v = zg_ref[0] + bias_ref[...]
    ls = jnp.minimum(v, 0.0) - jnp.log1p(jnp.exp(-jnp.abs(v)))
    r = lax.broadcasted_iota(jnp.int32, (tb, tb), 0)
    c = lax.broadcasted_iota(jnp.int32, (tb, tb), 1)
    tri = (c <= r).astype(F32)
    tri_chunk = jnp.where((r // B_CHUNK) == (c // B_CHUNK), tri, 0.0)
    csum = jnp.dot(tri, ls, precision=lax.Precision.HIGHEST, preferred_element_type=F32) + carry_ref[...]
    bcum = jnp.dot(tri_chunk, ls, precision=lax.Precision.HIGHEST, preferred_element_type=F32)
    carry_ref[...] = csum[tb - 1:tb, :]
    lane = lax.broadcasted_iota(jnp.int32, (tb, GATE_LANES), 1)
    out = jnp.where(lane < G_BI, csum, jnp.where(lane < G_BF, v, bcum))
    col_ref[0] = out
    row_ref[0] = out.T[0:16, :]


def _gates(zg3, bias, *, tb):
    b, s, _ = zg3.shape
    return pl.pallas_call(
        functools.partial(_gates_body, tb=tb),
        grid=(b, s // tb),
        in_specs=[
            pl.BlockSpec((1, tb, GATE_LANES), lambda i, j: (i, j, 0)),
            pl.BlockSpec((1, GATE_LANES), lambda i, j: (0, 0)),
        ],
        out_specs=[
            pl.BlockSpec((1, tb, GATE_LANES), lambda i, j: (i, j, 0)),
            pl.BlockSpec((1, 16, tb), lambda i, j: (i, 0, j)),
        ],
        out_shape=[
            jax.ShapeDtypeStruct((b, s, GATE_LANES), F32),
            jax.ShapeDtypeStruct((b, 16, s), F32),
        ],
        scratch_shapes=[pltpu.VMEM((1, GATE_LANES), F32)],
        compiler_params=_cparams(("parallel", "arbitrary")),
        name="gates",
    )(zg3, bias)


def _attn_body(q_ref, k_ref, v_ref, ccol_ref, crow_ref, g_ref, o_ref, kb_ref, vb_ref, m_ref, l_ref, acc_ref, *, tq):
    h = pl.program_id(1)
    qi = pl.program_id(2)
    log2e = 1.4426950408889634
    scale2 = (A_HEAD_DIM ** -0.5) * log2e

    @pl.when(qi == 0)
    def _():
        kb_ref[...] = k_ref[...].astype(BF16)
        vb_ref[...] = v_ref[...].astype(BF16)

    q = q_ref[...].astype(BF16)
    lane = lax.broadcasted_iota(jnp.int32, (tq, GATE_LANES), 1)
    ct2 = jnp.sum(jnp.where(lane == h, ccol_ref[0], 0.0), axis=-1, keepdims=True) * log2e
    m_ref[...] = jnp.full(m_ref.shape, -jnp.inf, F32)
    l_ref[...] = jnp.zeros_like(l_ref)
    acc_ref[...] = jnp.zeros_like(acc_ref)

    def logits(j):
        k0 = pl.multiple_of(j * tq, tq)
        cs2 = crow_ref[0, 0, :, pl.ds(k0, tq)] * log2e
        return _dot_nt(q, kb_ref[pl.ds(k0, tq), :]) * scale2 + (ct2 - cs2)

    def softmax(s):
        m_prev = m_ref[...]
        m_new = jnp.maximum(m_prev, jnp.max(s, axis=-1, keepdims=True))
        p = jnp.exp2(s - m_new)
        alpha = jnp.exp2(m_prev - m_new)
        l_ref[...] = alpha * l_ref[...] + jnp.sum(p, axis=-1, keepdims=True)
        m_ref[...] = m_new
        return p.astype(BF16), alpha

    def values(j, p, alpha):
        k0 = pl.multiple_of(j * tq, tq)
        acc_ref[...] = alpha * acc_ref[...] + _dot(p, vb_ref[pl.ds(k0, tq), :])

    def body(j, s_cur):
        s_next = logits(j + 1)
        p, alpha = softmax(s_cur)
        values(j, p, alpha)
        return s_next

    s_diag = lax.fori_loop(0, qi, body, logits(0))
    row = lax.broadcasted_iota(jnp.int32, (tq, tq), 0)
    col = lax.broadcasted_iota(jnp.int32, (tq, tq), 1)
    p, alpha = softmax(jnp.where(col <= row, s_diag, -jnp.inf))
    values(qi, p, alpha)
    o = acc_ref[...] / l_ref[...]
    o = o * lax.rsqrt(jnp.mean(o * o, axis=-1, keepdims=True) + EPS) * g_ref[...]
    o_ref[...] = o.astype(BF16)


def _attention(z2d, gcol, grow, g_a, *, batch, seq, tq):
    nq = seq // tq
    qc, kc, vc = COL_AQ // 128, COL_AK // 128, COL_AV // 128
    return pl.pallas_call(
        functools.partial(_attn_body, tq=tq),
        grid=(batch, A_HEADS, nq),
        in_specs=[
            pl.BlockSpec((tq, 128), lambda b, h, i: (b * nq + i, qc + h)),
            pl.BlockSpec((seq, 128), lambda b, h, i: (b, kc + h)),
            pl.BlockSpec((seq, 128), lambda b, h, i: (b, vc + h)),
            pl.BlockSpec((1, tq, GATE_LANES), lambda b, h, i: (b, i, 0)),
            pl.BlockSpec((1, 1, 1, seq), lambda b, h, i: (b, h, 0, 0)),
            pl.BlockSpec((1, 128), lambda b, h, i: (0, h)),
        ],
        out_specs=pl.BlockSpec((tq, 128), lambda b, h, i: (b * nq + i, h)),
        out_shape=jax.ShapeDtypeStruct((batch * seq, A_WIDTH), BF16),
        scratch_shapes=[pltpu.VMEM((seq, 128), BF16), pltpu.VMEM((seq, 128), BF16),
                        pltpu.VMEM((tq, 1), F32), pltpu.VMEM((tq, 1), F32), pltpu.VMEM((tq, 128), F32)],
        compiler_params=_cparams(("parallel", "parallel", "arbitrary")),
        name="fox_attention",
    )(z2d, z2d, z2d, gcol, grow, g_a)


def _mlstm_body(zb_ref, gcol_ref, grow_ref, cw_ref, g_ref, y_ref, ext_ref, c_ref, m_ref, *, nb):
    k = pl.program_id(0)
    ln = B_CHUNK
    hp = B_HEAD_PAD

    @pl.when(k == 0)
    def _():
        ext_ref[:, 0:8, :] = jnp.zeros((nb, 8, 2 * B_PAD_WIDTH), F32)
        c_ref[...] = jnp.zeros_like(c_ref)
        m_ref[...] = jnp.zeros_like(m_ref)

    first_half = (k % 2) == 0
    r = lax.broadcasted_iota(jnp.int32, (ln, ln), 0)
    c = lax.broadcasted_iota(jnp.int32, (ln, ln), 1)
    causal = c <= r
    lane = lax.broadcasted_iota(jnp.int32, (ln, hp), 1)
    qscale = B_HEAD_DIM ** -0.5

    for b in range(nb):
        ext_ref[b, 8:8 + ln, :] = zb_ref[b, :, 0:2 * B_PAD_WIDTH]
        conv = cw_ref[0:1, :] * ext_ref[b, 5:5 + ln, :]
        for j in range(1, B_CONV):
            conv = conv + cw_ref[j:j + 1, :] * ext_ref[b, 5 + j:5 + j + ln, :]
        ext_ref[b, 0:8, :] = ext_ref[b, ln:ln + 8, :]
        qk = conv * _sigmoid(conv)
        grow = grow_ref[b]
        grow = jnp.where(first_half, grow[:, 0:ln], grow[:, ln:2 * ln])
        gcol = gcol_ref[b]
        for h in range(B_HEADS):
            st = b * B_HEADS + h
            qq = (qk[:, COL_BQ + h * hp:COL_BQ + (h + 1) * hp] * qscale).astype(BF16)
            kk = qk[:, COL_BK + h * hp:COL_BK + (h + 1) * hp]
            vv = zb_ref[b, :, COL_BV + h * hp:COL_BV + (h + 1) * hp]
            vv = jnp.where(lane == B_HEAD_DIM, 1.0, vv).astype(BF16)
            og = zb_ref[b, :, COL_BO + h * hp:COL_BO + (h + 1) * hp]
            bc_col = gcol[:, G_BF + h:G_BF + h + 1]
            li_col = gcol[:, G_BI + h:G_BI + h + 1]
            bc_row = grow[G_BF + h:G_BF + h + 1, :]
            li_row = grow[G_BI + h:G_BI + h + 1, :]
            m = m_ref[st, 0:1, 0:1]

            d = jnp.where(causal, bc_col - bc_row + li_row, -jnp.inf)
            inter = bc_col + m
            m_t = jnp.maximum(inter, jnp.max(d, axis=-1, keepdims=True))
            w_inter = jnp.exp(inter - m_t)
            sc = _dot_nt(qq, kk.astype(BF16)) * jnp.exp(d - m_t)
            cm = c_ref[st]
            num = w_inter * _dot(qq, cm.astype(BF16)) + _dot(sc.astype(BF16), vv)
            den = num[:, B_HEAD_DIM:B_HEAD_DIM + 1]
            hh = num / jnp.maximum(jnp.abs(den), jnp.exp(-m_t))

            b_last = bc_col[ln - 1:ln, :]
            dec = b_last - bc_col + li_col
            m_new = jnp.maximum(b_last + m, jnp.max(dec, axis=0, keepdims=True))
            a = jnp.exp(b_last + m - m_new)
            wk = (jnp.exp(dec - m_new) * kk).astype(BF16)
            c_ref[st] = a * cm + _dot_tn(wk, vv)
            m_ref[st] = jnp.broadcast_to(m_new, (8, 128))

            y = jnp.where(lane < B_HEAD_DIM, _sigmoid(og) * hh, 0.0)
            ms = jnp.sum(y * y, axis=-1, keepdims=True) * (1.0 / B_HEAD_DIM)
            y = y * lax.rsqrt(ms + EPS) * g_ref[:, h * hp:(h + 1) * hp]
            y_ref[b, :, h * hp:(h + 1) * hp] = y.astype(BF16)


def _mlstm(z3, gcol, grow, cw, g_b):
    nb, seq, _ = z3.shape
    nc = seq // B_CHUNK
    return pl.pallas_call(
        functools.partial(_mlstm_body, nb=nb),
        grid=(nc,),
        in_specs=[
            pl.BlockSpec((nb, B_CHUNK, 4 * B_PAD_WIDTH), lambda k: (0, k, 0)),
            pl.BlockSpec((nb, B_CHUNK, GATE_LANES), lambda k: (0, k, 0)),
            pl.BlockSpec((nb, 16, 2 * B_CHUNK), lambda k: (0, 0, k // 2)),
            pl.BlockSpec((B_CONV, 2 * B_PAD_WIDTH), lambda k: (0, 0)),
            pl.BlockSpec((1, B_PAD_WIDTH), lambda k: (0, 0)),
        ],
        out_specs=pl.BlockSpec((nb, B_CHUNK, B_PAD_WIDTH), lambda k: (0, k, 0)),
        out_shape=jax.ShapeDtypeStruct((nb, seq, B_PAD_WIDTH), BF16),
        scratch_shapes=[
            pltpu.VMEM((nb, B_CHUNK + 8, 2 * B_PAD_WIDTH), F32),
            pltpu.VMEM((nb * B_HEADS, B_HEAD_PAD, B_HEAD_PAD), F32),
            pltpu.VMEM((nb * B_HEADS, 8, 128), F32),
        ],
        compiler_params=_cparams(("arbitrary",)),
        name="mlstm",
    )(z3, gcol, grow, cw, g_b)


def _conf_body(ca_ref, cg_ref, dw_ref, db_ref, lg_ref, lb_ref, pw_ref, pb_ref, g_ref, y_ref, ext_ref, *, ts):
    halo = 32

    @pl.when(pl.program_id(1) == 0)
    def _():
        ext_ref[0:halo, :] = jnp.zeros((halo, C_WIDTH), F32)

    ext_ref[halo:halo + ts, :] = ca_ref[0] * _sigmoid(cg_ref[0])
    off = halo - (C_CONV - 1)
    acc = None
    for r in range(8):
        taps = [(m, 8 * m + r - off) for m in range(halo // 8 + 1) if 0 <= 8 * m + r - off < C_CONV]
        ln = ts if r == 0 else ts + 8
        part = None
        for m, j in taps:
            term = dw_ref[j:j + 1, :] * ext_ref[8 * m:8 * m + ln, :]
            part = term if part is None else part + term
        part = part if r == 0 else part[r:r + ts, :]
        acc = part if acc is None else acc + part
    ext_ref[0:halo, :] = ext_ref[ts:ts + halo, :]
    y = acc + db_ref[...]
    mu = jnp.mean(y, axis=-1, keepdims=True)
    yc = y - mu
    y = yc * lax.rsqrt(jnp.mean(yc * yc, axis=-1, keepdims=True) + LN_EPS)
    y = y * lg_ref[...] + lb_ref[...]
    y = y * _sigmoid(y)
    y = _dot(y.astype(BF16), pw_ref[...]) + pb_ref[...]
    gw = C_WIDTH // C_GROUPS
    for gi in range(C_GROUPS):
        yg = y[:, gi * gw:(gi + 1) * gw]
        yg = yg * lax.rsqrt(jnp.mean(yg * yg, axis=-1, keepdims=True) + EPS) * g_ref[:, gi * gw:(gi + 1) * gw]
        y_ref[0, :, gi * gw:(gi + 1) * gw] = yg.astype(BF16)


def _conformer(z3, dw, db, lg, lb, pw, pb, g_c, *, ts):
    nb, seq, _ = z3.shape
    ca, cg = COL_CA // C_WIDTH, COL_CG // C_WIDTH
    vec = pl.BlockSpec((1, C_WIDTH), lambda b, i: (0, 0))
    return pl.pallas_call(
        functools.partial(_conf_body, ts=ts),
        grid=(nb, seq // ts),
        in_specs=[
            pl.BlockSpec((1, ts, C_WIDTH), lambda b, i: (b, i, ca)),
            pl.BlockSpec((1, ts, C_WIDTH), lambda b, i: (b, i, cg)),
            pl.BlockSpec((C_CONV, C_WIDTH), lambda b, i: (0, 0)),
            vec, vec, vec,
            pl.BlockSpec((C_WIDTH, C_WIDTH), lambda b, i: (0, 0)),
            vec, vec,
        ],
        out_specs=pl.BlockSpec((1, ts, C_WIDTH), lambda b, i: (b, i, 0)),
        out_shape=jax.ShapeDtypeStruct((nb, seq, C_WIDTH), BF16),
        scratch_shapes=[pltpu.VMEM((ts + 32, C_WIDTH), F32)],
        compiler_params=_cparams(("parallel", "arbitrary")),
        name="conformer",
    )(z3, z3, dw, db, lg, lb, pw, pb, g_c)


def _outproj_body(x_ref, ya_ref, yb_ref, yc_ref, wa_ref, wb_ref, wc_ref, o_ref):
    o_ref[...] = (x_ref[...] + _dot(ya_ref[...], wa_ref[...]) + _dot(yb_ref[...], wb_ref[...])
                  + _dot(yc_ref[...], wc_ref[...]))


def _out_proj(x2d, ya, yb, yc, wa, wb, wc, *, tm):
    t, d = x2d.shape
    once = pl.Buffered(1)
    return pl.pallas_call(
        _outproj_body,
        grid=(t // tm,),
        in_specs=[
            pl.BlockSpec((tm, d), lambda i: (i, 0)),
            pl.BlockSpec((tm, ya.shape[1]), lambda i: (i, 0)),
            pl.BlockSpec((tm, yb.shape[1]), lambda i: (i, 0)),
            pl.BlockSpec((tm, yc.shape[1]), lambda i: (i, 0)),
            pl.BlockSpec(wa.shape, lambda i: (0, 0), pipeline_mode=once),
            pl.BlockSpec(wb.shape, lambda i: (0, 0), pipeline_mode=once),
            pl.BlockSpec(wc.shape, lambda i: (0, 0), pipeline_mode=once),
        ],
        out_specs=pl.BlockSpec((tm, d), lambda i: (i, 0)),
        out_shape=jax.ShapeDtypeStruct((t, d), F32),
        compiler_params=_cparams(("parallel",)),
        name="out_proj",
    )(x2d, ya, yb, yc, wa, wb, wc)


def _peerq_body(x_ref, g_ref, w_ref, q_ref, xn_ref):
    xf = x_ref[...]
    ms = jnp.mean(xf * xf, axis=-1, keepdims=True)
    xn = (xf * lax.rsqrt(ms + EPS) * g_ref[...]).astype(BF16)
    xn_ref[...] = xn
    q_ref[...] = _dot(xn, w_ref[...])


def _peer_q(x2d, g, w, *, tm):
    t, d = x2d.shape
    n = w.shape[1]
    return pl.pallas_call(
        _peerq_body,
        grid=(t // tm,),
        in_specs=[
            pl.BlockSpec((tm, d), lambda i: (i, 0)),
            pl.BlockSpec((1, d), lambda i: (0, 0)),
            pl.BlockSpec((d, n), lambda i: (0, 0), pipeline_mode=pl.Buffered(1)),
        ],
        out_specs=[
            pl.BlockSpec((tm, n), lambda i: (i, 0)),
            pl.BlockSpec((tm, d), lambda i: (i, 0)),
        ],
        out_shape=[
            jax.ShapeDtypeStruct((t, n), F32),
            jax.ShapeDtypeStruct((t, d), BF16),
        ],
        compiler_params=_cparams(("parallel",)),
        name="peer_q",
    )(x2d, g, w)


def _extract_top(s, rowid, n_out):
    nrow = s.shape[0]
    rank = jnp.full(s.shape, float(n_out), F32)
    vals = []
    for it in range(n_out):
        m = jnp.max(s, axis=0, keepdims=True)
        idx = jnp.min(jnp.where(s == m, rowid, float(nrow)), axis=0, keepdims=True)
        hit = rowid == idx
        vals.append(m)
        rank = jnp.where(hit, float(it), rank)
        s = jnp.where(hit, -jnp.inf, s)
    return vals, rank, rank < float(n_out)


def _extract_top_distinct(s, n_out, want_rank):
    rank = jnp.full(s.shape, float(n_out), F32) if want_rank else None
    vals = []
    for it in range(n_out):
        m = jnp.max(s, axis=0, keepdims=True)
        hit = s == m
        vals.append(m)
        if want_rank:
            rank = jnp.where(hit, float(it), rank)
        s = jnp.where(hit, -jnp.inf, s)
    taken = s == -jnp.inf
    return vals, rank, taken, jnp.sum(jnp.where(taken, 1.0, 0.0), axis=0, keepdims=True)


_CAND_SHORT = 8


def _topk_body(q_ref, keys_ref, c1_ref, w1_ref, r2_ref, e2_ref, *, tt):
    rowid = lax.broadcasted_iota(jnp.int32, (N_KEYS, tt), 0).astype(F32)
    rowid2 = lax.broadcasted_iota(jnp.int32, (P_TOPK * P_TOPK, tt), 0).astype(F32)

    def pair_grid(x1, x2, op):
        parts = [op(x1[0:1, :], x2)]
        parts += [op(x1[a:a + 1, :], x2[0:_CAND_SHORT, :]) for a in range(1, _CAND_SHORT)]
        parts += [op(x1[_CAND_SHORT:, :], x2[0:1, :])]
        return jnp.concatenate(parts, axis=0)

    def finish(h, scores, v1, v2, is_rank, rank2, counts, zsum):
        cnt = jnp.zeros((N_KEYS, tt), F32)
        for a in range(P_TOPK):
            cnt = jnp.where(is_rank(a), counts[a], cnt)
        c1_ref[h] = cnt
        w1_ref[h] = jnp.exp(scores[0] - v1[0:1, :]) / zsum
        r2_ref[h] = rank2.astype(BF16)
        e2_ref[h] = jnp.exp(scores[1] - v2[0:1, :]).astype(BF16)

    def per_head(h, carry):
        scores = []
        for c in range(2):
            kh, kl = _split_bf16(keys_ref[2 * h + c])
            col = pl.multiple_of((2 * h + c) * P_HALF, P_HALF)
            qh, ql = _split_bf16(q_ref[:, pl.ds(col, P_HALF)])
            scores.append(_dot3(kh, kl, qh, ql, _dot_nt))

        tops, ranks, clean = [], [], None
        for c in range(2):
            vals, rank, _, lost = _extract_top_distinct(scores[c], P_TOPK, want_rank=(c == 1))
            tops.append(jnp.concatenate(vals, axis=0))
            ranks.append(rank)
            ok = lost == float(P_TOPK)
            clean = ok if clean is None else (clean & ok)
        v1, v2 = tops
        cand = pair_grid(v1, v2, lambda x, y: x + y)
        _, _, picked, lost = _extract_top_distinct(cand, P_TOPK, want_rank=False)
        clean = clean & (lost == float(P_TOPK))
        ev1 = jnp.exp(v1 - v1[0:1, :])
        ev2 = jnp.exp(v2 - v2[0:1, :])
        prod = pair_grid(ev1, ev2, lambda x, y: x * y)
        zsum = jnp.sum(jnp.where(picked, prod, 0.0), axis=0, keepdims=True)
        npick = jnp.where(picked, 1.0, 0.0)
        counts = [jnp.sum(npick[0:P_TOPK, :], axis=0, keepdims=True)]
        for a in range(1, _CAND_SHORT):
            r0 = P_TOPK + (a - 1) * _CAND_SHORT
            counts.append(jnp.sum(npick[r0:r0 + _CAND_SHORT, :], axis=0, keepdims=True))
        r0 = P_TOPK + (_CAND_SHORT - 1) * _CAND_SHORT
        counts += [npick[r0 + a:r0 + a + 1, :] for a in range(P_TOPK - _CAND_SHORT)]
        finish(h, scores, v1, v2, lambda a: scores[0] == v1[a:a + 1, :], ranks[1], counts, zsum)

        @pl.when(jnp.min(jnp.where(clean, 1.0, 0.0)) < 0.5)
        def _():
            tops, ranks = [], []
            for c in range(2):
                vals, rank, _ = _extract_top(scores[c], rowid, P_TOPK)
                tops.append(jnp.concatenate(vals, axis=0))
                ranks.append(rank)
            v1, v2 = tops
            cand = jnp.concatenate([v1[a:a + 1, :] + v2 for a in range(P_TOPK)], axis=0)
            _, _, picked = _extract_top(cand, rowid2, P_TOPK)
            ev1 = jnp.exp(v1 - v1[0:1, :])
            ev2 = jnp.exp(v2 - v2[0:1, :])
            prod = jnp.concatenate([ev1[a:a + 1, :] * ev2 for a in range(P_TOPK)], axis=0)
            zsum = jnp.sum(jnp.where(picked, prod, 0.0), axis=0, keepdims=True)
            npick = jnp.where(picked, 1.0, 0.0)
            counts = [jnp.sum(npick[a * P_TOPK:(a + 1) * P_TOPK, :], axis=0, keepdims=True) for a in range(P_TOPK)]
            finish(h, scores, v1, v2, lambda a: ranks[0] == float(a), ranks[1], counts, zsum)

        return carry

    lax.fori_loop(0, P_HEADS, per_head, 0)


def _peer_topk(q, keys16, *, tt):
    t = q.shape[0]
    st_shape = jax.ShapeDtypeStruct((P_HEADS, N_KEYS, t), F32)
    st_shape_b = jax.ShapeDtypeStruct((P_HEADS, N_KEYS, t), BF16)
    st_spec = pl.BlockSpec((P_HEADS, N_KEYS, tt), lambda i: (0, 0, i))
    return pl.pallas_call(
        functools.partial(_topk_body, tt=tt),
        grid=(t // tt,),
        in_specs=[
            pl.BlockSpec((tt, q.shape[1]), lambda i: (i, 0)),
            pl.BlockSpec(keys16.shape, lambda i: (0, 0, 0)),
        ],
        out_specs=[st_spec] * 4,
        out_shape=[st_shape, st_shape, st_shape_b, st_shape_b],
        compiler_params=_cparams(("parallel",)),
        name="peer_topk",
    )(q, keys16)


def _peer_dense_body(xn_ref, u_ref, v_ref, c1_ref, w1_ref, r2_ref, e2_ref, x_ref, g_ref, o_ref, c1s_ref, w1s_ref, *hg_refs,
                     tm, te, tc, rb, final):
    e = pl.program_id(1)

    @pl.when(e == 0)
    def _():
        o_ref[...] = x_ref[...]

    ub = u_ref[...].astype(BF16)
    vb = v_ref[...].astype(BF16)

    ng = te // N_KEYS
    nchunk = tm // tc
    which = e % (8 // ng)

    for src, dst in ((c1_ref, c1s_ref), (w1_ref, w1s_ref)):
        rows = src[:, 0:ng, :]
        for g in range(1, 8 // ng):
            rows = jnp.where(which == g, src[:, g * ng:(g + 1) * ng, :], rows)
        dst[:, 0:ng, :] = rows

    def bf16_row(ref, h, ii, t0):
        return ref[h, ii:ii + 1, t0:t0 + tc].astype(BF16)

    def stage_a(ci):
        t0 = ci * tc
        return _dot_nt(ub, xn_ref[t0:t0 + tc, :])

    def stage_b(ci, hts):
        t0 = ci * tc
        for ii in range(ng):
            rows = [(bf16_row(c1s_ref, h, ii, t0), bf16_row(w1s_ref, h, ii, t0)) for h in range(P_HEADS)]
            for j0 in range(0, N_KEYS, rb):
                r0 = ii * N_KEYS + j0
                ht = hts[r0:r0 + rb, :]
                act = ((0.5 * ht) * (1.0 + lax.erf(ht * (2.0 ** -0.5)))).astype(BF16)
                gate = jnp.zeros((rb, tc), BF16)
                for h in range(P_HEADS):
                    cnt, w1 = rows[h]
                    gate = gate + jnp.where(r2_ref[h, j0:j0 + rb, t0:t0 + tc] < cnt,
                                            e2_ref[h, j0:j0 + rb, t0:t0 + tc] * w1, jnp.zeros((), BF16))
                hg_refs[ci][r0:r0 + rb, :] = act * gate

    def stage_c(ci):
        t0 = ci * tc
        o_ref[t0:t0 + tc, :] += _dot_tn(hg_refs[ci][...], vb)

    ahead = 2
    hts = {ci: stage_a(ci) for ci in range(min(ahead, nchunk))}
    for ci in range(nchunk):
        if ci + ahead < nchunk:
            hts[ci + ahead] = stage_a(ci + ahead)
        stage_b(ci, hts.pop(ci))
        stage_c(ci)

    if final:
        @pl.when(e == pl.num_programs(1) - 1)
        def _():
            y = o_ref[...]
            o_ref[...] = y * lax.rsqrt(jnp.mean(y * y, axis=-1, keepdims=True) + EPS) * g_ref[...]


def _peer_dense(xn, u, v, layer, c1, w1, r2, e2, x2d, g, *, tm, te, tc, rb, final):
    t, d = xn.shape
    ne = u.shape[1]
    ng = te // N_KEYS
    once = pl.Buffered(1)
    row_spec = pl.BlockSpec((P_HEADS, 8, tm), lambda i, e: (0, e // (8 // ng), i))
    full_spec = pl.BlockSpec((P_HEADS, N_KEYS, tm), lambda i, e: (0, 0, i), pipeline_mode=once)
    tok_spec = pl.BlockSpec((tm, d), lambda i, e: (i, 0), pipeline_mode=once)
    return pl.pallas_call(
        functools.partial(_peer_dense_body, tm=tm, te=te, tc=tc, rb=rb, final=final),
        grid=(t // tm, ne // te),
        in_specs=[
            tok_spec,
            pl.BlockSpec((None, te, d), lambda i, e: (layer, e, 0)),
            pl.BlockSpec((None, te, d), lambda i, e: (layer, e, 0)),
            row_spec, row_spec, full_spec, full_spec,
            tok_spec,
            pl.BlockSpec((1, d), lambda i, e: (0, 0)),
        ],
        out_specs=pl.BlockSpec((tm, d), lambda i, e: (i, 0), pipeline_mode=once),
        out_shape=jax.ShapeDtypeStruct((t, d), F32),
        scratch_shapes=[pltpu.VMEM((P_HEADS, 8, tm), F32), pltpu.VMEM((P_HEADS, 8, tm), F32)]
        + [pltpu.VMEM((te, tc), BF16)] * (tm // tc),
        compiler_params=_cparams(("parallel", "arbitrary")),
        name="peer_dense",
    )(xn, u, v, c1, w1, r2, e2, x2d, g)


def _pad_heads(w):
    lead = w.shape[:-1]
    w = w.reshape(lead + (B_HEADS, B_HEAD_DIM))
    w = jnp.pad(w, [(0, 0)] * len(lead) + [(0, 0), (0, B_HEAD_PAD - B_HEAD_DIM)])
    return w.reshape(lead + (B_PAD_WIDTH,))


def _regroup_w_in(w):
    sizes = (A_WIDTH, A_WIDTH, A_WIDTH, A_HEADS, 2 * B_WIDTH, B_WIDTH, B_WIDTH, B_HEADS, B_HEADS, C_WIDTH, C_WIDTH)
    parts, p = [], 0
    for sz in sizes:
        parts.append(w[:, p:p + sz])
        p += sz
    aq, ak, av, af, bqk, bv, bo, bi, bf, ca, cg = parts
    d = w.shape[0]
    main = jnp.concatenate(
        [_pad_heads(bqk[:, :B_WIDTH]), _pad_heads(bqk[:, B_WIDTH:]), _pad_heads(bv), _pad_heads(bo),
         ca, cg, aq, ak, av, jnp.zeros((d, Z_WIDTH - Z_USED), w.dtype)], axis=1)
    gates = jnp.concatenate([af, bi, bf, jnp.zeros((d, GATE_LANES - A_HEADS - 2 * B_HEADS), w.dtype)], axis=1)
    return main.astype(BF16), gates


def _tile_sizes(t, seq):
    return dict(
        tm_in=min(1024, t), tn_in=1280,
        tb_gate=min(256, seq),
        tq=min(512, seq),
        ts_conf=min(256, seq),
        tm_out=min(512, t),
        tm_q=min(512, t),
        tt=min(256, t),
        tm_pd=min(1024, t), te=512, tc_pd=min(256, t), rb_pd=32,
    )


def kernel(x, w_in, fox_f_bias, mlstm_conv_w, mlstm_i_bias, mlstm_f_bias, conf_dw_w, conf_dw_b, conf_ln_g, conf_ln_b, conf_pw_w, conf_pw_b, out_norm_g, w_out, norm1_g, norm2_g, peer_wq, peer_keys, peer_u, peer_v, final_g):
    batch, seq, d = x.shape
    t = batch * seq
    depth = w_in.shape[0]
    ts = _tile_sizes(t, seq)
    xf = x.reshape(t, d)
    row = lambda v: v.reshape(1, -1)

    for l in range(depth):
        w_main, w_gate = _regroup_w_in(w_in[l])
        wgh, wgl = _split_bf16(w_gate)
        z, zg = _in_proj(xf, row(norm1_g[l]), w_main, wgh, wgl, tm=ts["tm_in"], tn=ts["tn_in"])
        bias = jnp.concatenate([fox_f_bias[l], mlstm_i_bias[l], mlstm_f_bias[l],
                                jnp.zeros((GATE_LANES - A_HEADS - 2 * B_HEADS,), F32)]).reshape(1, GATE_LANES)
        gcol, grow = _gates(zg.reshape(batch, seq, GATE_LANES), bias, tb=ts["tb_gate"])
        g_a = row(out_norm_g[l, :A_WIDTH])
        g_b = row(_pad_heads(out_norm_g[l, A_WIDTH:A_WIDTH + B_WIDTH]))
        g_c = row(out_norm_g[l, A_WIDTH + B_WIDTH:])
        z3 = z.reshape(batch, seq, Z_WIDTH)
        ya = _attention(z, gcol, grow.reshape(batch, 16, 1, seq), g_a, batch=batch, seq=seq, tq=ts["tq"])
        cw = jnp.concatenate([_pad_heads(mlstm_conv_w[l, :, :B_WIDTH]), _pad_heads(mlstm_conv_w[l, :, B_WIDTH:])], axis=1)
        yb = _mlstm(z3, gcol, grow, cw, g_b)
        yc = _conformer(z3, conf_dw_w[l], row(conf_dw_b[l]), row(conf_ln_g[l]), row(conf_ln_b[l]),
                        conf_pw_w[l].astype(BF16), row(conf_pw_b[l]), g_c, ts=ts["ts_conf"])
        wo = w_out[l]
        wa = wo[:A_WIDTH].astype(BF16)
        wb = jnp.pad(wo[A_WIDTH:A_WIDTH + B_WIDTH].reshape(B_HEADS, B_HEAD_DIM, d),
                     ((0, 0), (0, B_HEAD_PAD - B_HEAD_DIM), (0, 0))).reshape(B_PAD_WIDTH, d).astype(BF16)
        wc = wo[A_WIDTH + B_WIDTH:].astype(BF16)
        xf = _out_proj(xf, ya, yb.reshape(t, B_PAD_WIDTH), yc.reshape(t, C_WIDTH), wa, wb, wc,
                       tm=ts["tm_out"])

        q, xn = _peer_q(xf, row(norm2_g[l]), peer_wq[l].astype(BF16), tm=ts["tm_q"])
        keys16 = peer_keys[l].reshape(2 * P_HEADS, N_KEYS, P_HALF)
        c1, w1, r2, e2 = _peer_topk(q, keys16, tt=ts["tt"])
        xf = _peer_dense(xn, peer_u, peer_v, l, c1, w1, r2, e2, xf, row(final_g),
                         tm=ts["tm_pd"], te=ts["te"], tc=ts["tc_pd"], rb=ts["rb_pd"], final=(l == depth - 1))

    return xf.reshape(batch, seq, d)
```

```python
import functools

import jax
import jax.numpy as jnp
from jax import lax
from jax.experimental import pallas as pl
from jax.experimental.pallas import tpu as pltpu

F32 = jnp.float32
BF16 = jnp.bfloat16

EPS = 1e-6
LN_EPS = 1e-5

A_HEADS = 6
A_HEAD_DIM = 128
A_WIDTH = A_HEADS * A_HEAD_DIM
B_HEADS = 4
B_HEAD_DIM = 192
B_HEAD_PAD = 256
B_WIDTH = B_HEADS * B_HEAD_DIM
B_PAD_WIDTH = B_HEADS * B_HEAD_PAD
B_CHUNK = 64
B_CONV = 4
C_GROUPS = 4
C_WIDTH = 512
C_CONV = 31
P_HEADS = 8
P_HALF = 128
N_KEYS = 128
P_TOPK = 16
GATE_LANES = 128

COL_BQ = 0
COL_BK = B_PAD_WIDTH
COL_BV = 2 * B_PAD_WIDTH
COL_BO = 3 * B_PAD_WIDTH
COL_CA = 4 * B_PAD_WIDTH
COL_CG = COL_CA + C_WIDTH
COL_AQ = COL_CG + C_WIDTH
COL_AK = COL_AQ + A_WIDTH
COL_AV = COL_AK + A_WIDTH
Z_USED = COL_AV + A_WIDTH
Z_WIDTH = 7680
G_AF = 0
G_BI = A_HEADS
G_BF = A_HEADS + B_HEADS

VMEM_LIMIT = 60 * 1024 * 1024


def _cparams(sem):
    return pltpu.CompilerParams(dimension_semantics=sem, vmem_limit_bytes=VMEM_LIMIT)


def _split_bf16(a):
    hi = a.astype(BF16)
    lo = (a - hi.astype(F32)).astype(BF16)
    return hi, lo


def _dot(a, b):
    return jnp.dot(a, b, preferred_element_type=F32)


def _dot_nt(a, b):
    return lax.dot_general(a, b, (((1,), (1,)), ((), ())), preferred_element_type=F32)


def _dot_tn(a, b):
    return lax.dot_general(a, b, (((0,), (0,)), ((), ())), preferred_element_type=F32)


def _dot3(ah, al, bh, bl, dot):
    return dot(ah, bh) + dot(ah, bl) + dot(al, bh)


def _sigmoid(x):
    return 1.0 / (1.0 + jnp.exp(-x))


def _inproj_body(x_ref, g_ref, w_ref, wgh_ref, wgl_ref, z_ref, zg_ref, xn_ref):
    @pl.when(pl.program_id(1) == 0)
    def _():
        xf = x_ref[...]
        ms = jnp.mean(xf * xf, axis=-1, keepdims=True)
        xn = xf * lax.rsqrt(ms + EPS) * g_ref[...]
        hi, lo = _split_bf16(xn)
        xn_ref[...] = hi
        zg_ref[...] = _dot3(hi, lo, wgh_ref[...], wgl_ref[...], _dot)

    z_ref[...] = _dot(xn_ref[...], w_ref[...])


def _in_proj(x2d, g, w, wgh, wgl, layer, *, tm, tn):
    t, d = x2d.shape
    n = w.shape[2]
    return pl.pallas_call(
        _inproj_body,
        grid=(t // tm, n // tn),
        in_specs=[
            pl.BlockSpec((tm, d), lambda i, j: (i, 0), pipeline_mode=pl.Buffered(1)),
            pl.BlockSpec((None, 1, d), lambda i, j: (layer, 0, 0)),
            pl.BlockSpec((None, d, tn), lambda i, j: (layer, 0, j)),
            pl.BlockSpec((None, d, GATE_LANES), lambda i, j: (layer, 0, 0)),
            pl.BlockSpec((None, d, GATE_LANES), lambda i, j: (layer, 0, 0)),
        ],
        out_specs=[
            pl.BlockSpec((tm, tn), lambda i, j: (i, j)),
            pl.BlockSpec((tm, GATE_LANES), lambda i, j: (i, 0)),
        ],
        out_shape=[
            jax.ShapeDtypeStruct((t, n), F32),
            jax.ShapeDtypeStruct((t, GATE_LANES), F32),
        ],
        scratch_shapes=[pltpu.VMEM((tm, d), BF16)],
        compiler_params=_cparams(("parallel", "arbitrary")),
        name="in_proj",
    )(x2d, g, w, wgh, wgl)


def _gates_body(zg_ref, bias_ref, col_ref, row_ref, carry_ref, *, tb):
    @pl.when(pl.program_id(1) == 0)
    def _():
        carry_ref[...] = jnp.zeros_like(carry_ref)

    v = zg_ref[0] + bias_ref[...]
    ls = jnp.minimum(v, 0.0) - jnp.log1p(jnp.exp(-jnp.abs(v)))
    r = lax.broadcasted_iota(jnp.int32, (tb, tb), 0)
    c = lax.broadcasted_iota(jnp.int32, (tb, tb), 1)
    tri = (c <= r).astype(F32)
    tri_chunk = jnp.where((r // B_CHUNK) == (c // B_CHUNK), tri, 0.0)
    csum = jnp.dot(tri, ls, precision=lax.Precision.HIGHEST, preferred_element_type=F32) + carry_ref[...]
    bcum = jnp.dot(tri_chunk, ls, precision=lax.Precision.HIGHEST, preferred_element_type=F32)
    carry_ref[...] = csum[tb - 1:tb, :]
    lane = lax.broadcasted_iota(jnp.int32, (tb, GATE_LANES), 1)
    out = jnp.where(lane < G_BI, csum, jnp.where(lane < G_BF, v, bcum))
    col_ref[0] = out
    row_ref[0] = out.T[0:16, :]


def _gates(zg3, bias, layer, *, tb):
    b, s, _ = zg3.shape
    return pl.pallas_call(
        functools.partial(_gates_body, tb=tb),
        grid=(b, s // tb),
        in_specs=[
            pl.BlockSpec((1, tb, GATE_LANES), lambda i, j: (i, j, 0)),
            pl.BlockSpec((None, 1, GATE_LANES), lambda i, j: (layer, 0, 0)),
        ],
        out_specs=[
            pl.BlockSpec((1, tb, GATE_LANES), lambda i, j: (i, j, 0)),
            pl.BlockSpec((1, 16, tb), lambda i, j: (i, 0, j)),
        ],
        out_shape=[
            jax.ShapeDtypeStruct((b, s, GATE_LANES), F32),
            jax.ShapeDtypeStruct((b, 16, s), F32),
        ],
        scratch_shapes=[pltpu.VMEM((1, GATE_LANES), F32)],
        compiler_params=_cparams(("parallel", "arbitrary")),
        name="gates",
    )(zg3, bias)


def _attn_body(q_ref, k_ref, v_ref, ccol_ref, crow_ref, g_ref, o_ref, kb_ref, vb_ref, m_ref, l_ref, acc_ref, *, tq):
    h = pl.program_id(1)
    qi = pl.program_id(2)
    log2e = 1.4426950408889634
    scale2 = (A_HEAD_DIM ** -0.5) * log2e

    @pl.when(qi == 0)
    def _():
        kb_ref[...] = k_ref[...].astype(BF16)
        vb_ref[...] = v_ref[...].astype(BF16)

    q = q_ref[...].astype(BF16)
    lane = lax.broadcasted_iota(jnp.int32, (tq, GATE_LANES), 1)
    ct2 = jnp.sum(jnp.where(lane == h, ccol_ref[0], 0.0), axis=-1, keepdims=True) * log2e
    m_ref[...] = jnp.full(m_ref.shape, -jnp.inf, F32)
    l_ref[...] = jnp.zeros_like(l_ref)
    acc_ref[...] = jnp.zeros_like(acc_ref)

    def logits(j):
        k0 = pl.multiple_of(j * tq, tq)
        cs2 = crow_ref[0, 0, :, pl.ds(k0, tq)] * log2e
        return _dot_nt(q, kb_ref[pl.ds(k0, tq), :]) * scale2 + (ct2 - cs2)

    def softmax(s):
        m_prev = m_ref[...]
        m_new = jnp.maximum(m_prev, jnp.max(s, axis=-1, keepdims=True))
        p = jnp.exp2(s - m_new)
        alpha = jnp.exp2(m_prev - m_new)
        l_ref[...] = alpha * l_ref[...] + jnp.sum(p, axis=-1, keepdims=True)
        m_ref[...] = m_new
        return p.astype(BF16), alpha

    def values(j, p, alpha):
        k0 = pl.multiple_of(j * tq, tq)
        acc_ref[...] = alpha * acc_ref[...] + _dot(p, vb_ref[pl.ds(k0, tq), :])

    def body(j, s_cur):
        s_next = logits(j + 1)
        p, alpha = softmax(s_cur)
        values(j, p, alpha)
        return s_next

    s_diag = lax.fori_loop(0, qi, body, logits(0))
    row = lax.broadcasted_iota(jnp.int32, (tq, tq), 0)
    col = lax.broadcasted_iota(jnp.int32, (tq, tq), 1)
    p, alpha = softmax(jnp.where(col <= row, s_diag, -jnp.inf))
    values(qi, p, alpha)
    o = acc_ref[...] / l_ref[...]
    o = o * lax.rsqrt(jnp.mean(o * o, axis=-1, keepdims=True) + EPS) * g_ref[...]
    o_ref[...] = o.astype(BF16)


def _attention(z2d, gcol, grow, g_a, layer, *, batch, seq, tq):
    nq = seq // tq
    qc, kc, vc = COL_AQ // 128, COL_AK // 128, COL_AV // 128
    return pl.pallas_call(
        functools.partial(_attn_body, tq=tq),
        grid=(batch, A_HEADS, nq),
        in_specs=[
            pl.BlockSpec((tq, 128), lambda b, h, i: (b * nq + i, qc + h)),
            pl.BlockSpec((seq, 128), lambda b, h, i: (b, kc + h)),
            pl.BlockSpec((seq, 128), lambda b, h, i: (b, vc + h)),
            pl.BlockSpec((1, tq, GATE_LANES), lambda b, h, i: (b, i, 0)),
            pl.BlockSpec((1, 1, 1, seq), lambda b, h, i: (b, h, 0, 0)),
            pl.BlockSpec((None, 1, 128), lambda b, h, i: (layer, 0, h)),
        ],
        out_specs=pl.BlockSpec((tq, 128), lambda b, h, i: (b * nq + i, h)),
        out_shape=jax.ShapeDtypeStruct((batch * seq, A_WIDTH), BF16),
        scratch_shapes=[pltpu.VMEM((seq, 128), BF16), pltpu.VMEM((seq, 128), BF16),
                        pltpu.VMEM((tq, 1), F32), pltpu.VMEM((tq, 1), F32), pltpu.VMEM((tq, 128), F32)],
        compiler_params=_cparams(("parallel", "parallel", "arbitrary")),
        name="fox_attention",
    )(z2d, z2d, z2d, gcol, grow, g_a)


def _mlstm_body(zb_ref, gcol_ref, grow_ref, cw_ref, g_ref, y_ref, ext_ref, c_ref, m_ref, *, nb):
    k = pl.program_id(0)
    ln = B_CHUNK
    hp = B_HEAD_PAD

    @pl.when(k == 0)
    def _():
        ext_ref[:, 0:8, :] = jnp.zeros((nb, 8, 2 * B_PAD_WIDTH), F32)
        c_ref[...] = jnp.zeros_like(c_ref)
        m_ref[...] = jnp.zeros_like(m_ref)

    first_half = (k % 2) == 0
    r = lax.broadcasted_iota(jnp.int32, (ln, ln), 0)
    c = lax.broadcasted_iota(jnp.int32, (ln, ln), 1)
    causal = c <= r
    lane = lax.broadcasted_iota(jnp.int32, (ln, hp), 1)
    qscale = B_HEAD_DIM ** -0.5

    for b in range(nb):
        ext_ref[b, 8:8 + ln, :] = zb_ref[b, :, 0:2 * B_PAD_WIDTH]
        conv = cw_ref[0:1, :] * ext_ref[b, 5:5 + ln, :]
        for j in range(1, B_CONV):
            conv = conv + cw_ref[j:j + 1, :] * ext_ref[b, 5 + j:5 + j + ln, :]
        ext_ref[b, 0:8, :] = ext_ref[b, ln:ln + 8, :]
        qk = conv * _sigmoid(conv)
        grow = grow_ref[b]
        grow = jnp.where(first_half, grow[:, 0:ln], grow[:, ln:2 * ln])
        gcol = gcol_ref[b]
        for h in range(B_HEADS):
            st = b * B_HEADS + h
            qq = (qk[:, COL_BQ + h * hp:COL_BQ + (h + 1) * hp] * qscale).astype(BF16)
            kk = qk[:, COL_BK + h * hp:COL_BK + (h + 1) * hp]
            vv = zb_ref[b, :, COL_BV + h * hp:COL_BV + (h + 1) * hp]
            vv = jnp.where(lane == B_HEAD_DIM, 1.0, vv).astype(BF16)
            og = zb_ref[b, :, COL_BO + h * hp:COL_BO + (h + 1) * hp]
            bc_col = gcol[:, G_BF + h:G_BF + h + 1]
            li_col = gcol[:, G_BI + h:G_BI + h + 1]
            bc_row = grow[G_BF + h:G_BF + h + 1, :]
            li_row = grow[G_BI + h:G_BI + h + 1, :]
            m = m_ref[st, 0:1, 0:1]

            d = jnp.where(causal, bc_col - bc_row + li_row, -jnp.inf)
            inter = bc_col + m
            m_t = jnp.maximum(inter, jnp.max(d, axis=-1, keepdims=True))
            w_inter = jnp.exp(inter - m_t)
            sc = _dot_nt(qq, kk.astype(BF16)) * jnp.exp(d - m_t)
            cm = c_ref[st]
            num = w_inter * _dot(qq, cm.astype(BF16)) + _dot(sc.astype(BF16), vv)
            den = num[:, B_HEAD_DIM:B_HEAD_DIM + 1]
            hh = num / jnp.maximum(jnp.abs(den), jnp.exp(-m_t))

            b_last = bc_col[ln - 1:ln, :]
            dec = b_last - bc_col + li_col
            m_new = jnp.maximum(b_last + m, jnp.max(dec, axis=0, keepdims=True))
            a = jnp.exp(b_last + m - m_new)
            wk = (jnp.exp(dec - m_new) * kk).astype(BF16)
            c_ref[st] = a * cm + _dot_tn(wk, vv)
            m_ref[st] = jnp.broadcast_to(m_new, (8, 128))

            y = jnp.where(lane < B_HEAD_DIM, _sigmoid(og) * hh, 0.0)
            ms = jnp.sum(y * y, axis=-1, keepdims=True) * (1.0 / B_HEAD_DIM)
            y = y * lax.rsqrt(ms + EPS) * g_ref[:, h * hp:(h + 1) * hp]
            y_ref[b, :, h * hp:(h + 1) * hp] = y.astype(BF16)


def _mlstm(z3, gcol, grow, cw, g_b, layer):
    nb, seq, _ = z3.shape
    nc = seq // B_CHUNK
    return pl.pallas_call(
        functools.partial(_mlstm_body, nb=nb),
        grid=(nc,),
        in_specs=[
            pl.BlockSpec((nb, B_CHUNK, 4 * B_PAD_WIDTH), lambda k: (0, k, 0)),
            pl.BlockSpec((nb, B_CHUNK, GATE_LANES), lambda k: (0, k, 0)),
            pl.BlockSpec((nb, 16, 2 * B_CHUNK), lambda k: (0, 0, k // 2)),
            pl.BlockSpec((None, B_CONV, 2 * B_PAD_WIDTH), lambda k: (layer, 0, 0)),
            pl.BlockSpec((None, 1, B_PAD_WIDTH), lambda k: (layer, 0, 0)),
        ],
        out_specs=pl.BlockSpec((nb, B_CHUNK, B_PAD_WIDTH), lambda k: (0, k, 0)),
        out_shape=jax.ShapeDtypeStruct((nb, seq, B_PAD_WIDTH), BF16),
        scratch_shapes=[
            pltpu.VMEM((nb, B_CHUNK + 8, 2 * B_PAD_WIDTH), F32),
            pltpu.VMEM((nb * B_HEADS, B_HEAD_PAD, B_HEAD_PAD), F32),
            pltpu.VMEM((nb * B_HEADS, 8, 128), F32),
        ],
        compiler_params=_cparams(("arbitrary",)),
        name="mlstm",
    )(z3, gcol, grow, cw, g_b)


def _conf_body(ca_ref, cg_ref, dw_ref, db_ref, lg_ref, lb_ref, pw_ref, pb_ref, g_ref, y_ref, ext_ref, *, ts):
    halo = 32

    @pl.when(pl.program_id(1) == 0)
    def _():
        ext_ref[0:halo, :] = jnp.zeros((halo, C_WIDTH), F32)

    ext_ref[halo:halo + ts, :] = ca_ref[0] * _sigmoid(cg_ref[0])
    off = halo - (C_CONV - 1)
    acc = None
    for r in range(8):
        taps = [(m, 8 * m + r - off) for m in range(halo // 8 + 1) if 0 <= 8 * m + r - off < C_CONV]
        ln = ts if r == 0 else ts + 8
        part = None
        for m, j in taps:
            term = dw_ref[j:j + 1, :] * ext_ref[8 * m:8 * m + ln, :]
            part = term if part is None else part + term
        part = part if r == 0 else part[r:r + ts, :]
        acc = part if acc is None else acc + part
    ext_ref[0:halo, :] = ext_ref[ts:ts + halo, :]
    y = acc + db_ref[...]
    mu = jnp.mean(y, axis=-1, keepdims=True)
    yc = y - mu
    y = yc * lax.rsqrt(jnp.mean(yc * yc, axis=-1, keepdims=True) + LN_EPS)
    y = y * lg_ref[...] + lb_ref[...]
    y = y * _sigmoid(y)
    y = _dot(y.astype(BF16), pw_ref[...]) + pb_ref[...]
    gw = C_WIDTH // C_GROUPS
    for gi in range(C_GROUPS):
        yg = y[:, gi * gw:(gi + 1) * gw]
        yg = yg * lax.rsqrt(jnp.mean(yg * yg, axis=-1, keepdims=True) + EPS) * g_ref[:, gi * gw:(gi + 1) * gw]
        y_ref[0, :, gi * gw:(gi + 1) * gw] = yg.astype(BF16)


def _conformer(z3, dw, db, lg, lb, pw, pb, g_c, layer, *, ts):
    nb, seq, _ = z3.shape
    ca, cg = COL_CA // C_WIDTH, COL_CG // C_WIDTH
    vec = pl.BlockSpec((None, 1, C_WIDTH), lambda b, i: (layer, 0, 0))
    return pl.pallas_call(
        functools.partial(_conf_body, ts=ts),
        grid=(nb, seq // ts),
        in_specs=[
            pl.BlockSpec((1, ts, C_WIDTH), lambda b, i: (b, i, ca)),
            pl.BlockSpec((1, ts, C_WIDTH), lambda b, i: (b, i, cg)),
            pl.BlockSpec((None, C_CONV, C_WIDTH), lambda b, i: (layer, 0, 0)),
            vec, vec, vec,
            pl.BlockSpec((None, C_WIDTH, C_WIDTH), lambda b, i: (layer, 0, 0)),
            vec, vec,
        ],
        out_specs=pl.BlockSpec((1, ts, C_WIDTH), lambda b, i: (b, i, 0)),
        out_shape=jax.ShapeDtypeStruct((nb, seq, C_WIDTH), BF16),
        scratch_shapes=[pltpu.VMEM((ts + 32, C_WIDTH), F32)],
        compiler_params=_cparams(("parallel", "arbitrary")),
        name="conformer",
    )(z3, z3, dw, db, lg, lb, pw, pb, g_c)


def _outproj_body(x_ref, ya_ref, yb_ref, yc_ref, wa_ref, wb_ref, wc_ref, o_ref):
    o_ref[...] = (x_ref[...] + _dot(ya_ref[...], wa_ref[...]) + _dot(yb_ref[...], wb_ref[...])
                  + _dot(yc_ref[...], wc_ref[...]))


def _out_proj(x2d, ya, yb, yc, wa, wb, wc, layer, *, tm):
    t, d = x2d.shape
    once = pl.Buffered(1)
    return pl.pallas_call(
        _outproj_body,
        grid=(t // tm,),
        in_specs=[
            pl.BlockSpec((tm, d), lambda i: (i, 0)),
            pl.BlockSpec((tm, ya.shape[1]), lambda i: (i, 0)),
            pl.BlockSpec((tm, yb.shape[1]), lambda i: (i, 0)),
            pl.BlockSpec((tm, yc.shape[1]), lambda i: (i, 0)),
            pl.BlockSpec((None,) + wa.shape[1:], lambda i: (layer, 0, 0), pipeline_mode=once),
            pl.BlockSpec((None,) + wb.shape[1:], lambda i: (layer, 0, 0), pipeline_mode=once),
            pl.BlockSpec((None,) + wc.shape[1:], lambda i: (layer, 0, 0), pipeline_mode=once),
        ],
        out_specs=pl.BlockSpec((tm, d), lambda i: (i, 0)),
        out_shape=jax.ShapeDtypeStruct((t, d), F32),
        compiler_params=_cparams(("parallel",)),
        name="out_proj",
    )(x2d, ya, yb, yc, wa, wb, wc)


def _peerq_body(x_ref, g_ref, w_ref, q_ref, xn_ref):
    xf = x_ref[...]
    ms = jnp.mean(xf * xf, axis=-1, keepdims=True)
    xn = (xf * lax.rsqrt(ms + EPS) * g_ref[...]).astype(BF16)
    xn_ref[...] = xn
    q_ref[...] = _dot(xn, w_ref[...])


def _peer_q(x2d, g, w, layer, *, tm):
    t, d = x2d.shape
    n = w.shape[2]
    return pl.pallas_call(
        _peerq_body,
        grid=(t // tm,),
        in_specs=[
            pl.BlockSpec((tm, d), lambda i: (i, 0)),
            pl.BlockSpec((None, 1, d), lambda i: (layer, 0, 0)),
            pl.BlockSpec((None, d, n), lambda i: (layer, 0, 0), pipeline_mode=pl.Buffered(1)),
        ],
        out_specs=[
            pl.BlockSpec((tm, n), lambda i: (i, 0)),
            pl.BlockSpec((tm, d), lambda i: (i, 0)),
        ],
        out_shape=[
            jax.ShapeDtypeStruct((t, n), F32),
            jax.ShapeDtypeStruct((t, d), BF16),
        ],
        compiler_params=_cparams(("parallel",)),
        name="peer_q",
    )(x2d, g, w)


def _extract_top(s, rowid, n_out):
    nrow = s.shape[0]
    rank = jnp.full(s.shape, float(n_out), F32)
    vals = []
    for it in range(n_out):
        m = jnp.max(s, axis=0, keepdims=True)
        idx = jnp.min(jnp.where(s == m, rowid, float(nrow)), axis=0, keepdims=True)
        hit = rowid == idx
        vals.append(m)
        rank = jnp.where(hit, float(it), rank)
        s = jnp.where(hit, -jnp.inf, s)
    return vals, rank, rank < float(n_out)


def _extract_top_distinct(s, n_out, want_rank):
    rank = jnp.full(s.shape, float(n_out), F32) if want_rank else None
    vals = []
    for it in range(n_out):
        m = jnp.max(s, axis=0, keepdims=True)
        hit = s == m
        vals.append(m)
        if want_rank:
            rank = jnp.where(hit, float(it), rank)
        s = jnp.where(hit, -jnp.inf, s)
    taken = s == -jnp.inf
    return vals, rank, taken, jnp.sum(jnp.where(taken, 1.0, 0.0), axis=0, keepdims=True)


_CAND_SHORT = 8


def _topk_body(q_ref, keys_ref, c1_ref, w1_ref, r2_ref, e2_ref, *, tt):
    rowid = lax.broadcasted_iota(jnp.int32, (N_KEYS, tt), 0).astype(F32)
    rowid2 = lax.broadcasted_iota(jnp.int32, (P_TOPK * P_TOPK, tt), 0).astype(F32)

    def pair_grid(x1, x2, op):
        parts = [op(x1[0:1, :], x2)]
        parts += [op(x1[a:a + 1, :], x2[0:_CAND_SHORT, :]) for a in range(1, _CAND_SHORT)]
        parts += [op(x1[_CAND_SHORT:, :], x2[0:1, :])]
        return jnp.concatenate(parts, axis=0)

    def finish(h, scores, v1, v2, is_rank, rank2, counts, zsum):
        cnt = jnp.zeros((N_KEYS, tt), F32)
        for a in range(P_TOPK):
            cnt = jnp.where(is_rank(a), counts[a], cnt)
        c1_ref[h] = cnt
        w1_ref[h] = jnp.exp(scores[0] - v1[0:1, :]) / zsum
        r2_ref[h] = rank2.astype(BF16)
        e2_ref[h] = jnp.exp(scores[1] - v2[0:1, :]).astype(BF16)

    def per_head(h, carry):
        scores = []
        for c in range(2):
            kh, kl = _split_bf16(keys_ref[2 * h + c])
            col = pl.multiple_of((2 * h + c) * P_HALF, P_HALF)
            qh, ql = _split_bf16(q_ref[:, pl.ds(col, P_HALF)])
            scores.append(_dot3(kh, kl, qh, ql, _dot_nt))

        tops, ranks, clean = [], [], None
        for c in range(2):
            vals, rank, _, lost = _extract_top_distinct(scores[c], P_TOPK, want_rank=(c == 1))
            tops.append(jnp.concatenate(vals, axis=0))
            ranks.append(rank)
            ok = lost == float(P_TOPK)
            clean = ok if clean is None else (clean & ok)
        v1, v2 = tops
        cand = pair_grid(v1, v2, lambda x, y: x + y)
        _, _, picked, lost = _extract_top_distinct(cand, P_TOPK, want_rank=False)
        clean = clean & (lost == float(P_TOPK))
        ev1 = jnp.exp(v1 - v1[0:1, :])
        ev2 = jnp.exp(v2 - v2[0:1, :])
        prod = pair_grid(ev1, ev2, lambda x, y: x * y)
        zsum = jnp.sum(jnp.where(picked, prod, 0.0), axis=0, keepdims=True)
        npick = jnp.where(picked, 1.0, 0.0)
        counts = [jnp.sum(npick[0:P_TOPK, :], axis=0, keepdims=True)]
        for a in range(1, _CAND_SHORT):
            r0 = P_TOPK + (a - 1) * _CAND_SHORT
            counts.append(jnp.sum(npick[r0:r0 + _CAND_SHORT, :], axis=0, keepdims=True))
        r0 = P_TOPK + (_CAND_SHORT - 1) * _CAND_SHORT
        counts += [npick[r0 + a:r0 + a + 1, :] for a in range(P_TOPK - _CAND_SHORT)]
        finish(h, scores, v1, v2, lambda a: scores[0] == v1[a:a + 1, :], ranks[1], counts, zsum)

        @pl.when(jnp.min(jnp.where(clean, 1.0, 0.0)) < 0.5)
        def _():
            tops, ranks = [], []
            for c in range(2):
                vals, rank, _ = _extract_top(scores[c], rowid, P_TOPK)
                tops.append(jnp.concatenate(vals, axis=0))
                ranks.append(rank)
            v1, v2 = tops
            cand = jnp.concatenate([v1[a:a + 1, :] + v2 for a in range(P_TOPK)], axis=0)
            _, _, picked = _extract_top(cand, rowid2, P_TOPK)
            ev1 = jnp.exp(v1 - v1[0:1, :])
            ev2 = jnp.exp(v2 - v2[0:1, :])
            prod = jnp.concatenate([ev1[a:a + 1, :] * ev2 for a in range(P_TOPK)], axis=0)
            zsum = jnp.sum(jnp.where(picked, prod, 0.0), axis=0, keepdims=True)
            npick = jnp.where(picked, 1.0, 0.0)
            counts = [jnp.sum(npick[a * P_TOPK:(a + 1) * P_TOPK, :], axis=0, keepdims=True) for a in range(P_TOPK)]
            finish(h, scores, v1, v2, lambda a: ranks[0] == float(a), ranks[1], counts, zsum)

        return carry

    lax.fori_loop(0, P_HEADS, per_head, 0)


def _peer_topk(q, keys16, layer, *, tt):
    t = q.shape[0]
    st_shape = jax.ShapeDtypeStruct((P_HEADS, N_KEYS, t), F32)
    st_shape_b = jax.ShapeDtypeStruct((P_HEADS, N_KEYS, t), BF16)
    st_spec = pl.BlockSpec((P_HEADS, N_KEYS, tt), lambda i: (0, 0, i))
    return pl.pallas_call(
        functools.partial(_topk_body, tt=tt),
        grid=(t // tt,),
        in_specs=[
            pl.BlockSpec((tt, q.shape[1]), lambda i: (i, 0)),
            pl.BlockSpec((None,) + keys16.shape[1:], lambda i: (layer, 0, 0, 0)),
        ],
        out_specs=[st_spec] * 4,
        out_shape=[st_shape, st_shape, st_shape_b, st_shape_b],
        compiler_params=_cparams(("parallel",)),
        name="peer_topk",
    )(q, keys16)


def _peer_dense_body(xn_ref, u_ref, v_ref, c1_ref, w1_ref, r2_ref, e2_ref, x_ref, g_ref, o_ref, c1s_ref, w1s_ref, *hg_refs,
                     tm, te, tc, rb, final):
    e = pl.program_id(1)

    @pl.when(e == 0)
    def _():
        o_ref[...] = x_ref[...]

    ub = u_ref[...].astype(BF16)
    vb = v_ref[...].astype(BF16)

    ng = te // N_KEYS
    nchunk = tm // tc
    which = e % (8 // ng)

    for src, dst in ((c1_ref, c1s_ref), (w1_ref, w1s_ref)):
        rows = src[:, 0:ng, :]
        for g in range(1, 8 // ng):
            rows = jnp.where(which == g, src[:, g * ng:(g + 1) * ng, :], rows)
        dst[:, 0:ng, :] = rows

    def bf16_row(ref, h, ii, t0):
        return ref[h, ii:ii + 1, t0:t0 + tc].astype(BF16)

    def stage_a(ci):
        t0 = ci * tc
        ht = _dot_nt(ub, xn_ref[t0:t0 + tc, :])
        return ((0.5 * ht) * (1.0 + lax.erf(ht * (2.0 ** -0.5)))).astype(BF16)

    def stage_b(ci, act):
        t0 = ci * tc
        for ii in range(ng):
            rows = [(bf16_row(c1s_ref, h, ii, t0), bf16_row(w1s_ref, h, ii, t0)) for h in range(P_HEADS)]
            for j0 in range(0, N_KEYS, rb):
                r0 = ii * N_KEYS + j0
                gate = jnp.zeros((rb, tc), BF16)
                for h in range(P_HEADS):
                    cnt, w1 = rows[h]
                    gate = gate + jnp.where(r2_ref[h, j0:j0 + rb, t0:t0 + tc] < cnt,
                                            e2_ref[h, j0:j0 + rb, t0:t0 + tc] * w1, jnp.zeros((), BF16))
                hg_refs[ci][r0:r0 + rb, :] = act[r0:r0 + rb, :] * gate

    def stage_c(ci):
        t0 = ci * tc
        o_ref[t0:t0 + tc, :] += _dot_tn(hg_refs[ci][...], vb)

    ahead = 2
    hts = {ci: stage_a(ci) for ci in range(min(ahead, nchunk))}
    for ci in range(nchunk):
        if ci + ahead < nchunk:
            hts[ci + ahead] = stage_a(ci + ahead)
        stage_b(ci, hts.pop(ci))
        stage_c(ci)

    if final:
        @pl.when(e == pl.num_programs(1) - 1)
        def _():
            y = o_ref[...]
            o_ref[...] = y * lax.rsqrt(jnp.mean(y * y, axis=-1, keepdims=True) + EPS) * g_ref[...]


def _peer_dense(xn, u, v, layer, c1, w1, r2, e2, x2d, g, *, tm, te, tc, rb, final):
    t, d = xn.shape
    ne = u.shape[1]
    ng = te // N_KEYS
    once = pl.Buffered(1)
    row_spec = pl.BlockSpec((P_HEADS, 8, tm), lambda i, e: (0, e // (8 // ng), i))
    full_spec = pl.BlockSpec((P_HEADS, N_KEYS, tm), lambda i, e: (0, 0, i), pipeline_mode=once)
    tok_spec = pl.BlockSpec((tm, d), lambda i, e: (i, 0), pipeline_mode=once)
    return pl.pallas_call(
        functools.partial(_peer_dense_body, tm=tm, te=te, tc=tc, rb=rb, final=final),
        grid=(t // tm, ne // te),
        in_specs=[
            tok_spec,
            pl.BlockSpec((None, te, d), lambda i, e: (layer, e, 0)),
            pl.BlockSpec((None, te, d), lambda i, e: (layer, e, 0)),
            row_spec, row_spec, full_spec, full_spec,
            tok_spec,
            pl.BlockSpec((1, d), lambda i, e: (0, 0)),
        ],
        out_specs=pl.BlockSpec((tm, d), lambda i, e: (i, 0), pipeline_mode=once),
        out_shape=jax.ShapeDtypeStruct((t, d), F32),
        scratch_shapes=[pltpu.VMEM((P_HEADS, 8, tm), F32), pltpu.VMEM((P_HEADS, 8, tm), F32)]
        + [pltpu.VMEM((te, tc), BF16)] * (tm // tc),
        compiler_params=_cparams(("parallel", "arbitrary")),
        name="peer_dense",
    )(xn, u, v, c1, w1, r2, e2, x2d, g)


def _pad_heads(w):
    lead = w.shape[:-1]
    w = w.reshape(lead + (B_HEADS, B_HEAD_DIM))
    w = jnp.pad(w, [(0, 0)] * len(lead) + [(0, 0), (0, B_HEAD_PAD - B_HEAD_DIM)])
    return w.reshape(lead + (B_PAD_WIDTH,))


def _regroup_w_in(w):
    sizes = (A_WIDTH, A_WIDTH, A_WIDTH, A_HEADS, 2 * B_WIDTH, B_WIDTH, B_WIDTH, B_HEADS, B_HEADS, C_WIDTH, C_WIDTH)
    parts, p = [], 0
    for sz in sizes:
        parts.append(w[..., p:p + sz])
        p += sz
    aq, ak, av, af, bqk, bv, bo, bi, bf, ca, cg = parts
    lead = w.shape[:-1]
    main = jnp.concatenate(
        [_pad_heads(bqk[..., :B_WIDTH]), _pad_heads(bqk[..., B_WIDTH:]), _pad_heads(bv), _pad_heads(bo),
         ca, cg, aq, ak, av, jnp.zeros(lead + (Z_WIDTH - Z_USED,), w.dtype)], axis=-1)
    gates = jnp.concatenate([af, bi, bf, jnp.zeros(lead + (GATE_LANES - A_HEADS - 2 * B_HEADS,), w.dtype)], axis=-1)
    return main.astype(BF16), gates


def _tile_sizes(t, seq):
    return dict(
        tm_in=min(1024, t), tn_in=1280,
        tb_gate=min(256, seq),
        tq=min(512, seq),
        ts_conf=min(256, seq),
        tm_out=min(512, t),
        tm_q=min(512, t),
        tt=min(256, t),
        tm_pd=min(1024, t), te=512, tc_pd=min(256, t), rb_pd=32,
    )


def kernel(x, w_in, fox_f_bias, mlstm_conv_w, mlstm_i_bias, mlstm_f_bias, conf_dw_w, conf_dw_b, conf_ln_g, conf_ln_b, conf_pw_w, conf_pw_b, out_norm_g, w_out, norm1_g, norm2_g, peer_wq, peer_keys, peer_u, peer_v, final_g):
    batch, seq, d = x.shape
    t = batch * seq
    depth = w_in.shape[0]
    ts = _tile_sizes(t, seq)
    xf = x.reshape(t, d)
    rows = lambda v: v[:, None, :]

    w_main, w_gate = _regroup_w_in(w_in)
    wgh, wgl = _split_bf16(w_gate)
    gate_bias = rows(jnp.concatenate([fox_f_bias, mlstm_i_bias, mlstm_f_bias,
                                      jnp.zeros((depth, GATE_LANES - A_HEADS - 2 * B_HEADS), F32)], axis=-1))
    g_a = rows(out_norm_g[:, :A_WIDTH])
    g_b = rows(_pad_heads(out_norm_g[:, A_WIDTH:A_WIDTH + B_WIDTH]))
    g_c = rows(out_norm_g[:, A_WIDTH + B_WIDTH:])
    cw = jnp.concatenate([_pad_heads(mlstm_conv_w[..., :B_WIDTH]), _pad_heads(mlstm_conv_w[..., B_WIDTH:])], axis=-1)
    wa = w_out[:, :A_WIDTH].astype(BF16)
    wb = jnp.pad(w_out[:, A_WIDTH:A_WIDTH + B_WIDTH].reshape(depth, B_HEADS, B_HEAD_DIM, d),
                 ((0, 0), (0, 0), (0, B_HEAD_PAD - B_HEAD_DIM), (0, 0))).reshape(depth, B_PAD_WIDTH, d).astype(BF16)
    wc = w_out[:, A_WIDTH + B_WIDTH:].astype(BF16)
    pw = conf_pw_w.astype(BF16)
    wq = peer_wq.astype(BF16)
    keys16 = peer_keys.reshape(depth, 2 * P_HEADS, N_KEYS, P_HALF)
    n1, n2 = rows(norm1_g), rows(norm2_g)
    db, lg, lb, pb = rows(conf_dw_b), rows(conf_ln_g), rows(conf_ln_b), rows(conf_pw_b)

    for l in range(depth):
        z, zg = _in_proj(xf, n1, w_main, wgh, wgl, l, tm=ts["tm_in"], tn=ts["tn_in"])
        gcol, grow = _gates(zg.reshape(batch, seq, GATE_LANES), gate_bias, l, tb=ts["tb_gate"])
        z3 = z.reshape(batch, seq, Z_WIDTH)
        ya = _attention(z, gcol, grow.reshape(batch, 16, 1, seq), g_a, l, batch=batch, seq=seq, tq=ts["tq"])
        yb = _mlstm(z3, gcol, grow, cw, g_b, l)
        yc = _conformer(z3, conf_dw_w, db, lg, lb, pw, pb, g_c, l, ts=ts["ts_conf"])
        xf = _out_proj(xf, ya, yb.reshape(t, B_PAD_WIDTH), yc.reshape(t, C_WIDTH), wa, wb, wc, l, tm=ts["tm_out"])

        q, xn = _peer_q(xf, n2, wq, l, tm=ts["tm_q"])
        c1, w1, r2, e2 = _peer_topk(q, keys16, l, tt=ts["tt"])
        xf = _peer_dense(xn, peer_u, peer_v, l, c1, w1, r2, e2, xf, final_g.reshape(1, d),
                         tm=ts["tm_pd"], te=ts["te"], tc=ts["tc_pd"], rb=ts["rb_pd"], final=(l == depth - 1))

    return xf.reshape(batch, seq, d)
```

```python
import functools

import jax
import jax.numpy as jnp
from jax import lax
from jax.experimental import pallas as pl
from jax.experimental.pallas import tpu as pltpu

F32 = jnp.float32
BF16 = jnp.bfloat16

EPS = 1e-6
LN_EPS = 1e-5

A_HEADS = 6
A_HEAD_DIM = 128
A_WIDTH = A_HEADS * A_HEAD_DIM
B_HEADS = 4
B_HEAD_DIM = 192
B_HEAD_PAD = 256
B_WIDTH = B_HEADS * B_HEAD_DIM
B_PAD_WIDTH = B_HEADS * B_HEAD_PAD
B_CHUNK = 64
B_CONV = 4
C_GROUPS = 4
C_WIDTH = 512
C_CONV = 31
P_HEADS = 8
P_HALF = 128
N_KEYS = 128
P_TOPK = 16
GATE_LANES = 128

COL_BQ = 0
COL_BK = B_PAD_WIDTH
COL_BV = 2 * B_PAD_WIDTH
COL_BO = 3 * B_PAD_WIDTH
COL_CA = 4 * B_PAD_WIDTH
COL_CG = COL_CA + C_WIDTH
COL_AQ = COL_CG + C_WIDTH
COL_AK = COL_AQ + A_WIDTH
COL_AV = COL_AK + A_WIDTH
Z_USED = COL_AV + A_WIDTH
Z_WIDTH = 7680
G_BI = A_HEADS
G_BF = A_HEADS + B_HEADS
GATE_ROWS = 16

VMEM_LIMIT = 60 * 1024 * 1024


def _cparams(sem):
    return pltpu.CompilerParams(dimension_semantics=sem, vmem_limit_bytes=VMEM_LIMIT)


def _split_bf16(a):
    hi = a.astype(BF16)
    lo = (a - hi.astype(F32)).astype(BF16)
    return hi, lo


def _dot(a, b):
    return jnp.dot(a, b, preferred_element_type=F32)


def _dot_nt(a, b):
    return lax.dot_general(a, b, (((1,), (1,)), ((), ())), preferred_element_type=F32)


def _dot_tn(a, b):
    return lax.dot_general(a, b, (((0,), (0,)), ((), ())), preferred_element_type=F32)


def _dot3(ah, al, bh, bl, dot):
    return dot(ah, bh) + dot(ah, bl) + dot(al, bh)


def _sigmoid(x):
    return 1.0 / (1.0 + jnp.exp(-x))


def _inproj_body(x_ref, g_ref, w_ref, wgh_ref, wgl_ref, z_ref, zg_ref, xn_ref):
    @pl.when(pl.program_id(1) == 0)
    def _():
        xf = x_ref[...]
        ms = jnp.mean(xf * xf, axis=-1, keepdims=True)
        xn = xf * lax.rsqrt(ms + EPS) * g_ref[...]
        hi, lo = _split_bf16(xn)
        xn_ref[...] = hi
        zg_ref[...] = _dot3(hi, lo, wgh_ref[...], wgl_ref[...], _dot)

    z_ref[...] = _dot(xn_ref[...], w_ref[...])


def _in_proj(x2d, g, w, wgh, wgl, layer, *, tm, tn):
    t, d = x2d.shape
    n = w.shape[2]
    return pl.pallas_call(
        _inproj_body,
        grid=(t // tm, n // tn),
        in_specs=[
            pl.BlockSpec((tm, d), lambda i, j: (i, 0), pipeline_mode=pl.Buffered(1)),
            pl.BlockSpec((None, 1, d), lambda i, j: (layer, 0, 0)),
            pl.BlockSpec((None, d, tn), lambda i, j: (layer, 0, j)),
            pl.BlockSpec((None, d, GATE_LANES), lambda i, j: (layer, 0, 0)),
            pl.BlockSpec((None, d, GATE_LANES), lambda i, j: (layer, 0, 0)),
        ],
        out_specs=[
            pl.BlockSpec((tm, tn), lambda i, j: (i, j)),
            pl.BlockSpec((tm, GATE_LANES), lambda i, j: (i, 0)),
        ],
        out_shape=[
            jax.ShapeDtypeStruct((t, n), F32),
            jax.ShapeDtypeStruct((t, GATE_LANES), F32),
        ],
        scratch_shapes=[pltpu.VMEM((tm, d), BF16)],
        compiler_params=_cparams(("parallel", "arbitrary")),
        name="in_proj",
    )(x2d, g, w, wgh, wgl)


def _gates_body(zg_ref, bias_ref, col_ref, row_ref, carry_ref, *, tb):
    @pl.when(pl.program_id(1) == 0)
    def _():
        carry_ref[...] = jnp.zeros_like(carry_ref)

    v = zg_ref[0] + bias_ref[...]
    ls = jnp.minimum(v, 0.0) - jnp.log1p(jnp.exp(-jnp.abs(v)))
    r = lax.broadcasted_iota(jnp.int32, (tb, tb), 0)
    c = lax.broadcasted_iota(jnp.int32, (tb, tb), 1)
    tri = (c <= r).astype(F32)
    tri_chunk = jnp.where((r // B_CHUNK) == (c // B_CHUNK), tri, 0.0)
    csum = jnp.dot(tri, ls, precision=lax.Precision.HIGHEST, preferred_element_type=F32) + carry_ref[...]
    bcum = jnp.dot(tri_chunk, ls, precision=lax.Precision.HIGHEST, preferred_element_type=F32)
    carry_ref[...] = csum[tb - 1:tb, :]
    lane = lax.broadcasted_iota(jnp.int32, (tb, GATE_LANES), 1)
    out = jnp.where(lane < G_BI, csum, jnp.where(lane < G_BF, v, bcum))
    col_ref[0] = out
    row_ref[0] = out.T[0:GATE_ROWS, :]


def _gates(zg3, bias, layer, *, tb):
    b, s, _ = zg3.shape
    return pl.pallas_call(
        functools.partial(_gates_body, tb=tb),
        grid=(b, s // tb),
        in_specs=[
            pl.BlockSpec((1, tb, GATE_LANES), lambda i, j: (i, j, 0)),
            pl.BlockSpec((None, 1, GATE_LANES), lambda i, j: (layer, 0, 0)),
        ],
        out_specs=[
            pl.BlockSpec((1, tb, GATE_LANES), lambda i, j: (i, j, 0)),
            pl.BlockSpec((1, GATE_ROWS, tb), lambda i, j: (i, 0, j)),
        ],
        out_shape=[
            jax.ShapeDtypeStruct((b, s, GATE_LANES), F32),
            jax.ShapeDtypeStruct((b, GATE_ROWS, s), F32),
        ],
        scratch_shapes=[pltpu.VMEM((1, GATE_LANES), F32)],
        compiler_params=_cparams(("parallel", "arbitrary")),
        name="gates",
    )(zg3, bias)


def _attn_body(q_ref, k_ref, v_ref, ccol_ref, crow_ref, g_ref, o_ref, kb_ref, vb_ref, m_ref, l_ref, acc_ref, *, tq, tk):
    h = pl.program_id(1)
    qi = pl.program_id(2)
    log2e = 1.4426950408889634
    scale2 = (A_HEAD_DIM ** -0.5) * log2e

    @pl.when(qi == 0)
    def _():
        kb_ref[...] = k_ref[...].astype(BF16)
        vb_ref[...] = v_ref[...].astype(BF16)

    q = q_ref[...].astype(BF16)
    lane = lax.broadcasted_iota(jnp.int32, (tq, GATE_LANES), 1)
    ct2 = jnp.sum(jnp.where(lane == h, ccol_ref[0], 0.0), axis=-1, keepdims=True) * log2e
    m_ref[...] = jnp.full(m_ref.shape, -jnp.inf, F32)
    l_ref[...] = jnp.zeros_like(l_ref)
    acc_ref[...] = jnp.zeros_like(acc_ref)

    def logits(j):
        k0 = pl.multiple_of(j * tk, tk)
        cs2 = crow_ref[0, 0, :, pl.ds(k0, tk)] * log2e
        return _dot_nt(q, kb_ref[pl.ds(k0, tk), :]) * scale2 + (ct2 - cs2)

    def softmax(s):
        m_prev = m_ref[...]
        m_new = jnp.maximum(m_prev, jnp.max(s, axis=-1, keepdims=True))
        p = jnp.exp2(s - m_new)
        alpha = jnp.exp2(m_prev - m_new)
        l_ref[...] = alpha * l_ref[...] + jnp.sum(p, axis=-1, keepdims=True)
        m_ref[...] = m_new
        return p.astype(BF16), alpha

    def values(j, p, alpha):
        k0 = pl.multiple_of(j * tk, tk)
        acc_ref[...] = alpha * acc_ref[...] + _dot(p, vb_ref[pl.ds(k0, tk), :])

    def body(j, s_cur):
        s_next = logits(j + 1)
        p, alpha = softmax(s_cur)
        values(j, p, alpha)
        return s_next

    per_q = tq // tk
    first = qi * per_q
    s_cur = lax.fori_loop(0, first, body, logits(0))
    row = lax.broadcasted_iota(jnp.int32, (tq, tk), 0)
    col = lax.broadcasted_iota(jnp.int32, (tq, tk), 1)
    for dd in range(per_q):
        s_next = logits(first + dd + 1) if dd + 1 < per_q else None
        p, alpha = softmax(jnp.where(col + dd * tk <= row, s_cur, -jnp.inf))
        values(first + dd, p, alpha)
        s_cur = s_next
    o = acc_ref[...] / l_ref[...]
    o = o * lax.rsqrt(jnp.mean(o * o, axis=-1, keepdims=True) + EPS) * g_ref[...]
    o_ref[...] = o.astype(BF16)


def _attention(z2d, gcol, grow, g_a, layer, *, batch, seq, tq, tk):
    nq = seq // tq
    qc, kc, vc = COL_AQ // 128, COL_AK // 128, COL_AV // 128
    return pl.pallas_call(
        functools.partial(_attn_body, tq=tq, tk=tk),
        grid=(batch, A_HEADS, nq),
        in_specs=[
            pl.BlockSpec((tq, 128), lambda b, h, i: (b * nq + i, qc + h)),
            pl.BlockSpec((seq, 128), lambda b, h, i: (b, kc + h)),
            pl.BlockSpec((seq, 128), lambda b, h, i: (b, vc + h)),
            pl.BlockSpec((1, tq, GATE_LANES), lambda b, h, i: (b, i, 0)),
            pl.BlockSpec((1, 1, 1, seq), lambda b, h, i: (b, h, 0, 0)),
            pl.BlockSpec((None, 1, 128), lambda b, h, i: (layer, 0, h)),
        ],
        out_specs=pl.BlockSpec((tq, 128), lambda b, h, i: (b * nq + i, h)),
        out_shape=jax.ShapeDtypeStruct((batch * seq, A_WIDTH), BF16),
        scratch_shapes=[pltpu.VMEM((seq, 128), BF16), pltpu.VMEM((seq, 128), BF16),
                        pltpu.VMEM((tq, 1), F32), pltpu.VMEM((tq, 1), F32), pltpu.VMEM((tq, 128), F32)],
        compiler_params=_cparams(("parallel", "parallel", "arbitrary")),
        name="fox_attention",
    )(z2d, z2d, z2d, gcol, grow, g_a)


def _mlstm_body(zb_ref, gcol_ref, grow_ref, cw_ref, g_ref, y_ref, ext_ref, c_ref, m_ref, *, nb):
    k = pl.program_id(0)
    ln = B_CHUNK
    hp = B_HEAD_PAD

    @pl.when(k == 0)
    def _():
        ext_ref[:, 0:8, :] = jnp.zeros((nb, 8, 2 * B_PAD_WIDTH), F32)
        c_ref[...] = jnp.zeros_like(c_ref)
        m_ref[...] = jnp.zeros_like(m_ref)

    first_half = (k % 2) == 0
    r = lax.broadcasted_iota(jnp.int32, (ln, ln), 0)
    c = lax.broadcasted_iota(jnp.int32, (ln, ln), 1)
    causal = c <= r
    lane = lax.broadcasted_iota(jnp.int32, (ln, hp), 1)
    qscale = B_HEAD_DIM ** -0.5

    for b in range(nb):
        ext_ref[b, 8:8 + ln, :] = zb_ref[b, :, 0:2 * B_PAD_WIDTH]
        conv = cw_ref[0:1, :] * ext_ref[b, 5:5 + ln, :]
        for j in range(1, B_CONV):
            conv = conv + cw_ref[j:j + 1, :] * ext_ref[b, 5 + j:5 + j + ln, :]
        ext_ref[b, 0:8, :] = ext_ref[b, ln:ln + 8, :]
        qk = conv * _sigmoid(conv)
        grow = grow_ref[b]
        grow = jnp.where(first_half, grow[:, 0:ln], grow[:, ln:2 * ln])
        gcol = gcol_ref[b]
        for h in range(B_HEADS):
            st = b * B_HEADS + h
            qq = (qk[:, COL_BQ + h * hp:COL_BQ + (h + 1) * hp] * qscale).astype(BF16)
            kk = qk[:, COL_BK + h * hp:COL_BK + (h + 1) * hp]
            vv = zb_ref[b, :, COL_BV + h * hp:COL_BV + (h + 1) * hp]
            vv = jnp.where(lane == B_HEAD_DIM, 1.0, vv).astype(BF16)
            og = zb_ref[b, :, COL_BO + h * hp:COL_BO + (h + 1) * hp]
            bc_col = gcol[:, G_BF + h:G_BF + h + 1]
            li_col = gcol[:, G_BI + h:G_BI + h + 1]
            bc_row = grow[G_BF + h:G_BF + h + 1, :]
            li_row = grow[G_BI + h:G_BI + h + 1, :]
            m = m_ref[st, 0:1, 0:1]

            d = jnp.where(causal, bc_col - bc_row + li_row, -jnp.inf)
            inter = bc_col + m
            m_t = jnp.maximum(inter, jnp.max(d, axis=-1, keepdims=True))
            w_inter = jnp.exp(inter - m_t)
            sc = _dot_nt(qq, kk.astype(BF16)) * jnp.exp(d - m_t)
            cm = c_ref[st]
            num = w_inter * _dot(qq, cm.astype(BF16)) + _dot(sc.astype(BF16), vv)
            den = num[:, B_HEAD_DIM:B_HEAD_DIM + 1]
            hh = num / jnp.maximum(jnp.abs(den), jnp.exp(-m_t))

            b_last = bc_col[ln - 1:ln, :]
            dec = b_last - bc_col + li_col
            m_new = jnp.maximum(b_last + m, jnp.max(dec, axis=0, keepdims=True))
            a = jnp.exp(b_last + m - m_new)
            wk = (jnp.exp(dec - m_new) * kk).astype(BF16)
            c_ref[st] = a * cm + _dot_tn(wk, vv)
            m_ref[st] = jnp.broadcast_to(m_new, (8, 128))

            y = jnp.where(lane < B_HEAD_DIM, _sigmoid(og) * hh, 0.0)
            ms = jnp.sum(y * y, axis=-1, keepdims=True) * (1.0 / B_HEAD_DIM)
            y = y * lax.rsqrt(ms + EPS) * g_ref[:, h * hp:(h + 1) * hp]
            y_ref[b, :, h * hp:(h + 1) * hp] = y.astype(BF16)


def _mlstm(z3, gcol, grow, cw, g_b, layer):
    nb, seq, _ = z3.shape
    nc = seq // B_CHUNK
    return pl.pallas_call(
        functools.partial(_mlstm_body, nb=nb),
        grid=(nc,),
        in_specs=[
            pl.BlockSpec((nb, B_CHUNK, 4 * B_PAD_WIDTH), lambda k: (0, k, 0)),
            pl.BlockSpec((nb, B_CHUNK, GATE_LANES), lambda k: (0, k, 0)),
            pl.BlockSpec((nb, GATE_ROWS, 2 * B_CHUNK), lambda k: (0, 0, k // 2)),
            pl.BlockSpec((None, B_CONV, 2 * B_PAD_WIDTH), lambda k: (layer, 0, 0)),
            pl.BlockSpec((None, 1, B_PAD_WIDTH), lambda k: (layer, 0, 0)),
        ],
        out_specs=pl.BlockSpec((nb, B_CHUNK, B_PAD_WIDTH), lambda k: (0, k, 0)),
        out_shape=jax.ShapeDtypeStruct((nb, seq, B_PAD_WIDTH), BF16),
        scratch_shapes=[
            pltpu.VMEM((nb, B_CHUNK + 8, 2 * B_PAD_WIDTH), F32),
            pltpu.VMEM((nb * B_HEADS, B_HEAD_PAD, B_HEAD_PAD), F32),
            pltpu.VMEM((nb * B_HEADS, 8, 128), F32),
        ],
        compiler_params=_cparams(("arbitrary",)),
        name="mlstm",
    )(z3, gcol, grow, cw, g_b)


def _conf_body(ca_ref, cg_ref, dw_ref, db_ref, lg_ref, lb_ref, pw_ref, pb_ref, g_ref, y_ref, ext_ref, *, ts):
    halo = 32

    @pl.when(pl.program_id(1) == 0)
    def _():
        ext_ref[0:halo, :] = jnp.zeros((halo, C_WIDTH), F32)

    ext_ref[halo:halo + ts, :] = ca_ref[0] * _sigmoid(cg_ref[0])
    off = halo - (C_CONV - 1)
    acc = None
    for r in range(8):
        taps = [(m, 8 * m + r - off) for m in range(halo // 8 + 1) if 0 <= 8 * m + r - off < C_CONV]
        ln = ts if r == 0 else ts + 8
        part = None
        for m, j in taps:
            term = dw_ref[j:j + 1, :] * ext_ref[8 * m:8 * m + ln, :]
            part = term if part is None else part + term
        part = part if r == 0 else part[r:r + ts, :]
        acc = part if acc is None else acc + part
    ext_ref[0:halo, :] = ext_ref[ts:ts + halo, :]
    y = acc + db_ref[...]
    mu = jnp.mean(y, axis=-1, keepdims=True)
    yc = y - mu
    y = yc * lax.rsqrt(jnp.mean(yc * yc, axis=-1, keepdims=True) + LN_EPS)
    y = y * lg_ref[...] + lb_ref[...]
    y = y * _sigmoid(y)
    y = _dot(y.astype(BF16), pw_ref[...]) + pb_ref[...]
    gw = C_WIDTH // C_GROUPS
    for gi in range(C_GROUPS):
        yg = y[:, gi * gw:(gi + 1) * gw]
        yg = yg * lax.rsqrt(jnp.mean(yg * yg, axis=-1, keepdims=True) + EPS) * g_ref[:, gi * gw:(gi + 1) * gw]
        y_ref[0, :, gi * gw:(gi + 1) * gw] = yg.astype(BF16)


def _conformer(z3, dw, db, lg, lb, pw, pb, g_c, layer, *, ts):
    nb, seq, _ = z3.shape
    ca, cg = COL_CA // C_WIDTH, COL_CG // C_WIDTH
    vec = pl.BlockSpec((None, 1, C_WIDTH), lambda b, i: (layer, 0, 0))
    return pl.pallas_call(
        functools.partial(_conf_body, ts=ts),
        grid=(nb, seq // ts),
        in_specs=[
            pl.BlockSpec((1, ts, C_WIDTH), lambda b, i: (b, i, ca)),
            pl.BlockSpec((1, ts, C_WIDTH), lambda b, i: (b, i, cg)),
            pl.BlockSpec((None, C_CONV, C_WIDTH), lambda b, i: (layer, 0, 0)),
            vec, vec, vec,
            pl.BlockSpec((None, C_WIDTH, C_WIDTH), lambda b, i: (layer, 0, 0)),
            vec, vec,
        ],
        out_specs=pl.BlockSpec((1, ts, C_WIDTH), lambda b, i: (b, i, 0)),
        out_shape=jax.ShapeDtypeStruct((nb, seq, C_WIDTH), BF16),
        scratch_shapes=[pltpu.VMEM((ts + 32, C_WIDTH), F32)],
        compiler_params=_cparams(("parallel", "arbitrary")),
        name="conformer",
    )(z3, z3, dw, db, lg, lb, pw, pb, g_c)


def _outproj_body(x_ref, ya_ref, yb_ref, yc_ref, wa_ref, wb_ref, wc_ref, o_ref):
    o_ref[...] = (x_ref[...] + _dot(ya_ref[...], wa_ref[...]) + _dot(yb_ref[...], wb_ref[...])
                  + _dot(yc_ref[...], wc_ref[...]))


def _out_proj(x2d, ya, yb, yc, wa, wb, wc, layer, *, tm):
    t, d = x2d.shape
    once = pl.Buffered(1)
    return pl.pallas_call(
        _outproj_body,
        grid=(t // tm,),
        in_specs=[
            pl.BlockSpec((tm, d), lambda i: (i, 0)),
            pl.BlockSpec((tm, ya.shape[1]), lambda i: (i, 0)),
            pl.BlockSpec((tm, yb.shape[1]), lambda i: (i, 0)),
            pl.BlockSpec((tm, yc.shape[1]), lambda i: (i, 0)),
            pl.BlockSpec((None,) + wa.shape[1:], lambda i: (layer, 0, 0), pipeline_mode=once),
            pl.BlockSpec((None,) + wb.shape[1:], lambda i: (layer, 0, 0), pipeline_mode=once),
            pl.BlockSpec((None,) + wc.shape[1:], lambda i: (layer, 0, 0), pipeline_mode=once),
        ],
        out_specs=pl.BlockSpec((tm, d), lambda i: (i, 0)),
        out_shape=jax.ShapeDtypeStruct((t, d), F32),
        compiler_params=_cparams(("parallel",)),
        name="out_proj",
    )(x2d, ya, yb, yc, wa, wb, wc)


def _peerq_body(x_ref, g_ref, w_ref, q_ref, xn_ref):
    xf = x_ref[...]
    ms = jnp.mean(xf * xf, axis=-1, keepdims=True)
    xn = (xf * lax.rsqrt(ms + EPS) * g_ref[...]).astype(BF16)
    xn_ref[...] = xn
    q_ref[...] = _dot(xn, w_ref[...])


def _peer_q(x2d, g, w, layer, *, tm):
    t, d = x2d.shape
    n = w.shape[2]
    return pl.pallas_call(
        _peerq_body,
        grid=(t // tm,),
        in_specs=[
            pl.BlockSpec((tm, d), lambda i: (i, 0)),
            pl.BlockSpec((None, 1, d), lambda i: (layer, 0, 0)),
            pl.BlockSpec((None, d, n), lambda i: (layer, 0, 0), pipeline_mode=pl.Buffered(1)),
        ],
        out_specs=[
            pl.BlockSpec((tm, n), lambda i: (i, 0)),
            pl.BlockSpec((tm, d), lambda i: (i, 0)),
        ],
        out_shape=[
            jax.ShapeDtypeStruct((t, n), F32),
            jax.ShapeDtypeStruct((t, d), BF16),
        ],
        compiler_params=_cparams(("parallel",)),
        name="peer_q",
    )(x2d, g, w)


def _extract_top(s, rowid, n_out):
    nrow = s.shape[0]
    rank = jnp.full(s.shape, float(n_out), F32)
    vals = []
    for it in range(n_out):
        m = jnp.max(s, axis=0, keepdims=True)
        idx = jnp.min(jnp.where(s == m, rowid, float(nrow)), axis=0, keepdims=True)
        hit = rowid == idx
        vals.append(m)
        rank = jnp.where(hit, float(it), rank)
        s = jnp.where(hit, -jnp.inf, s)
    return vals, rank, rank < float(n_out)


def _take_top_distinct(s, n_out):
    for _ in range(n_out):
        m = jnp.max(s, axis=0, keepdims=True)
        s = jnp.where(s == m, -jnp.inf, s)
    taken = s == -jnp.inf
    return taken, jnp.sum(jnp.where(taken, 1.0, 0.0), axis=0, keepdims=True)


def _top_sorted(s, n_out):
    nblk = s.shape[0] // 8
    assert nblk == n_out and nblk & (nblk - 1) == 0
    lst = [s[8 * k:8 * k + 8, :] for k in range(nblk)]

    def exchange(i, l, descending):
        hi, lo = jnp.maximum(lst[i], lst[l]), jnp.minimum(lst[i], lst[l])
        lst[i], lst[l] = (hi, lo) if descending else (lo, hi)

    def merge_bitonic():
        d = nblk // 2
        while d >= 1:
            for i in range(nblk):
                if (i // d) % 2 == 0:
                    exchange(i, i + d, True)
            d //= 2

    k = 2
    while k <= nblk:
        j = k // 2
        while j >= 1:
            for i in range(nblk):
                if i ^ j > i:
                    exchange(i, i ^ j, (i & k) == 0)
            j //= 2
        k *= 2
    for shift in (4, 2, 1):
        other = [pltpu.roll(x, shift, 0) for x in lst]
        lst = [jnp.maximum(lst[i], other[nblk - 1 - i]) for i in range(nblk)]
        merge_bitonic()
    return jnp.concatenate([x[0:1, :] for x in lst], axis=0)


_CAND_SHORT = 8


def _topk_body(q_ref, keys_ref, c1_ref, w1_ref, r2_ref, e2_ref, *, tt):
    rowid = lax.broadcasted_iota(jnp.int32, (N_KEYS, tt), 0).astype(F32)
    rowid2 = lax.broadcasted_iota(jnp.int32, (P_TOPK * P_TOPK, tt), 0).astype(F32)

    def pair_grid(x1, x2, op):
        parts = [op(x1[0:1, :], x2)]
        parts += [op(x1[a:a + 1, :], x2[0:_CAND_SHORT, :]) for a in range(1, _CAND_SHORT)]
        parts += [op(x1[_CAND_SHORT:, :], x2[0:1, :])]
        return jnp.concatenate(parts, axis=0)

    def finish(h, scores, v1, v2, is_rank, rank2, counts, zsum):
        cnt = jnp.zeros((N_KEYS, tt), F32)
        for a in range(P_TOPK):
            cnt = jnp.where(is_rank(a), counts[a], cnt)
        c1_ref[h] = cnt
        w1_ref[h] = jnp.exp(scores[0] - v1[0:1, :]) / zsum
        r2_ref[h] = rank2.astype(BF16)
        e2_ref[h] = jnp.exp(scores[1] - v2[0:1, :]).astype(BF16)

    def per_head(h, carry):
        scores = []
        for c in range(2):
            kh, kl = _split_bf16(keys_ref[2 * h + c])
            col = pl.multiple_of((2 * h + c) * P_HALF, P_HALF)
            qh, ql = _split_bf16(q_ref[:, pl.ds(col, P_HALF)])
            scores.append(_dot3(kh, kl, qh, ql, _dot_nt))

        tops, clean = [], None
        for c in range(2):
            v = _top_sorted(scores[c], P_TOPK)
            tops.append(v)
            repeats = jnp.max(jnp.where(v[0:P_TOPK - 1, :] == v[1:P_TOPK, :], 1.0, 0.0), axis=0, keepdims=True)
            reach = jnp.sum(jnp.where(scores[c] >= v[P_TOPK - 1:P_TOPK, :], 1.0, 0.0), axis=0, keepdims=True)
            ok = (repeats < 0.5) & (reach == float(P_TOPK))
            clean = ok if clean is None else (clean & ok)
        v1, v2 = tops
        rank2 = jnp.full((N_KEYS, tt), float(P_TOPK), F32)
        for a in range(P_TOPK):
            rank2 = jnp.where(scores[1] == v2[a:a + 1, :], float(a), rank2)
        cand = pair_grid(v1, v2, lambda x, y: x + y)
        picked, lost = _take_top_distinct(cand, P_TOPK)
        clean = clean & (lost == float(P_TOPK))
        ev1 = jnp.exp(v1 - v1[0:1, :])
        ev2 = jnp.exp(v2 - v2[0:1, :])
        prod = pair_grid(ev1, ev2, lambda x, y: x * y)
        zsum = jnp.sum(jnp.where(picked, prod, 0.0), axis=0, keepdims=True)
        npick = jnp.where(picked, 1.0, 0.0)
        counts = [jnp.sum(npick[0:P_TOPK, :], axis=0, keepdims=True)]
        for a in range(1, _CAND_SHORT):
            r0 = P_TOPK + (a - 1) * _CAND_SHORT
            counts.append(jnp.sum(npick[r0:r0 + _CAND_SHORT, :], axis=0, keepdims=True))
        r0 = P_TOPK + (_CAND_SHORT - 1) * _CAND_SHORT
        counts += [npick[r0 + a:r0 + a + 1, :] for a in range(P_TOPK - _CAND_SHORT)]
        finish(h, scores, v1, v2, lambda a: scores[0] == v1[a:a + 1, :], rank2, counts, zsum)

        @pl.when(jnp.min(jnp.where(clean, 1.0, 0.0)) < 0.5)
        def _():
            tops, ranks = [], []
            for c in range(2):
                vals, rank, _ = _extract_top(scores[c], rowid, P_TOPK)
                tops.append(jnp.concatenate(vals, axis=0))
                ranks.append(rank)
            v1, v2 = tops
            cand = jnp.concatenate([v1[a:a + 1, :] + v2 for a in range(P_TOPK)], axis=0)
            _, _, picked = _extract_top(cand, rowid2, P_TOPK)
            ev1 = jnp.exp(v1 - v1[0:1, :])
            ev2 = jnp.exp(v2 - v2[0:1, :])
            prod = jnp.concatenate([ev1[a:a + 1, :] * ev2 for a in range(P_TOPK)], axis=0)
            zsum = jnp.sum(jnp.where(picked, prod, 0.0), axis=0, keepdims=True)
            npick = jnp.where(picked, 1.0, 0.0)
            counts = [jnp.sum(npick[a * P_TOPK:(a + 1) * P_TOPK, :], axis=0, keepdims=True) for a in range(P_TOPK)]
            finish(h, scores, v1, v2, lambda a: ranks[0] == float(a), ranks[1], counts, zsum)

        return carry

    lax.fori_loop(0, P_HEADS, per_head, 0)


def _peer_topk(q, keys16, layer, *, tt):
    t = q.shape[0]
    st_shape = jax.ShapeDtypeStruct((P_HEADS, N_KEYS, t), F32)
    st_shape_b = jax.ShapeDtypeStruct((P_HEADS, N_KEYS, t), BF16)
    st_spec = pl.BlockSpec((P_HEADS, N_KEYS, tt), lambda i: (0, 0, i))
    return pl.pallas_call(
        functools.partial(_topk_body, tt=tt),
        grid=(t // tt,),
        in_specs=[
            pl.BlockSpec((tt, q.shape[1]), lambda i: (i, 0)),
            pl.BlockSpec((None,) + keys16.shape[1:], lambda i: (layer, 0, 0, 0)),
        ],
        out_specs=[st_spec] * 4,
        out_shape=[st_shape, st_shape, st_shape_b, st_shape_b],
        compiler_params=_cparams(("parallel",)),
        name="peer_topk",
    )(q, keys16)


def _peer_dense_body(xn_ref, u_ref, v_ref, c1_ref, w1_ref, r2_ref, e2_ref, x_ref, g_ref, o_ref, c1s_ref, w1s_ref, *hg_refs,
                     tm, te, tc, rb, final):
    e = pl.program_id(1)

    @pl.when(e == 0)
    def _():
        o_ref[...] = x_ref[...]

    ub = u_ref[...].astype(BF16)
    vb = v_ref[...].astype(BF16)

    ng = te // N_KEYS
    nchunk = tm // tc
    which = e % (8 // ng)

    for src, dst in ((c1_ref, c1s_ref), (w1_ref, w1s_ref)):
        rows = src[:, 0:ng, :]
        for g in range(1, 8 // ng):
            rows = jnp.where(which == g, src[:, g * ng:(g + 1) * ng, :], rows)
        dst[:, 0:ng, :] = rows

    def bf16_row(ref, h, ii, t0):
        return ref[h, ii:ii + 1, t0:t0 + tc].astype(BF16)

    def stage_a(ci):
        t0 = ci * tc
        ht = _dot_nt(ub, xn_ref[t0:t0 + tc, :])
        return ((0.5 * ht) * (1.0 + lax.erf(ht * (2.0 ** -0.5)))).astype(BF16)

    def stage_b(ci, act):
        t0 = ci * tc
        for ii in range(ng):
            rows = [(bf16_row(c1s_ref, h, ii, t0), bf16_row(w1s_ref, h, ii, t0)) for h in range(P_HEADS)]
            for j0 in range(0, N_KEYS, rb):
                r0 = ii * N_KEYS + j0
                gate = jnp.zeros((rb, tc), BF16)
                for h in range(P_HEADS):
                    cnt, w1 = rows[h]
                    gate = gate + jnp.where(r2_ref[h, j0:j0 + rb, t0:t0 + tc] < cnt,
                                            e2_ref[h, j0:j0 + rb, t0:t0 + tc] * w1, jnp.zeros((), BF16))
                hg_refs[ci][r0:r0 + rb, :] = act[r0:r0 + rb, :] * gate

    def stage_c(ci):
        t0 = ci * tc
        o_ref[t0:t0 + tc, :] += _dot_tn(hg_refs[ci][...], vb)

    ahead = 2
    hts = {ci: stage_a(ci) for ci in range(min(ahead, nchunk))}
    for ci in range(nchunk):
        if ci + ahead < nchunk:
            hts[ci + ahead] = stage_a(ci + ahead)
        stage_b(ci, hts.pop(ci))
        stage_c(ci)

    if final:
        @pl.when(e == pl.num_programs(1) - 1)
        def _():
            y = o_ref[...]
            o_ref[...] = y * lax.rsqrt(jnp.mean(y * y, axis=-1, keepdims=True) + EPS) * g_ref[...]


def _peer_dense(xn, u, v, layer, c1, w1, r2, e2, x2d, g, *, tm, te, tc, rb, final):
    t, d = xn.shape
    ne = u.shape[1]
    ng = te // N_KEYS
    once = pl.Buffered(1)
    row_spec = pl.BlockSpec((P_HEADS, 8, tm), lambda i, e: (0, e // (8 // ng), i))
    full_spec = pl.BlockSpec((P_HEADS, N_KEYS, tm), lambda i, e: (0, 0, i), pipeline_mode=once)
    tok_spec = pl.BlockSpec((tm, d), lambda i, e: (i, 0), pipeline_mode=once)
    return pl.pallas_call(
        functools.partial(_peer_dense_body, tm=tm, te=te, tc=tc, rb=rb, final=final),
        grid=(t // tm, ne // te),
        in_specs=[
            tok_spec,
            pl.BlockSpec((None, te, d), lambda i, e: (layer, e, 0)),
            pl.BlockSpec((None, te, d), lambda i, e: (layer, e, 0)),
            row_spec, row_spec, full_spec, full_spec,
            tok_spec,
            pl.BlockSpec((1, d), lambda i, e: (0, 0)),
        ],
        out_specs=pl.BlockSpec((tm, d), lambda i, e: (i, 0), pipeline_mode=once),
        out_shape=jax.ShapeDtypeStruct((t, d), F32),
        scratch_shapes=[pltpu.VMEM((P_HEADS, 8, tm), F32), pltpu.VMEM((P_HEADS, 8, tm), F32)]
        + [pltpu.VMEM((te, tc), BF16)] * (tm // tc),
        compiler_params=_cparams(("parallel", "arbitrary")),
        name="peer_dense",
    )(xn, u, v, c1, w1, r2, e2, x2d, g)


def _pad_heads(w):
    lead = w.shape[:-1]
    w = w.reshape(lead + (B_HEADS, B_HEAD_DIM))
    w = jnp.pad(w, [(0, 0)] * len(lead) + [(0, 0), (0, B_HEAD_PAD - B_HEAD_DIM)])
    return w.reshape(lead + (B_PAD_WIDTH,))


def _regroup_w_in(w):
    sizes = (A_WIDTH, A_WIDTH, A_WIDTH, A_HEADS, 2 * B_WIDTH, B_WIDTH, B_WIDTH, B_HEADS, B_HEADS, C_WIDTH, C_WIDTH)
    parts, p = [], 0
    for sz in sizes:
        parts.append(w[..., p:p + sz])
        p += sz
    aq, ak, av, af, bqk, bv, bo, bi, bf, ca, cg = parts
    lead = w.shape[:-1]
    main = jnp.concatenate(
        [_pad_heads(bqk[..., :B_WIDTH]), _pad_heads(bqk[..., B_WIDTH:]), _pad_heads(bv), _pad_heads(bo),
         ca, cg, aq, ak, av, jnp.zeros(lead + (Z_WIDTH - Z_USED,), w.dtype)], axis=-1)
    gates = jnp.concatenate([af, bi, bf, jnp.zeros(lead + (GATE_LANES - A_HEADS - 2 * B_HEADS,), w.dtype)], axis=-1)
    return main.astype(BF16), gates


def _tile_sizes(t, seq):
    return dict(
        tm_in=min(1024, t), tn_in=1280,
        tb_gate=min(256, seq),
        tq=min(512, seq), tk=min(512, seq),
        ts_conf=min(256, seq),
        tm_out=min(512, t),
        tm_q=min(512, t),
        tt=min(256, t),
        tm_pd=min(1024, t), te=512, tc_pd=min(256, t), rb_pd=32,
    )


def kernel(x, w_in, fox_f_bias, mlstm_conv_w, mlstm_i_bias, mlstm_f_bias, conf_dw_w, conf_dw_b, conf_ln_g, conf_ln_b, conf_pw_w, conf_pw_b, out_norm_g, w_out, norm1_g, norm2_g, peer_wq, peer_keys, peer_u, peer_v, final_g):
    batch, seq, d = x.shape
    t = batch * seq
    depth = w_in.shape[0]
    ts = _tile_sizes(t, seq)
    xf = x.reshape(t, d)
    rows = lambda v: v[:, None, :]

    w_main, w_gate = _regroup_w_in(w_in)
    wgh, wgl = _split_bf16(w_gate)
    gate_bias = rows(jnp.concatenate([fox_f_bias, mlstm_i_bias, mlstm_f_bias,
                                      jnp.zeros((depth, GATE_LANES - A_HEADS - 2 * B_HEADS), F32)], axis=-1))
    g_a = rows(out_norm_g[:, :A_WIDTH])
    g_b = rows(_pad_heads(out_norm_g[:, A_WIDTH:A_WIDTH + B_WIDTH]))
    g_c = rows(out_norm_g[:, A_WIDTH + B_WIDTH:])
    cw = jnp.concatenate([_pad_heads(mlstm_conv_w[..., :B_WIDTH]), _pad_heads(mlstm_conv_w[..., B_WIDTH:])], axis=-1)
    wa = w_out[:, :A_WIDTH].astype(BF16)
    wb = jnp.pad(w_out[:, A_WIDTH:A_WIDTH + B_WIDTH].reshape(depth, B_HEADS, B_HEAD_DIM, d),
                 ((0, 0), (0, 0), (0, B_HEAD_PAD - B_HEAD_DIM), (0, 0))).reshape(depth, B_PAD_WIDTH, d).astype(BF16)
    wc = w_out[:, A_WIDTH + B_WIDTH:].astype(BF16)
    pw = conf_pw_w.astype(BF16)
    wq = peer_wq.astype(BF16)
    keys16 = peer_keys.reshape(depth, 2 * P_HEADS, N_KEYS, P_HALF)
    n1, n2 = rows(norm1_g), rows(norm2_g)
    db, lg, lb, pb = rows(conf_dw_b), rows(conf_ln_g), rows(conf_ln_b), rows(conf_pw_b)

    for l in range(depth):
        z, zg = _in_proj(xf, n1, w_main, wgh, wgl, l, tm=ts["tm_in"], tn=ts["tn_in"])
        gcol, grow = _gates(zg.reshape(batch, seq, GATE_LANES), gate_bias, l, tb=ts["tb_gate"])
        z3 = z.reshape(batch, seq, Z_WIDTH)
        ya = _attention(z, gcol, grow.reshape(batch, GATE_ROWS, 1, seq), g_a, l, batch=batch, seq=seq, tq=ts["tq"], tk=ts["tk"])
        yb = _mlstm(z3, gcol, grow, cw, g_b, l)
        yc = _conformer(z3, conf_dw_w, db, lg, lb, pw, pb, g_c, l, ts=ts["ts_conf"])
        xf = _out_proj(xf, ya, yb.reshape(t, B_PAD_WIDTH), yc.reshape(t, C_WIDTH), wa, wb, wc, l, tm=ts["tm_out"])

        q, xn = _peer_q(xf, n2, wq, l, tm=ts["tm_q"])
        c1, w1, r2, e2 = _peer_topk(q, keys16, l, tt=ts["tt"])
        xf = _peer_dense(xn, peer_u, peer_v, l, c1, w1, r2, e2, xf, final_g.reshape(1, d),
                         tm=ts["tm_pd"], te=ts["te"], tc=ts["tc_pd"], rb=ts["rb_pd"], final=(l == depth - 1))

    return xf.reshape(batch, seq, d)
```

```python
import functools

import jax
import jax.numpy as jnp
from jax import lax
from jax.experimental import pallas as pl
from jax.experimental.pallas import tpu as pltpu

F32 = jnp.float32
BF16 = jnp.bfloat16

EPS = 1e-6
LN_EPS = 1e-5

A_HEADS = 6
A_HEAD_DIM = 128
A_WIDTH = A_HEADS * A_HEAD_DIM
B_HEADS = 4
B_HEAD_DIM = 192
B_HEAD_PAD = 256
B_WIDTH = B_HEADS * B_HEAD_DIM
B_PAD_WIDTH = B_HEADS * B_HEAD_PAD
B_CHUNK = 64
B_CONV = 4
C_GROUPS = 4
C_WIDTH = 512
C_CONV = 31
P_HEADS = 8
P_HALF = 128
N_KEYS = 128
P_TOPK = 16
GATE_LANES = 128

COL_BQ = 0
COL_BK = B_PAD_WIDTH
COL_BV = 2 * B_PAD_WIDTH
COL_BO = 3 * B_PAD_WIDTH
COL_CA = 4 * B_PAD_WIDTH
COL_CG = COL_CA + C_WIDTH
COL_AQ = COL_CG + C_WIDTH
COL_AK = COL_AQ + A_WIDTH
COL_AV = COL_AK + A_WIDTH
Z_USED = COL_AV + A_WIDTH
Z_WIDTH = 7680
G_BI = A_HEADS
G_BF = A_HEADS + B_HEADS
GATE_ROWS = 16

VMEM_LIMIT = 60 * 1024 * 1024


def _cparams(sem):
    return pltpu.CompilerParams(dimension_semantics=sem, vmem_limit_bytes=VMEM_LIMIT)


def _split_bf16(a):
    hi = a.astype(BF16)
    lo = (a - hi.astype(F32)).astype(BF16)
    return hi, lo


def _dot(a, b):
    return jnp.dot(a, b, preferred_element_type=F32)


def _dot_nt(a, b):
    return lax.dot_general(a, b, (((1,), (1,)), ((), ())), preferred_element_type=F32)


def _dot_tn(a, b):
    return lax.dot_general(a, b, (((0,), (0,)), ((), ())), preferred_element_type=F32)


def _dot3(ah, al, bh, bl, dot):
    return dot(ah, bh) + dot(ah, bl) + dot(al, bh)


def _sigmoid(x):
    return 1.0 / (1.0 + jnp.exp(-x))


def _inproj_body(x_ref, g_ref, w_ref, wgh_ref, wgl_ref, z_ref, zg_ref, xn_ref):
    @pl.when(pl.program_id(1) == 0)
    def _():
        xf = x_ref[...]
        ms = jnp.mean(xf * xf, axis=-1, keepdims=True)
        xn = xf * lax.rsqrt(ms + EPS) * g_ref[...]
        hi, lo = _split_bf16(xn)
        xn_ref[...] = hi
        zg_ref[...] = _dot3(hi, lo, wgh_ref[...], wgl_ref[...], _dot)

    z_ref[...] = _dot(xn_ref[...], w_ref[...]).astype(BF16)


def _in_proj(x2d, g, w, wgh, wgl, layer, *, tm, tn):
    t, d = x2d.shape
    n = w.shape[2]
    return pl.pallas_call(
        _inproj_body,
        grid=(t // tm, n // tn),
        in_specs=[
            pl.BlockSpec((tm, d), lambda i, j: (i, 0), pipeline_mode=pl.Buffered(1)),
            pl.BlockSpec((None, 1, d), lambda i, j: (layer, 0, 0)),
            pl.BlockSpec((None, d, tn), lambda i, j: (layer, 0, j)),
            pl.BlockSpec((None, d, GATE_LANES), lambda i, j: (layer, 0, 0)),
            pl.BlockSpec((None, d, GATE_LANES), lambda i, j: (layer, 0, 0)),
        ],
        out_specs=[
            pl.BlockSpec((tm, tn), lambda i, j: (i, j)),
            pl.BlockSpec((tm, GATE_LANES), lambda i, j: (i, 0)),
        ],
        out_shape=[
            jax.ShapeDtypeStruct((t, n), BF16),
            jax.ShapeDtypeStruct((t, GATE_LANES), F32),
        ],
        scratch_shapes=[pltpu.VMEM((tm, d), BF16)],
        compiler_params=_cparams(("parallel", "arbitrary")),
        name="in_proj",
    )(x2d, g, w, wgh, wgl)


def _gates_body(zg_ref, bias_ref, col_ref, row_ref, carry_ref, *, tb):
    @pl.when(pl.program_id(1) == 0)
    def _():
        carry_ref[...] = jnp.zeros_like(carry_ref)

    v = zg_ref[0] + bias_ref[...]
    ls = jnp.minimum(v, 0.0) - jnp.log1p(jnp.exp(-jnp.abs(v)))
    r = lax.broadcasted_iota(jnp.int32, (tb, tb), 0)
    c = lax.broadcasted_iota(jnp.int32, (tb, tb), 1)
    tri = (c <= r).astype(F32)
    tri_chunk = jnp.where((r // B_CHUNK) == (c // B_CHUNK), tri, 0.0)
    csum = jnp.dot(tri, ls, precision=lax.Precision.HIGHEST, preferred_element_type=F32) + carry_ref[...]
    bcum = jnp.dot(tri_chunk, ls, precision=lax.Precision.HIGHEST, preferred_element_type=F32)
    carry_ref[...] = csum[tb - 1:tb, :]
    lane = lax.broadcasted_iota(jnp.int32, (tb, GATE_LANES), 1)
    out = jnp.where(lane < G_BI, csum, jnp.where(lane < G_BF, v, bcum))
    col_ref[0] = out
    row_ref[0] = out.T[0:GATE_ROWS, :]


def _gates(zg3, bias, layer, *, tb):
    b, s, _ = zg3.shape
    return pl.pallas_call(
        functools.partial(_gates_body, tb=tb),
        grid=(b, s // tb),
        in_specs=[
            pl.BlockSpec((1, tb, GATE_LANES), lambda i, j: (i, j, 0)),
            pl.BlockSpec((None, 1, GATE_LANES), lambda i, j: (layer, 0, 0)),
        ],
        out_specs=[
            pl.BlockSpec((1, tb, GATE_LANES), lambda i, j: (i, j, 0)),
            pl.BlockSpec((1, GATE_ROWS, tb), lambda i, j: (i, 0, j)),
        ],
        out_shape=[
            jax.ShapeDtypeStruct((b, s, GATE_LANES), F32),
            jax.ShapeDtypeStruct((b, GATE_ROWS, s), F32),
        ],
        scratch_shapes=[pltpu.VMEM((1, GATE_LANES), F32)],
        compiler_params=_cparams(("parallel", "arbitrary")),
        name="gates",
    )(zg3, bias)


def _attn_body(q_ref, k_ref, v_ref, ccol_ref, crow_ref, g_ref, o_ref, kb_ref, vb_ref, m_ref, l_ref, acc_ref, *, tq, tk):
    h = pl.program_id(1)
    qi = pl.program_id(2)
    log2e = 1.4426950408889634
    scale2 = (A_HEAD_DIM ** -0.5) * log2e

    @pl.when(qi == 0)
    def _():
        kb_ref[...] = k_ref[...].astype(BF16)
        vb_ref[...] = v_ref[...].astype(BF16)

    q = q_ref[...].astype(BF16)
    lane = lax.broadcasted_iota(jnp.int32, (tq, GATE_LANES), 1)
    ct2 = jnp.sum(jnp.where(lane == h, ccol_ref[0], 0.0), axis=-1, keepdims=True) * log2e
    m_ref[...] = jnp.full(m_ref.shape, -jnp.inf, F32)
    l_ref[...] = jnp.zeros_like(l_ref)
    acc_ref[...] = jnp.zeros_like(acc_ref)

    def logits(j):
        k0 = pl.multiple_of(j * tk, tk)
        cs2 = crow_ref[0, 0, :, pl.ds(k0, tk)] * log2e
        return _dot_nt(q, kb_ref[pl.ds(k0, tk), :]) * scale2 + (ct2 - cs2)

    def softmax(s):
        m_prev = m_ref[...]
        m_new = jnp.maximum(m_prev, jnp.max(s, axis=-1, keepdims=True))
        p = jnp.exp2(s - m_new)
        alpha = jnp.exp2(m_prev - m_new)
        l_ref[...] = alpha * l_ref[...] + jnp.sum(p, axis=-1, keepdims=True)
        m_ref[...] = m_new
        return p.astype(BF16), alpha

    def values(j, p, alpha):
        k0 = pl.multiple_of(j * tk, tk)
        acc_ref[...] = alpha * acc_ref[...] + _dot(p, vb_ref[pl.ds(k0, tk), :])

    def body(j, s_cur):
        s_next = logits(j + 1)
        p, alpha = softmax(s_cur)
        values(j, p, alpha)
        return s_next

    per_q = tq // tk
    first = qi * per_q
    s_cur = lax.fori_loop(0, first, body, logits(0))
    row = lax.broadcasted_iota(jnp.int32, (tq, tk), 0)
    col = lax.broadcasted_iota(jnp.int32, (tq, tk), 1)
    for dd in range(per_q):
        s_next = logits(first + dd + 1) if dd + 1 < per_q else None
        p, alpha = softmax(jnp.where(col + dd * tk <= row, s_cur, -jnp.inf))
        values(first + dd, p, alpha)
        s_cur = s_next
    o = acc_ref[...] / l_ref[...]
    o = o * lax.rsqrt(jnp.mean(o * o, axis=-1, keepdims=True) + EPS) * g_ref[...]
    o_ref[...] = o.astype(BF16)


def _attention(z2d, gcol, grow, g_a, layer, *, batch, seq, tq, tk):
    nq = seq // tq
    qc, kc, vc = COL_AQ // 128, COL_AK // 128, COL_AV // 128
    return pl.pallas_call(
        functools.partial(_attn_body, tq=tq, tk=tk),
        grid=(batch, A_HEADS, nq),
        in_specs=[
            pl.BlockSpec((tq, 128), lambda b, h, i: (b * nq + i, qc + h)),
            pl.BlockSpec((seq, 128), lambda b, h, i: (b, kc + h)),
            pl.BlockSpec((seq, 128), lambda b, h, i: (b, vc + h)),
            pl.BlockSpec((1, tq, GATE_LANES), lambda b, h, i: (b, i, 0)),
            pl.BlockSpec((1, 1, 1, seq), lambda b, h, i: (b, h, 0, 0)),
            pl.BlockSpec((None, 1, 128), lambda b, h, i: (layer, 0, h)),
        ],
        out_specs=pl.BlockSpec((tq, 128), lambda b, h, i: (b * nq + i, h)),
        out_shape=jax.ShapeDtypeStruct((batch * seq, A_WIDTH), BF16),
        scratch_shapes=[pltpu.VMEM((seq, 128), BF16), pltpu.VMEM((seq, 128), BF16),
                        pltpu.VMEM((tq, 1), F32), pltpu.VMEM((tq, 1), F32), pltpu.VMEM((tq, 128), F32)],
        compiler_params=_cparams(("parallel", "parallel", "arbitrary")),
        name="fox_attention",
    )(z2d, z2d, z2d, gcol, grow, g_a)


def _mlstm_body(zb_ref, gcol_ref, grow_ref, cw_ref, g_ref, y_ref, ext_ref, c_ref, m_ref, *, nb):
    k = pl.program_id(0)
    ln = B_CHUNK
    hp = B_HEAD_PAD

    @pl.when(k == 0)
    def _():
        ext_ref[:, 0:8, :] = jnp.zeros((nb, 8, 2 * B_PAD_WIDTH), F32)
        c_ref[...] = jnp.zeros_like(c_ref)
        m_ref[...] = jnp.zeros_like(m_ref)

    first_half = (k % 2) == 0
    r = lax.broadcasted_iota(jnp.int32, (ln, ln), 0)
    c = lax.broadcasted_iota(jnp.int32, (ln, ln), 1)
    causal = c <= r
    lane = lax.broadcasted_iota(jnp.int32, (ln, hp), 1)
    qscale = B_HEAD_DIM ** -0.5

    for b in range(nb):
        ext_ref[b, 8:8 + ln, :] = zb_ref[b, :, 0:2 * B_PAD_WIDTH].astype(F32)
        conv = cw_ref[0:1, :] * ext_ref[b, 5:5 + ln, :]
        for j in range(1, B_CONV):
            conv = conv + cw_ref[j:j + 1, :] * ext_ref[b, 5 + j:5 + j + ln, :]
        ext_ref[b, 0:8, :] = ext_ref[b, ln:ln + 8, :]
        qk = conv * _sigmoid(conv)
        grow = grow_ref[b]
        grow = jnp.where(first_half, grow[:, 0:ln], grow[:, ln:2 * ln])
        gcol = gcol_ref[b]
        for h in range(B_HEADS):
            st = b * B_HEADS + h
            qq = (qk[:, COL_BQ + h * hp:COL_BQ + (h + 1) * hp] * qscale).astype(BF16)
            kk = qk[:, COL_BK + h * hp:COL_BK + (h + 1) * hp]
            vv = zb_ref[b, :, COL_BV + h * hp:COL_BV + (h + 1) * hp]
            vv = jnp.where(lane == B_HEAD_DIM, jnp.ones((), BF16), vv)
            og = zb_ref[b, :, COL_BO + h * hp:COL_BO + (h + 1) * hp].astype(F32)
            bc_col = gcol[:, G_BF + h:G_BF + h + 1]
            li_col = gcol[:, G_BI + h:G_BI + h + 1]
            bc_row = grow[G_BF + h:G_BF + h + 1, :]
            li_row = grow[G_BI + h:G_BI + h + 1, :]
            m = m_ref[st, 0:1, 0:1]

            d = jnp.where(causal, bc_col - bc_row + li_row, -jnp.inf)
            inter = bc_col + m
            m_t = jnp.maximum(inter, jnp.max(d, axis=-1, keepdims=True))
            w_inter = jnp.exp(inter - m_t)
            sc = _dot_nt(qq, kk.astype(BF16)) * jnp.exp(d - m_t)
            cm = c_ref[st]
            num = w_inter * _dot(qq, cm.astype(BF16)) + _dot(sc.astype(BF16), vv)
            den = num[:, B_HEAD_DIM:B_HEAD_DIM + 1]
            hh = num / jnp.maximum(jnp.abs(den), jnp.exp(-m_t))

            b_last = bc_col[ln - 1:ln, :]
            dec = b_last - bc_col + li_col
            m_new = jnp.maximum(b_last + m, jnp.max(dec, axis=0, keepdims=True))
            a = jnp.exp(b_last + m - m_new)
            wk = (jnp.exp(dec - m_new) * kk).astype(BF16)
            c_ref[st] = a * cm + _dot_tn(wk, vv)
            m_ref[st] = jnp.broadcast_to(m_new, (8, 128))

            y = jnp.where(lane < B_HEAD_DIM, _sigmoid(og) * hh, 0.0)
            ms = jnp.sum(y * y, axis=-1, keepdims=True) * (1.0 / B_HEAD_DIM)
            y = y * lax.rsqrt(ms + EPS) * g_ref[:, h * hp:(h + 1) * hp]
            y_ref[b, :, h * hp:(h + 1) * hp] = y.astype(BF16)


def _mlstm(z3, gcol, grow, cw, g_b, layer):
    nb, seq, _ = z3.shape
    nc = seq // B_CHUNK
    return pl.pallas_call(
        functools.partial(_mlstm_body, nb=nb),
        grid=(nc,),
        in_specs=[
            pl.BlockSpec((nb, B_CHUNK, 4 * B_PAD_WIDTH), lambda k: (0, k, 0)),
            pl.BlockSpec((nb, B_CHUNK, GATE_LANES), lambda k: (0, k, 0)),
            pl.BlockSpec((nb, GATE_ROWS, 2 * B_CHUNK), lambda k: (0, 0, k // 2)),
            pl.BlockSpec((None, B_CONV, 2 * B_PAD_WIDTH), lambda k: (layer, 0, 0)),
            pl.BlockSpec((None, 1, B_PAD_WIDTH), lambda k: (layer, 0, 0)),
        ],
        out_specs=pl.BlockSpec((nb, B_CHUNK, B_PAD_WIDTH), lambda k: (0, k, 0)),
        out_shape=jax.ShapeDtypeStruct((nb, seq, B_PAD_WIDTH), BF16),
        scratch_shapes=[
            pltpu.VMEM((nb, B_CHUNK + 8, 2 * B_PAD_WIDTH), F32),
            pltpu.VMEM((nb * B_HEADS, B_HEAD_PAD, B_HEAD_PAD), F32),
            pltpu.VMEM((nb * B_HEADS, 8, 128), F32),
        ],
        compiler_params=_cparams(("arbitrary",)),
        name="mlstm",
    )(z3, gcol, grow, cw, g_b)


def _conf_body(ca_ref, cg_ref, dw_ref, db_ref, lg_ref, lb_ref, pw_ref, pb_ref, g_ref, y_ref, ext_ref, *, ts):
    halo = 32

    @pl.when(pl.program_id(1) == 0)
    def _():
        ext_ref[0:halo, :] = jnp.zeros((halo, C_WIDTH), F32)

    ext_ref[halo:halo + ts, :] = ca_ref[0].astype(F32) * _sigmoid(cg_ref[0].astype(F32))
    off = halo - (C_CONV - 1)
    acc = None
    for r in range(8):
        taps = [(m, 8 * m + r - off) for m in range(halo // 8 + 1) if 0 <= 8 * m + r - off < C_CONV]
        ln = ts if r == 0 else ts + 8
        part = None
        for m, j in taps:
            term = dw_ref[j:j + 1, :] * ext_ref[8 * m:8 * m + ln, :]
            part = term if part is None else part + term
        part = part if r == 0 else part[r:r + ts, :]
        acc = part if acc is None else acc + part
    ext_ref[0:halo, :] = ext_ref[ts:ts + halo, :]
    y = acc + db_ref[...]
    mu = jnp.mean(y, axis=-1, keepdims=True)
    yc = y - mu
    y = yc * lax.rsqrt(jnp.mean(yc * yc, axis=-1, keepdims=True) + LN_EPS)
    y = y * lg_ref[...] + lb_ref[...]
    y = y * _sigmoid(y)
    y = _dot(y.astype(BF16), pw_ref[...]) + pb_ref[...]
    gw = C_WIDTH // C_GROUPS
    for gi in range(C_GROUPS):
        yg = y[:, gi * gw:(gi + 1) * gw]
        yg = yg * lax.rsqrt(jnp.mean(yg * yg, axis=-1, keepdims=True) + EPS) * g_ref[:, gi * gw:(gi + 1) * gw]
        y_ref[0, :, gi * gw:(gi + 1) * gw] = yg.astype(BF16)


def _conformer(z3, dw, db, lg, lb, pw, pb, g_c, layer, *, ts):
    nb, seq, _ = z3.shape
    ca, cg = COL_CA // C_WIDTH, COL_CG // C_WIDTH
    vec = pl.BlockSpec((None, 1, C_WIDTH), lambda b, i: (layer, 0, 0))
    return pl.pallas_call(
        functools.partial(_conf_body, ts=ts),
        grid=(nb, seq // ts),
        in_specs=[
            pl.BlockSpec((1, ts, C_WIDTH), lambda b, i: (b, i, ca)),
            pl.BlockSpec((1, ts, C_WIDTH), lambda b, i: (b, i, cg)),
            pl.BlockSpec((None, C_CONV, C_WIDTH), lambda b, i: (layer, 0, 0)),
            vec, vec, vec,
            pl.BlockSpec((None, C_WIDTH, C_WIDTH), lambda b, i: (layer, 0, 0)),
            vec, vec,
        ],
        out_specs=pl.BlockSpec((1, ts, C_WIDTH), lambda b, i: (b, i, 0)),
        out_shape=jax.ShapeDtypeStruct((nb, seq, C_WIDTH), BF16),
        scratch_shapes=[pltpu.VMEM((ts + 32, C_WIDTH), F32)],
        compiler_params=_cparams(("parallel", "arbitrary")),
        name="conformer",
    )(z3, z3, dw, db, lg, lb, pw, pb, g_c)


def _outproj_body(x_ref, ya_ref, yb_ref, yc_ref, wa_ref, wb_ref, wc_ref, o_ref):
    o_ref[...] = (x_ref[...] + _dot(ya_ref[...], wa_ref[...]) + _dot(yb_ref[...], wb_ref[...])
                  + _dot(yc_ref[...], wc_ref[...]))


def _out_proj(x2d, ya, yb, yc, wa, wb, wc, layer, *, tm):
    t, d = x2d.shape
    once = pl.Buffered(1)
    return pl.pallas_call(
        _outproj_body,
        grid=(t // tm,),
        in_specs=[
            pl.BlockSpec((tm, d), lambda i: (i, 0)),
            pl.BlockSpec((tm, ya.shape[1]), lambda i: (i, 0)),
            pl.BlockSpec((tm, yb.shape[1]), lambda i: (i, 0)),
            pl.BlockSpec((tm, yc.shape[1]), lambda i: (i, 0)),
            pl.BlockSpec((None,) + wa.shape[1:], lambda i: (layer, 0, 0), pipeline_mode=once),
            pl.BlockSpec((None,) + wb.shape[1:], lambda i: (layer, 0, 0), pipeline_mode=once),
            pl.BlockSpec((None,) + wc.shape[1:], lambda i: (layer, 0, 0), pipeline_mode=once),
        ],
        out_specs=pl.BlockSpec((tm, d), lambda i: (i, 0)),
        out_shape=jax.ShapeDtypeStruct((t, d), F32),
        compiler_params=_cparams(("parallel",)),
        name="out_proj",
    )(x2d, ya, yb, yc, wa, wb, wc)


def _peerq_body(x_ref, g_ref, w_ref, q_ref, xn_ref):
    xf = x_ref[...]
    ms = jnp.mean(xf * xf, axis=-1, keepdims=True)
    xn = (xf * lax.rsqrt(ms + EPS) * g_ref[...]).astype(BF16)
    xn_ref[...] = xn
    q_ref[...] = _dot(xn, w_ref[...])


def _peer_q(x2d, g, w, layer, *, tm):
    t, d = x2d.shape
    n = w.shape[2]
    return pl.pallas_call(
        _peerq_body,
        grid=(t // tm,),
        in_specs=[
            pl.BlockSpec((tm, d), lambda i: (i, 0)),
            pl.BlockSpec((None, 1, d), lambda i: (layer, 0, 0)),
            pl.BlockSpec((None, d, n), lambda i: (layer, 0, 0), pipeline_mode=pl.Buffered(1)),
        ],
        out_specs=[
            pl.BlockSpec((tm, n), lambda i: (i, 0)),
            pl.BlockSpec((tm, d), lambda i: (i, 0)),
        ],
        out_shape=[
            jax.ShapeDtypeStruct((t, n), F32),
            jax.ShapeDtypeStruct((t, d), BF16),
        ],
        compiler_params=_cparams(("parallel",)),
        name="peer_q",
    )(x2d, g, w)


def _extract_top(s, rowid, n_out):
    nrow = s.shape[0]
    rank = jnp.full(s.shape, float(n_out), F32)
    vals = []
    for it in range(n_out):
        m = jnp.max(s, axis=0, keepdims=True)
        idx = jnp.min(jnp.where(s == m, rowid, float(nrow)), axis=0, keepdims=True)
        hit = rowid == idx
        vals.append(m)
        rank = jnp.where(hit, float(it), rank)
        s = jnp.where(hit, -jnp.inf, s)
    return vals, rank, rank < float(n_out)


def _take_top_distinct(s, n_out):
    for _ in range(n_out):
        m = jnp.max(s, axis=0, keepdims=True)
        s = jnp.where(s == m, -jnp.inf, s)
    taken = s == -jnp.inf
    return taken, jnp.sum(jnp.where(taken, 1.0, 0.0), axis=0, keepdims=True)


def _top_sorted(s, n_out):
    nblk = s.shape[0] // 8
    assert nblk == n_out and nblk & (nblk - 1) == 0
    lst = [s[8 * k:8 * k + 8, :] for k in range(nblk)]

    def exchange(i, l, descending):
        hi, lo = jnp.maximum(lst[i], lst[l]), jnp.minimum(lst[i], lst[l])
        lst[i], lst[l] = (hi, lo) if descending else (lo, hi)

    def merge_bitonic():
        d = nblk // 2
        while d >= 1:
            for i in range(nblk):
                if (i // d) % 2 == 0:
                    exchange(i, i + d, True)
            d //= 2

    k = 2
    while k <= nblk:
        j = k // 2
        while j >= 1:
            for i in range(nblk):
                if i ^ j > i:
                    exchange(i, i ^ j, (i & k) == 0)
            j //= 2
        k *= 2
    for shift in (4, 2, 1):
        other = [pltpu.roll(x, shift, 0) for x in lst]
        lst = [jnp.maximum(lst[i], other[nblk - 1 - i]) for i in range(nblk)]
        merge_bitonic()
    return jnp.concatenate([x[0:1, :] for x in lst], axis=0)


_CAND_SHORT = 8


def _topk_body(q_ref, keys_ref, c1_ref, w1_ref, r2_ref, e2_ref, *, tt):
    rowid = lax.broadcasted_iota(jnp.int32, (N_KEYS, tt), 0).astype(F32)
    rowid2 = lax.broadcasted_iota(jnp.int32, (P_TOPK * P_TOPK, tt), 0).astype(F32)

    def pair_grid(x1, x2, op):
        parts = [op(x1[0:1, :], x2)]
        parts += [op(x1[a:a + 1, :], x2[0:_CAND_SHORT, :]) for a in range(1, _CAND_SHORT)]
        parts += [op(x1[_CAND_SHORT:, :], x2[0:1, :])]
        return jnp.concatenate(parts, axis=0)

    def finish(h, scores, v1, v2, is_rank, rank2, counts, zsum):
        cnt = jnp.zeros((N_KEYS, tt), F32)
        for a in range(P_TOPK):
            cnt = jnp.where(is_rank(a), counts[a], cnt)
        c1_ref[h] = cnt
        w1_ref[h] = jnp.exp(scores[0] - v1[0:1, :]) / zsum
        r2_ref[h] = rank2.astype(BF16)
        e2_ref[h] = jnp.exp(scores[1] - v2[0:1, :]).astype(BF16)

    def per_head(h, carry):
        scores = []
        for c in range(2):
            kh, kl = _split_bf16(keys_ref[2 * h + c])
            col = pl.multiple_of((2 * h + c) * P_HALF, P_HALF)
            qh, ql = _split_bf16(q_ref[:, pl.ds(col, P_HALF)])
            scores.append(_dot3(kh, kl, qh, ql, _dot_nt))

        tops, clean = [], None
        for c in range(2):
            v = _top_sorted(scores[c], P_TOPK)
            tops.append(v)
            repeats = jnp.max(jnp.where(v[0:P_TOPK - 1, :] == v[1:P_TOPK, :], 1.0, 0.0), axis=0, keepdims=True)
            reach = jnp.sum(jnp.where(scores[c] >= v[P_TOPK - 1:P_TOPK, :], 1.0, 0.0), axis=0, keepdims=True)
            ok = (repeats < 0.5) & (reach == float(P_TOPK))
            clean = ok if clean is None else (clean & ok)
        v1, v2 = tops
        rank2 = jnp.full((N_KEYS, tt), float(P_TOPK), F32)
        for a in range(P_TOPK):
            rank2 = jnp.where(scores[1] == v2[a:a + 1, :], float(a), rank2)
        cand = pair_grid(v1, v2, lambda x, y: x + y)
        picked, lost = _take_top_distinct(cand, P_TOPK)
        clean = clean & (lost == float(P_TOPK))
        ev1 = jnp.exp(v1 - v1[0:1, :])
        ev2 = jnp.exp(v2 - v2[0:1, :])
        prod = pair_grid(ev1, ev2, lambda x, y: x * y)
        zsum = jnp.sum(jnp.where(picked, prod, 0.0), axis=0, keepdims=True)
        npick = jnp.where(picked, 1.0, 0.0)
        counts = [jnp.sum(npick[0:P_TOPK, :], axis=0, keepdims=True)]
        for a in range(1, _CAND_SHORT):
            r0 = P_TOPK + (a - 1) * _CAND_SHORT
            counts.append(jnp.sum(npick[r0:r0 + _CAND_SHORT, :], axis=0, keepdims=True))
        r0 = P_TOPK + (_CAND_SHORT - 1) * _CAND_SHORT
        counts += [npick[r0 + a:r0 + a + 1, :] for a in range(P_TOPK - _CAND_SHORT)]
        finish(h, scores, v1, v2, lambda a: scores[0] == v1[a:a + 1, :], rank2, counts, zsum)

        @pl.when(jnp.min(jnp.where(clean, 1.0, 0.0)) < 0.5)
        def _():
            tops, ranks = [], []
            for c in range(2):
                vals, rank, _ = _extract_top(scores[c], rowid, P_TOPK)
                tops.append(jnp.concatenate(vals, axis=0))
                ranks.append(rank)
            v1, v2 = tops
            cand = jnp.concatenate([v1[a:a + 1, :] + v2 for a in range(P_TOPK)], axis=0)
            _, _, picked = _extract_top(cand, rowid2, P_TOPK)
            ev1 = jnp.exp(v1 - v1[0:1, :])
            ev2 = jnp.exp(v2 - v2[0:1, :])
            prod = jnp.concatenate([ev1[a:a + 1, :] * ev2 for a in range(P_TOPK)], axis=0)
            zsum = jnp.sum(jnp.where(picked, prod, 0.0), axis=0, keepdims=True)
            npick = jnp.where(picked, 1.0, 0.0)
            counts = [jnp.sum(npick[a * P_TOPK:(a + 1) * P_TOPK, :], axis=0, keepdims=True) for a in range(P_TOPK)]
            finish(h, scores, v1, v2, lambda a: ranks[0] == float(a), ranks[1], counts, zsum)

        return carry

    lax.fori_loop(0, P_HEADS, per_head, 0)


def _peer_topk(q, keys16, layer, *, tt):
    t = q.shape[0]
    st_shape = jax.ShapeDtypeStruct((P_HEADS, N_KEYS, t), F32)
    st_shape_b = jax.ShapeDtypeStruct((P_HEADS, N_KEYS, t), BF16)
    st_spec = pl.BlockSpec((P_HEADS, N_KEYS, tt), lambda i: (0, 0, i))
    return pl.pallas_call(
        functools.partial(_topk_body, tt=tt),
        grid=(t // tt,),
        in_specs=[
            pl.BlockSpec((tt, q.shape[1]), lambda i: (i, 0)),
            pl.BlockSpec((None,) + keys16.shape[1:], lambda i: (layer, 0, 0, 0)),
        ],
        out_specs=[st_spec] * 4,
        out_shape=[st_shape, st_shape, st_shape_b, st_shape_b],
        compiler_params=_cparams(("parallel",)),
        name="peer_topk",
    )(q, keys16)


def _peer_dense_body(xn_ref, u_ref, v_ref, c1_ref, w1_ref, r2_ref, e2_ref, x_ref, g_ref, o_ref, c1s_ref, w1s_ref, *hg_refs,
                     tm, te, tc, rb, final):
    e = pl.program_id(1)

    @pl.when(e == 0)
    def _():
        o_ref[...] = x_ref[...]

    ub = u_ref[...].astype(BF16)
    vb = v_ref[...].astype(BF16)

    ng = te // N_KEYS
    nchunk = tm // tc
    which = e % (8 // ng)

    for src, dst in ((c1_ref, c1s_ref), (w1_ref, w1s_ref)):
        rows = src[:, 0:ng, :]
        for g in range(1, 8 // ng):
            rows = jnp.where(which == g, src[:, g * ng:(g + 1) * ng, :], rows)
        dst[:, 0:ng, :] = rows

    def bf16_row(ref, h, ii, t0):
        return ref[h, ii:ii + 1, t0:t0 + tc].astype(BF16)

    def stage_a(ci):
        t0 = ci * tc
        ht = _dot_nt(ub, xn_ref[t0:t0 + tc, :])
        return ((0.5 * ht) * (1.0 + lax.erf(ht * (2.0 ** -0.5)))).astype(BF16)

    def stage_b(ci, act):
        t0 = ci * tc
        for ii in range(ng):
            rows = [(bf16_row(c1s_ref, h, ii, t0), bf16_row(w1s_ref, h, ii, t0)) for h in range(P_HEADS)]
            for j0 in range(0, N_KEYS, rb):
                r0 = ii * N_KEYS + j0
                gate = jnp.zeros((rb, tc), BF16)
                for h in range(P_HEADS):
                    cnt, w1 = rows[h]
                    gate = gate + jnp.where(r2_ref[h, j0:j0 + rb, t0:t0 + tc] < cnt,
                                            e2_ref[h, j0:j0 + rb, t0:t0 + tc] * w1, jnp.zeros((), BF16))
                hg_refs[ci][r0:r0 + rb, :] = act[r0:r0 + rb, :] * gate

    def stage_c(ci):
        t0 = ci * tc
        o_ref[t0:t0 + tc, :] += _dot_tn(hg_refs[ci][...], vb)

    ahead = 2
    hts = {ci: stage_a(ci) for ci in range(min(ahead, nchunk))}
    for ci in range(nchunk):
        if ci + ahead < nchunk:
            hts[ci + ahead] = stage_a(ci + ahead)
        stage_b(ci, hts.pop(ci))
        stage_c(ci)

    if final:
        @pl.when(e == pl.num_programs(1) - 1)
        def _():
            y = o_ref[...]
            o_ref[...] = y * lax.rsqrt(jnp.mean(y * y, axis=-1, keepdims=True) + EPS) * g_ref[...]


def _peer_dense(xn, u, v, layer, c1, w1, r2, e2, x2d, g, *, tm, te, tc, rb, final):
    t, d = xn.shape
    ne = u.shape[1]
    ng = te // N_KEYS
    once = pl.Buffered(1)
    row_spec = pl.BlockSpec((P_HEADS, 8, tm), lambda i, e: (0, e // (8 // ng), i))
    full_spec = pl.BlockSpec((P_HEADS, N_KEYS, tm), lambda i, e: (0, 0, i), pipeline_mode=once)
    tok_spec = pl.BlockSpec((tm, d), lambda i, e: (i, 0), pipeline_mode=once)
    return pl.pallas_call(
        functools.partial(_peer_dense_body, tm=tm, te=te, tc=tc, rb=rb, final=final),
        grid=(t // tm, ne // te),
        in_specs=[
            tok_spec,
            pl.BlockSpec((None, te, d), lambda i, e: (layer, e, 0)),
            pl.BlockSpec((None, te, d), lambda i, e: (layer, e, 0)),
            row_spec, row_spec, full_spec, full_spec,
            tok_spec,
            pl.BlockSpec((1, d), lambda i, e: (0, 0)),
        ],
        out_specs=pl.BlockSpec((tm, d), lambda i, e: (i, 0), pipeline_mode=once),
        out_shape=jax.ShapeDtypeStruct((t, d), F32),
        scratch_shapes=[pltpu.VMEM((P_HEADS, 8, tm), F32), pltpu.VMEM((P_HEADS, 8, tm), F32)]
        + [pltpu.VMEM((te, tc), BF16)] * (tm // tc),
        compiler_params=_cparams(("parallel", "arbitrary")),
        name="peer_dense",
    )(xn, u, v, c1, w1, r2, e2, x2d, g)


def _pad_heads(w):
    lead = w.shape[:-1]
    w = w.reshape(lead + (B_HEADS, B_HEAD_DIM))
    w = jnp.pad(w, [(0, 0)] * len(lead) + [(0, 0), (0, B_HEAD_PAD - B_HEAD_DIM)])
    return w.reshape(lead + (B_PAD_WIDTH,))


def _regroup_w_in(w):
    sizes = (A_WIDTH, A_WIDTH, A_WIDTH, A_HEADS, 2 * B_WIDTH, B_WIDTH, B_WIDTH, B_HEADS, B_HEADS, C_WIDTH, C_WIDTH)
    parts, p = [], 0
    for sz in sizes:
        parts.append(w[..., p:p + sz])
        p += sz
    aq, ak, av, af, bqk, bv, bo, bi, bf, ca, cg = parts
    lead = w.shape[:-1]
    main = jnp.concatenate(
        [_pad_heads(bqk[..., :B_WIDTH]), _pad_heads(bqk[..., B_WIDTH:]), _pad_heads(bv), _pad_heads(bo),
         ca, cg, aq, ak, av, jnp.zeros(lead + (Z_WIDTH - Z_USED,), w.dtype)], axis=-1)
    gates = jnp.concatenate([af, bi, bf, jnp.zeros(lead + (GATE_LANES - A_HEADS - 2 * B_HEADS,), w.dtype)], axis=-1)
    return main.astype(BF16), gates


def _tile_sizes(t, seq):
    return dict(
        tm_in=min(1024, t), tn_in=1280,
        tb_gate=min(256, seq),
        tq=min(512, seq), tk=min(512, seq),
        ts_conf=min(256, seq),
        tm_out=min(512, t),
        tm_q=min(512, t),
        tt=min(256, t),
        tm_pd=min(1024, t), te=512, tc_pd=min(256, t), rb_pd=32,
    )


def kernel(x, w_in, fox_f_bias, mlstm_conv_w, mlstm_i_bias, mlstm_f_bias, conf_dw_w, conf_dw_b, conf_ln_g, conf_ln_b, conf_pw_w, conf_pw_b, out_norm_g, w_out, norm1_g, norm2_g, peer_wq, peer_keys, peer_u, peer_v, final_g):
    batch, seq, d = x.shape
    t = batch * seq
    depth = w_in.shape[0]
    ts = _tile_sizes(t, seq)
    xf = x.reshape(t, d)
    rows = lambda v: v[:, None, :]

    w_main, w_gate = _regroup_w_in(w_in)
    wgh, wgl = _split_bf16(w_gate)
    gate_bias = rows(jnp.concatenate([fox_f_bias, mlstm_i_bias, mlstm_f_bias,
                                      jnp.zeros((depth, GATE_LANES - A_HEADS - 2 * B_HEADS), F32)], axis=-1))
    g_a = rows(out_norm_g[:, :A_WIDTH])
    g_b = rows(_pad_heads(out_norm_g[:, A_WIDTH:A_WIDTH + B_WIDTH]))
    g_c = rows(out_norm_g[:, A_WIDTH + B_WIDTH:])
    cw = jnp.concatenate([_pad_heads(mlstm_conv_w[..., :B_WIDTH]), _pad_heads(mlstm_conv_w[..., B_WIDTH:])], axis=-1)
    wa = w_out[:, :A_WIDTH].astype(BF16)
    wb = jnp.pad(w_out[:, A_WIDTH:A_WIDTH + B_WIDTH].reshape(depth, B_HEADS, B_HEAD_DIM, d),
                 ((0, 0), (0, 0), (0, B_HEAD_PAD - B_HEAD_DIM), (0, 0))).reshape(depth, B_PAD_WIDTH, d).astype(BF16)
    wc = w_out[:, A_WIDTH + B_WIDTH:].astype(BF16)
    pw = conf_pw_w.astype(BF16)
    wq = peer_wq.astype(BF16)
    keys16 = peer_keys.reshape(depth, 2 * P_HEADS, N_KEYS, P_HALF)
    n1, n2 = rows(norm1_g), rows(norm2_g)
    db, lg, lb, pb = rows(conf_dw_b), rows(conf_ln_g), rows(conf_ln_b), rows(conf_pw_b)

    for l in range(depth):
        z, zg = _in_proj(xf, n1, w_main, wgh, wgl, l, tm=ts["tm_in"], tn=ts["tn_in"])
        gcol, grow = _gates(zg.reshape(batch, seq, GATE_LANES), gate_bias, l, tb=ts["tb_gate"])
        z3 = z.reshape(batch, seq, Z_WIDTH)
        ya = _attention(z, gcol, grow.reshape(batch, GATE_ROWS, 1, seq), g_a, l, batch=batch, seq=seq, tq=ts["tq"], tk=ts["tk"])
        yb = _mlstm(z3, gcol, grow, cw, g_b, l)
        yc = _conformer(z3, conf_dw_w, db, lg, lb, pw, pb, g_c, l, ts=ts["ts_conf"])
        xf = _out_proj(xf, ya, yb.reshape(t, B_PAD_WIDTH), yc.reshape(t, C_WIDTH), wa, wb, wc, l, tm=ts["tm_out"])

        q, xn = _peer_q(xf, n2, wq, l, tm=ts["tm_q"])
        c1, w1, r2, e2 = _peer_topk(q, keys16, l, tt=ts["tt"])
        xf = _peer_dense(xn, peer_u, peer_v, l, c1, w1, r2, e2, xf, final_g.reshape(1, d),
                         tm=ts["tm_pd"], te=ts["te"], tc=ts["tc_pd"], rb=ts["rb_pd"], final=(l == depth - 1))

    return xf.reshape(batch, seq, d)
```

```python
import functools

import jax
import jax.numpy as jnp
from jax import lax
from jax.experimental import pallas as pl
from jax.experimental.pallas import tpu as pltpu

F32 = jnp.float32
BF16 = jnp.bfloat16

EPS = 1e-6
LN_EPS = 1e-5

A_HEADS = 6
A_HEAD_DIM = 128
A_WIDTH = A_HEADS * A_HEAD_DIM
B_HEADS = 4
B_HEAD_DIM = 192
B_HEAD_PAD = 256
B_WIDTH = B_HEADS * B_HEAD_DIM
B_PAD_WIDTH = B_HEADS * B_HEAD_PAD
B_CHUNK = 64
B_CONV = 4
C_GROUPS = 4
C_WIDTH = 512
C_CONV = 31
P_HEADS = 8
P_HALF = 128
N_KEYS = 128
P_TOPK = 16
GATE_LANES = 128

COL_BQ = 0
COL_BK = B_PAD_WIDTH
COL_BV = 2 * B_PAD_WIDTH
COL_BO = 3 * B_PAD_WIDTH
COL_CA = 4 * B_PAD_WIDTH
COL_CG = COL_CA + C_WIDTH
COL_AQ = COL_CG + C_WIDTH
COL_AK = COL_AQ + A_WIDTH
COL_AV = COL_AK + A_WIDTH
Z_USED = COL_AV + A_WIDTH
Z_WIDTH = 7680
G_BI = A_HEADS
G_BF = A_HEADS + B_HEADS
GATE_ROWS = 16

VMEM_LIMIT = 60 * 1024 * 1024


def _cparams(sem):
    return pltpu.CompilerParams(dimension_semantics=sem, vmem_limit_bytes=VMEM_LIMIT)


def _split_bf16(a):
    hi = a.astype(BF16)
    lo = (a - hi.astype(F32)).astype(BF16)
    return hi, lo


def _dot(a, b):
    return jnp.dot(a, b, preferred_element_type=F32)


def _dot_nt(a, b):
    return lax.dot_general(a, b, (((1,), (1,)), ((), ())), preferred_element_type=F32)


def _dot_tn(a, b):
    return lax.dot_general(a, b, (((0,), (0,)), ((), ())), preferred_element_type=F32)


def _dot3(ah, al, bh, bl, dot):
    return dot(ah, bh) + dot(ah, bl) + dot(al, bh)


def _sigmoid(x):
    return 1.0 / (1.0 + jnp.exp(-x))


def _inproj_body(x_ref, g_ref, w_ref, wgh_ref, wgl_ref, z_ref, zg_ref, xn_ref):
    @pl.when(pl.program_id(1) == 0)
    def _():
        xf = x_ref[...]
        ms = jnp.mean(xf * xf, axis=-1, keepdims=True)
        xn = xf * lax.rsqrt(ms + EPS) * g_ref[...]
        hi, lo = _split_bf16(xn)
        xn_ref[...] = hi
        zg_ref[...] = _dot3(hi, lo, wgh_ref[...], wgl_ref[...], _dot)

    z_ref[...] = _dot(xn_ref[...], w_ref[...])


def _in_proj(x2d, g, w, wgh, wgl, layer, *, tm, tn):
    t, d = x2d.shape
    n = w.shape[2]
    return pl.pallas_call(
        _inproj_body,
        grid=(t // tm, n // tn),
        in_specs=[
            pl.BlockSpec((tm, d), lambda i, j: (i, 0), pipeline_mode=pl.Buffered(1)),
            pl.BlockSpec((None, 1, d), lambda i, j: (layer, 0, 0)),
            pl.BlockSpec((None, d, tn), lambda i, j: (layer, 0, j)),
            pl.BlockSpec((None, d, GATE_LANES), lambda i, j: (layer, 0, 0)),
            pl.BlockSpec((None, d, GATE_LANES), lambda i, j: (layer, 0, 0)),
        ],
        out_specs=[
            pl.BlockSpec((tm, tn), lambda i, j: (i, j)),
            pl.BlockSpec((tm, GATE_LANES), lambda i, j: (i, 0)),
        ],
        out_shape=[
            jax.ShapeDtypeStruct((t, n), F32),
            jax.ShapeDtypeStruct((t, GATE_LANES), F32),
        ],
        scratch_shapes=[pltpu.VMEM((tm, d), BF16)],
        compiler_params=_cparams(("parallel", "arbitrary")),
        name="in_proj",
    )(x2d, g, w, wgh, wgl)


def _gates_body(zg_ref, bias_ref, col_ref, row_ref, carry_ref, *, tb):
    @pl.when(pl.program_id(1) == 0)
    def _():
        carry_ref[...] = jnp.zeros_like(carry_ref)

    v = zg_ref[0] + bias_ref[...]
    ls = jnp.minimum(v, 0.0) - jnp.log1p(jnp.exp(-jnp.abs(v)))
    r = lax.broadcasted_iota(jnp.int32, (tb, tb), 0)
    c = lax.broadcasted_iota(jnp.int32, (tb, tb), 1)
    tri = (c <= r).astype(F32)
    tri_chunk = jnp.where((r // B_CHUNK) == (c // B_CHUNK), tri, 0.0)
    csum = jnp.dot(tri, ls, precision=lax.Precision.HIGHEST, preferred_element_type=F32) + carry_ref[...]
    bcum = jnp.dot(tri_chunk, ls, precision=lax.Precision.HIGHEST, preferred_element_type=F32)
    carry_ref[...] = csum[tb - 1:tb, :]
    lane = lax.broadcasted_iota(jnp.int32, (tb, GATE_LANES), 1)
    out = jnp.where(lane < G_BI, csum, jnp.where(lane < G_BF, v, bcum))
    col_ref[0] = out
    row_ref[0] = out.T[0:GATE_ROWS, :]


def _gates(zg3, bias, layer, *, tb):
    b, s, _ = zg3.shape
    return pl.pallas_call(
        functools.partial(_gates_body, tb=tb),
        grid=(b, s // tb),
        in_specs=[
            pl.BlockSpec((1, tb, GATE_LANES), lambda i, j: (i, j, 0)),
            pl.BlockSpec((None, 1, GATE_LANES), lambda i, j: (layer, 0, 0)),
        ],
        out_specs=[
            pl.BlockSpec((1, tb, GATE_LANES), lambda i, j: (i, j, 0)),
            pl.BlockSpec((1, GATE_ROWS, tb), lambda i, j: (i, 0, j)),
        ],
        out_shape=[
            jax.ShapeDtypeStruct((b, s, GATE_LANES), F32),
            jax.ShapeDtypeStruct((b, GATE_ROWS, s), F32),
        ],
        scratch_shapes=[pltpu.VMEM((1, GATE_LANES), F32)],
        compiler_params=_cparams(("parallel", "arbitrary")),
        name="gates",
    )(zg3, bias)


def _attn_body(q_ref, k_ref, v_ref, ccol_ref, crow_ref, g_ref, o_ref, kb_ref, vb_ref, m_ref, l_ref, acc_ref, *, tq, tk):
    h = pl.program_id(1)
    qi = pl.program_id(2)
    log2e = 1.4426950408889634
    scale2 = (A_HEAD_DIM ** -0.5) * log2e

    @pl.when(qi == 0)
    def _():
        kb_ref[...] = k_ref[...].astype(BF16)
        vb_ref[...] = v_ref[...].astype(BF16)

    q = q_ref[...].astype(BF16)
    lane = lax.broadcasted_iota(jnp.int32, (tq, GATE_LANES), 1)
    ct2 = jnp.sum(jnp.where(lane == h, ccol_ref[0], 0.0), axis=-1, keepdims=True) * log2e
    m_ref[...] = jnp.full(m_ref.shape, -jnp.inf, F32)
    l_ref[...] = jnp.zeros_like(l_ref)
    acc_ref[...] = jnp.zeros_like(acc_ref)

    def logits(j):
        k0 = pl.multiple_of(j * tk, tk)
        cs2 = crow_ref[0, 0, :, pl.ds(k0, tk)] * log2e
        return _dot_nt(q, kb_ref[pl.ds(k0, tk), :]) * scale2 + (ct2 - cs2)

    def softmax(s):
        m_prev = m_ref[...]
        m_new = jnp.maximum(m_prev, jnp.max(s, axis=-1, keepdims=True))
        p = jnp.exp2(s - m_new)
        alpha = jnp.exp2(m_prev - m_new)
        l_ref[...] = alpha * l_ref[...] + jnp.sum(p, axis=-1, keepdims=True)
        m_ref[...] = m_new
        return p.astype(BF16), alpha

    def values(j, p, alpha):
        k0 = pl.multiple_of(j * tk, tk)
        acc_ref[...] = alpha * acc_ref[...] + _dot(p, vb_ref[pl.ds(k0, tk), :])

    def body(j, s_cur):
        s_next = logits(j + 1)
        p, alpha = softmax(s_cur)
        values(j, p, alpha)
        return s_next

    per_q = tq // tk
    first = qi * per_q
    s_cur = lax.fori_loop(0, first, body, logits(0))
    row = lax.broadcasted_iota(jnp.int32, (tq, tk), 0)
    col = lax.broadcasted_iota(jnp.int32, (tq, tk), 1)
    for dd in range(per_q):
        s_next = logits(first + dd + 1) if dd + 1 < per_q else None
        p, alpha = softmax(jnp.where(col + dd * tk <= row, s_cur, -jnp.inf))
        values(first + dd, p, alpha)
        s_cur = s_next
    o = acc_ref[...] / l_ref[...]
    o = o * lax.rsqrt(jnp.mean(o * o, axis=-1, keepdims=True) + EPS) * g_ref[...]
    o_ref[...] = o.astype(BF16)


def _attention(z2d, gcol, grow, g_a, layer, *, batch, seq, tq, tk):
    nq = seq // tq
    qc, kc, vc = COL_AQ // 128, COL_AK // 128, COL_AV // 128
    return pl.pallas_call(
        functools.partial(_attn_body, tq=tq, tk=tk),
        grid=(batch, A_HEADS, nq),
        in_specs=[
            pl.BlockSpec((tq, 128), lambda b, h, i: (b * nq + i, qc + h)),
            pl.BlockSpec((seq, 128), lambda b, h, i: (b, kc + h)),
            pl.BlockSpec((seq, 128), lambda b, h, i: (b, vc + h)),
            pl.BlockSpec((1, tq, GATE_LANES), lambda b, h, i: (b, i, 0)),
            pl.BlockSpec((1, 1, 1, seq), lambda b, h, i: (b, h, 0, 0)),
            pl.BlockSpec((None, 1, 128), lambda b, h, i: (layer, 0, h)),
        ],
        out_specs=pl.BlockSpec((tq, 128), lambda b, h, i: (b * nq + i, h)),
        out_shape=jax.ShapeDtypeStruct((batch * seq, A_WIDTH), BF16),
        scratch_shapes=[pltpu.VMEM((seq, 128), BF16), pltpu.VMEM((seq, 128), BF16),
                        pltpu.VMEM((tq, 1), F32), pltpu.VMEM((tq, 1), F32), pltpu.VMEM((tq, 128), F32)],
        compiler_params=_cparams(("parallel", "parallel", "arbitrary")),
        name="fox_attention",
    )(z2d, z2d, z2d, gcol, grow, g_a)


def _mlstm_body(zb_ref, gcol_ref, grow_ref, cw_ref, g_ref, y_ref, ext_ref, c_ref, m_ref, *, nb):
    k = pl.program_id(0)
    ln = B_CHUNK
    hp = B_HEAD_PAD

    @pl.when(k == 0)
    def _():
        ext_ref[:, 0:8, :] = jnp.zeros((nb, 8, 2 * B_PAD_WIDTH), F32)
        c_ref[...] = jnp.zeros_like(c_ref)
        m_ref[...] = jnp.zeros_like(m_ref)

    first_half = (k % 2) == 0
    r = lax.broadcasted_iota(jnp.int32, (ln, ln), 0)
    c = lax.broadcasted_iota(jnp.int32, (ln, ln), 1)
    causal = c <= r
    lane = lax.broadcasted_iota(jnp.int32, (ln, hp), 1)
    qscale = B_HEAD_DIM ** -0.5

    for b in range(nb):
        ext_ref[b, 8:8 + ln, :] = zb_ref[b, :, 0:2 * B_PAD_WIDTH]
        conv = cw_ref[0:1, :] * ext_ref[b, 5:5 + ln, :]
        for j in range(1, B_CONV):
            conv = conv + cw_ref[j:j + 1, :] * ext_ref[b, 5 + j:5 + j + ln, :]
        ext_ref[b, 0:8, :] = ext_ref[b, ln:ln + 8, :]
        qk = conv * _sigmoid(conv)
        grow = grow_ref[b]
        grow = jnp.where(first_half, grow[:, 0:ln], grow[:, ln:2 * ln])
        gcol = gcol_ref[b]
        for h in range(B_HEADS):
            st = b * B_HEADS + h
            qq = (qk[:, COL_BQ + h * hp:COL_BQ + (h + 1) * hp] * qscale).astype(BF16)
            kk = qk[:, COL_BK + h * hp:COL_BK + (h + 1) * hp]
            vv = zb_ref[b, :, COL_BV + h * hp:COL_BV + (h + 1) * hp]
            vv = jnp.where(lane == B_HEAD_DIM, 1.0, vv).astype(BF16)
            og = zb_ref[b, :, COL_BO + h * hp:COL_BO + (h + 1) * hp]
            bc_col = gcol[:, G_BF + h:G_BF + h + 1]
            li_col = gcol[:, G_BI + h:G_BI + h + 1]
            bc_row = grow[G_BF + h:G_BF + h + 1, :]
            li_row = grow[G_BI + h:G_BI + h + 1, :]
            m = m_ref[st, 0:1, 0:1]

            d = jnp.where(causal, bc_col - bc_row + li_row, -jnp.inf)
            inter = bc_col + m
            m_t = jnp.maximum(inter, jnp.max(d, axis=-1, keepdims=True))
            w_inter = jnp.exp(inter - m_t)
            sc = _dot_nt(qq, kk.astype(BF16)) * jnp.exp(d - m_t)
            cm = c_ref[st]
            num = w_inter * _dot(qq, cm.astype(BF16)) + _dot(sc.astype(BF16), vv)
            den = num[:, B_HEAD_DIM:B_HEAD_DIM + 1]
            hh = num / jnp.maximum(jnp.abs(den), jnp.exp(-m_t))

            b_last = bc_col[ln - 1:ln, :]
            dec = b_last - bc_col + li_col
            m_new = jnp.maximum(b_last + m, jnp.max(dec, axis=0, keepdims=True))
            a = jnp.exp(b_last + m - m_new)
            wk = (jnp.exp(dec - m_new) * kk).astype(BF16)
            c_ref[st] = a * cm + _dot_tn(wk, vv)
            m_ref[st] = jnp.broadcast_to(m_new, (8, 128))

            y = jnp.where(lane < B_HEAD_DIM, _sigmoid(og) * hh, 0.0)
            ms = jnp.sum(y * y, axis=-1, keepdims=True) * (1.0 / B_HEAD_DIM)
            y = y * lax.rsqrt(ms + EPS) * g_ref[:, h * hp:(h + 1) * hp]
            y_ref[b, :, h * hp:(h + 1) * hp] = y.astype(BF16)


def _mlstm(z3, gcol, grow, cw, g_b, layer):
    nb, seq, _ = z3.shape
    nc = seq // B_CHUNK
    return pl.pallas_call(
        functools.partial(_mlstm_body, nb=nb),
        grid=(nc,),
        in_specs=[
            pl.BlockSpec((nb, B_CHUNK, 4 * B_PAD_WIDTH), lambda k: (0, k, 0)),
            pl.BlockSpec((nb, B_CHUNK, GATE_LANES), lambda k: (0, k, 0)),
            pl.BlockSpec((nb, GATE_ROWS, 2 * B_CHUNK), lambda k: (0, 0, k // 2)),
            pl.BlockSpec((None, B_CONV, 2 * B_PAD_WIDTH), lambda k: (layer, 0, 0)),
            pl.BlockSpec((None, 1, B_PAD_WIDTH), lambda k: (layer, 0, 0)),
        ],
        out_specs=pl.BlockSpec((nb, B_CHUNK, B_PAD_WIDTH), lambda k: (0, k, 0)),
        out_shape=jax.ShapeDtypeStruct((nb, seq, B_PAD_WIDTH), BF16),
        scratch_shapes=[
            pltpu.VMEM((nb, B_CHUNK + 8, 2 * B_PAD_WIDTH), F32),
            pltpu.VMEM((nb * B_HEADS, B_HEAD_PAD, B_HEAD_PAD), F32),
            pltpu.VMEM((nb * B_HEADS, 8, 128), F32),
        ],
        compiler_params=_cparams(("arbitrary",)),
        name="mlstm",
    )(z3, gcol, grow, cw, g_b)


def _conf_body(ca_ref, cg_ref, dw_ref, db_ref, lg_ref, lb_ref, pw_ref, pb_ref, g_ref, y_ref, ext_ref, *, ts):
    halo = 32

    @pl.when(pl.program_id(1) == 0)
    def _():
        ext_ref[0:halo, :] = jnp.zeros((halo, C_WIDTH), F32)

    ext_ref[halo:halo + ts, :] = ca_ref[0] * _sigmoid(cg_ref[0])
    off = halo - (C_CONV - 1)
    acc = None
    for r in range(8):
        taps = [(m, 8 * m + r - off) for m in range(halo // 8 + 1) if 0 <= 8 * m + r - off < C_CONV]
        ln = ts if r == 0 else ts + 8
        part = None
        for m, j in taps:
            term = dw_ref[j:j + 1, :] * ext_ref[8 * m:8 * m + ln, :]
            part = term if part is None else part + term
        part = part if r == 0 else part[r:r + ts, :]
        acc = part if acc is None else acc + part
    ext_ref[0:halo, :] = ext_ref[ts:ts + halo, :]
    y = acc + db_ref[...]
    mu = jnp.mean(y, axis=-1, keepdims=True)
    yc = y - mu
    y = yc * lax.rsqrt(jnp.mean(yc * yc, axis=-1, keepdims=True) + LN_EPS)
    y = y * lg_ref[...] + lb_ref[...]
    y = y * _sigmoid(y)
    y = _dot(y.astype(BF16), pw_ref[...]) + pb_ref[...]
    gw = C_WIDTH // C_GROUPS
    for gi in range(C_GROUPS):
        yg = y[:, gi * gw:(gi + 1) * gw]
        yg = yg * lax.rsqrt(jnp.mean(yg * yg, axis=-1, keepdims=True) + EPS) * g_ref[:, gi * gw:(gi + 1) * gw]
        y_ref[0, :, gi * gw:(gi + 1) * gw] = yg.astype(BF16)


def _conformer(z3, dw, db, lg, lb, pw, pb, g_c, layer, *, ts):
    nb, seq, _ = z3.shape
    ca, cg = COL_CA // C_WIDTH, COL_CG // C_WIDTH
    vec = pl.BlockSpec((None, 1, C_WIDTH), lambda b, i: (layer, 0, 0))
    return pl.pallas_call(
        functools.partial(_conf_body, ts=ts),
        grid=(nb, seq // ts),
        in_specs=[
            pl.BlockSpec((1, ts, C_WIDTH), lambda b, i: (b, i, ca)),
            pl.BlockSpec((1, ts, C_WIDTH), lambda b, i: (b, i, cg)),
            pl.BlockSpec((None, C_CONV, C_WIDTH), lambda b, i: (layer, 0, 0)),
            vec, vec, vec,
            pl.BlockSpec((None, C_WIDTH, C_WIDTH), lambda b, i: (layer, 0, 0)),
            vec, vec,
        ],
        out_specs=pl.BlockSpec((1, ts, C_WIDTH), lambda b, i: (b, i, 0)),
        out_shape=jax.ShapeDtypeStruct((nb, seq, C_WIDTH), BF16),
        scratch_shapes=[pltpu.VMEM((ts + 32, C_WIDTH), F32)],
        compiler_params=_cparams(("parallel", "arbitrary")),
        name="conformer",
    )(z3, z3, dw, db, lg, lb, pw, pb, g_c)


def _outproj_body(x_ref, ya_ref, yb_ref, yc_ref, wa_ref, wb_ref, wc_ref, o_ref):
    o_ref[...] = (x_ref[...] + _dot(ya_ref[...], wa_ref[...]) + _dot(yb_ref[...], wb_ref[...])
                  + _dot(yc_ref[...], wc_ref[...]))


def _out_proj(x2d, ya, yb, yc, wa, wb, wc, layer, *, tm):
    t, d = x2d.shape
    once = pl.Buffered(1)
    return pl.pallas_call(
        _outproj_body,
        grid=(t // tm,),
        in_specs=[
            pl.BlockSpec((tm, d), lambda i: (i, 0)),
            pl.BlockSpec((tm, ya.shape[1]), lambda i: (i, 0)),
            pl.BlockSpec((tm, yb.shape[1]), lambda i: (i, 0)),
            pl.BlockSpec((tm, yc.shape[1]), lambda i: (i, 0)),
            pl.BlockSpec((None,) + wa.shape[1:], lambda i: (layer, 0, 0), pipeline_mode=once),
            pl.BlockSpec((None,) + wb.shape[1:], lambda i: (layer, 0, 0), pipeline_mode=once),
            pl.BlockSpec((None,) + wc.shape[1:], lambda i: (layer, 0, 0), pipeline_mode=once),
        ],
        out_specs=pl.BlockSpec((tm, d), lambda i: (i, 0)),
        out_shape=jax.ShapeDtypeStruct((t, d), F32),
        compiler_params=_cparams(("parallel",)),
        name="out_proj",
    )(x2d, ya, yb, yc, wa, wb, wc)


def _peerq_body(x_ref, g_ref, w_ref, q_ref, xn_ref):
    xf = x_ref[...]
    ms = jnp.mean(xf * xf, axis=-1, keepdims=True)
    xn = (xf * lax.rsqrt(ms + EPS) * g_ref[...]).astype(BF16)
    xn_ref[...] = xn
    q_ref[...] = _dot(xn, w_ref[...])


def _peer_q(x2d, g, w, layer, *, tm):
    t, d = x2d.shape
    n = w.shape[2]
    return pl.pallas_call(
        _peerq_body,
        grid=(t // tm,),
        in_specs=[
            pl.BlockSpec((tm, d), lambda i: (i, 0)),
            pl.BlockSpec((None, 1, d), lambda i: (layer, 0, 0)),
            pl.BlockSpec((None, d, n), lambda i: (layer, 0, 0), pipeline_mode=pl.Buffered(1)),
        ],
        out_specs=[
            pl.BlockSpec((tm, n), lambda i: (i, 0)),
            pl.BlockSpec((tm, d), lambda i: (i, 0)),
        ],
        out_shape=[
            jax.ShapeDtypeStruct((t, n), F32),
            jax.ShapeDtypeStruct((t, d), BF16),
        ],
        compiler_params=_cparams(("parallel",)),
        name="peer_q",
    )(x2d, g, w)


def _extract_top(s, rowid, n_out):
    nrow = s.shape[0]
    rank = jnp.full(s.shape, float(n_out), F32)
    vals = []
    for it in range(n_out):
        m = jnp.max(s, axis=0, keepdims=True)
        idx = jnp.min(jnp.where(s == m, rowid, float(nrow)), axis=0, keepdims=True)
        hit = rowid == idx
        vals.append(m)
        rank = jnp.where(hit, float(it), rank)
        s = jnp.where(hit, -jnp.inf, s)
    return vals, rank, rank < float(n_out)


def _take_top_distinct(s, n_out):
    for _ in range(n_out):
        m = jnp.max(s, axis=0, keepdims=True)
        s = jnp.where(s == m, -jnp.inf, s)
    taken = s == -jnp.inf
    return taken, jnp.sum(jnp.where(taken, 1.0, 0.0), axis=0, keepdims=True)


def _top_sorted(s, n_out):
    nblk = s.shape[0] // 8
    assert nblk == n_out and nblk & (nblk - 1) == 0
    lst = [s[8 * k:8 * k + 8, :] for k in range(nblk)]

    def exchange(i, l, descending):
        hi, lo = jnp.maximum(lst[i], lst[l]), jnp.minimum(lst[i], lst[l])
        lst[i], lst[l] = (hi, lo) if descending else (lo, hi)

    def merge_bitonic():
        d = nblk // 2
        while d >= 1:
            for i in range(nblk):
                if (i // d) % 2 == 0:
                    exchange(i, i + d, True)
            d //= 2

    k = 2
    while k <= nblk:
        j = k // 2
        while j >= 1:
            for i in range(nblk):
                if i ^ j > i:
                    exchange(i, i ^ j, (i & k) == 0)
            j //= 2
        k *= 2
    for shift in (4, 2, 1):
        other = [pltpu.roll(x, shift, 0) for x in lst]
        lst = [jnp.maximum(lst[i], other[nblk - 1 - i]) for i in range(nblk)]
        merge_bitonic()
    return jnp.concatenate([x[0:1, :] for x in lst], axis=0)


_CAND_SHORT = 8


def _topk_body(q_ref, keys_ref, c1_ref, w1_ref, r2_ref, e2_ref, *, tt):
    rowid = lax.broadcasted_iota(jnp.int32, (N_KEYS, tt), 0).astype(F32)
    rowid2 = lax.broadcasted_iota(jnp.int32, (P_TOPK * P_TOPK, tt), 0).astype(F32)

    def pair_grid(x1, x2, op):
        parts = [op(x1[0:1, :], x2)]
        parts += [op(x1[a:a + 1, :], x2[0:_CAND_SHORT, :]) for a in range(1, _CAND_SHORT)]
        parts += [op(x1[_CAND_SHORT:, :], x2[0:1, :])]
        return jnp.concatenate(parts, axis=0)

    def finish(h, scores, v1, v2, is_rank, rank2, counts, zsum):
        cnt = jnp.zeros((N_KEYS, tt), F32)
        for a in range(P_TOPK):
            cnt = jnp.where(is_rank(a), counts[a], cnt)
        c1_ref[h] = cnt
        w1_ref[h] = jnp.exp(scores[0] - v1[0:1, :]) / zsum
        r2_ref[h] = rank2.astype(BF16)
        e2_ref[h] = jnp.exp(scores[1] - v2[0:1, :]).astype(BF16)

    def per_head(h, carry):
        scores = []
        for c in range(2):
            kh, kl = _split_bf16(keys_ref[2 * h + c])
            col = pl.multiple_of((2 * h + c) * P_HALF, P_HALF)
            qh, ql = _split_bf16(q_ref[:, pl.ds(col, P_HALF)])
            scores.append(_dot3(kh, kl, qh, ql, _dot_nt))

        tops, clean = [], None
        for c in range(2):
            v = _top_sorted(scores[c], P_TOPK)
            tops.append(v)
            repeats = jnp.max(jnp.where(v[0:P_TOPK - 1, :] == v[1:P_TOPK, :], 1.0, 0.0), axis=0, keepdims=True)
            reach = jnp.sum(jnp.where(scores[c] >= v[P_TOPK - 1:P_TOPK, :], 1.0, 0.0), axis=0, keepdims=True)
            ok = (repeats < 0.5) & (reach == float(P_TOPK))
            clean = ok if clean is None else (clean & ok)
        v1, v2 = tops
        rank2 = jnp.full((N_KEYS, tt), float(P_TOPK), F32)
        for a in range(P_TOPK):
            rank2 = jnp.where(scores[1] == v2[a:a + 1, :], float(a), rank2)
        cand = pair_grid(v1, v2, lambda x, y: x + y)
        picked, lost = _take_top_distinct(cand, P_TOPK)
        clean = clean & (lost == float(P_TOPK))
        ev1 = jnp.exp(v1 - v1[0:1, :])
        ev2 = jnp.exp(v2 - v2[0:1, :])
        prod = pair_grid(ev1, ev2, lambda x, y: x * y)
        zsum = jnp.sum(jnp.where(picked, prod, 0.0), axis=0, keepdims=True)
        npick = jnp.where(picked, 1.0, 0.0)
        counts = [jnp.sum(npick[0:P_TOPK, :], axis=0, keepdims=True)]
        for a in range(1, _CAND_SHORT):
            r0 = P_TOPK + (a - 1) * _CAND_SHORT
            counts.append(jnp.sum(npick[r0:r0 + _CAND_SHORT, :], axis=0, keepdims=True))
        r0 = P_TOPK + (_CAND_SHORT - 1) * _CAND_SHORT
        counts += [npick[r0 + a:r0 + a + 1, :] for a in range(P_TOPK - _CAND_SHORT)]
        finish(h, scores, v1, v2, lambda a: scores[0] == v1[a:a + 1, :], rank2, counts, zsum)

        @pl.when(jnp.min(jnp.where(clean, 1.0, 0.0)) < 0.5)
        def _():
            tops, ranks = [], []
            for c in range(2):
                vals, rank, _ = _extract_top(scores[c], rowid, P_TOPK)
                tops.append(jnp.concatenate(vals, axis=0))
                ranks.append(rank)
            v1, v2 = tops
            cand = jnp.concatenate([v1[a:a + 1, :] + v2 for a in range(P_TOPK)], axis=0)
            _, _, picked = _extract_top(cand, rowid2, P_TOPK)
            ev1 = jnp.exp(v1 - v1[0:1, :])
            ev2 = jnp.exp(v2 - v2[0:1, :])
            prod = jnp.concatenate([ev1[a:a + 1, :] * ev2 for a in range(P_TOPK)], axis=0)
            zsum = jnp.sum(jnp.where(picked, prod, 0.0), axis=0, keepdims=True)
            npick = jnp.where(picked, 1.0, 0.0)
            counts = [jnp.sum(npick[a * P_TOPK:(a + 1) * P_TOPK, :], axis=0, keepdims=True) for a in range(P_TOPK)]
            finish(h, scores, v1, v2, lambda a: ranks[0] == float(a), ranks[1], counts, zsum)

        return carry

    lax.fori_loop(0, P_HEADS, per_head, 0)


def _peer_topk(q, keys16, layer, *, tt):
    t = q.shape[0]
    st_shape = jax.ShapeDtypeStruct((P_HEADS, N_KEYS, t), F32)
    st_shape_b = jax.ShapeDtypeStruct((P_HEADS, N_KEYS, t), BF16)
    st_spec = pl.BlockSpec((P_HEADS, N_KEYS, tt), lambda i: (0, 0, i))
    return pl.pallas_call(
        functools.partial(_topk_body, tt=tt),
        grid=(t // tt,),
        in_specs=[
            pl.BlockSpec((tt, q.shape[1]), lambda i: (i, 0)),
            pl.BlockSpec((None,) + keys16.shape[1:], lambda i: (layer, 0, 0, 0)),
        ],
        out_specs=[st_spec] * 4,
        out_shape=[st_shape, st_shape, st_shape_b, st_shape_b],
        compiler_params=_cparams(("parallel",)),
        name="peer_topk",
    )(q, keys16)


def _peer_dense_body(xn_ref, u_ref, v_ref, c1_ref, w1_ref, r2_ref, e2_ref, x_ref, g_ref, o_ref, c1s_ref, w1s_ref, *hg_refs,
                     tm, te, tc, rb, final):
    e = pl.program_id(1)

    @pl.when(e == 0)
    def _():
        o_ref[...] = x_ref[...]

    ub = u_ref[...].astype(BF16)
    vb = v_ref[...].astype(BF16)

    ng = te // N_KEYS
    nchunk = tm // tc
    which = e % (8 // ng)

    for src, dst in ((c1_ref, c1s_ref), (w1_ref, w1s_ref)):
        rows = src[:, 0:ng, :]
        for g in range(1, 8 // ng):
            rows = jnp.where(which == g, src[:, g * ng:(g + 1) * ng, :], rows)
        dst[:, 0:ng, :] = rows

    def bf16_row(ref, h, ii, t0):
        return ref[h, ii:ii + 1, t0:t0 + tc].astype(BF16)

    def stage_a(ci):
        t0 = ci * tc
        ht = _dot_nt(ub, xn_ref[t0:t0 + tc, :])
        return ((0.5 * ht) * (1.0 + lax.erf(ht * (2.0 ** -0.5)))).astype(BF16)

    def stage_b(ci, act):
        t0 = ci * tc
        for ii in range(ng):
            rows = [(bf16_row(c1s_ref, h, ii, t0), bf16_row(w1s_ref, h, ii, t0)) for h in range(P_HEADS)]
            for j0 in range(0, N_KEYS, rb):
                r0 = ii * N_KEYS + j0
                gate = jnp.zeros((rb, tc), BF16)
                for h in range(P_HEADS):
                    cnt, w1 = rows[h]
                    gate = gate + jnp.where(r2_ref[h, j0:j0 + rb, t0:t0 + tc] < cnt,
                                            e2_ref[h, j0:j0 + rb, t0:t0 + tc] * w1, jnp.zeros((), BF16))
                hg_refs[ci][r0:r0 + rb, :] = act[r0:r0 + rb, :] * gate

    def stage_c(ci):
        t0 = ci * tc
        o_ref[t0:t0 + tc, :] += _dot_tn(hg_refs[ci][...], vb)

    ahead = 2
    hts = {ci: stage_a(ci) for ci in range(min(ahead, nchunk))}
    for ci in range(nchunk):
        if ci + ahead < nchunk:
            hts[ci + ahead] = stage_a(ci + ahead)
        stage_b(ci, hts.pop(ci))
        stage_c(ci)

    if final:
        @pl.when(e == pl.num_programs(1) - 1)
        def _():
            y = o_ref[...]
            o_ref[...] = y * lax.rsqrt(jnp.mean(y * y, axis=-1, keepdims=True) + EPS) * g_ref[...]


def _peer_dense(xn, u, v, layer, c1, w1, r2, e2, x2d, g, *, tm, te, tc, rb, final):
    t, d = xn.shape
    ne = u.shape[1]
    ng = te // N_KEYS
    once = pl.Buffered(1)
    row_spec = pl.BlockSpec((P_HEADS, 8, tm), lambda i, e: (0, e // (8 // ng), i))
    full_spec = pl.BlockSpec((P_HEADS, N_KEYS, tm), lambda i, e: (0, 0, i), pipeline_mode=once)
    tok_spec = pl.BlockSpec((tm, d), lambda i, e: (i, 0), pipeline_mode=once)
    return pl.pallas_call(
        functools.partial(_peer_dense_body, tm=tm, te=te, tc=tc, rb=rb, final=final),
        grid=(t // tm, ne // te),
        in_specs=[
            tok_spec,
            pl.BlockSpec((None, te, d), lambda i, e: (layer, e, 0)),
            pl.BlockSpec((None, te, d), lambda i, e: (layer, e, 0)),
            row_spec, row_spec, full_spec, full_spec,
            tok_spec,
            pl.BlockSpec((1, d), lambda i, e: (0, 0)),
        ],
        out_specs=pl.BlockSpec((tm, d), lambda i, e: (i, 0), pipeline_mode=once),
        out_shape=jax.ShapeDtypeStruct((t, d), F32),
        scratch_shapes=[pltpu.VMEM((P_HEADS, 8, tm), F32), pltpu.VMEM((P_HEADS, 8, tm), F32)]
        + [pltpu.VMEM((te, tc), BF16)] * (tm // tc),
        compiler_params=_cparams(("parallel", "arbitrary")),
        name="peer_dense",
    )(xn, u, v, c1, w1, r2, e2, x2d, g)


def _pad_heads(w):
    lead, n = w.shape[:-1], w.shape[-1] // B_HEAD_DIM
    w = w.reshape(lead + (n, B_HEAD_DIM))
    w = jnp.pad(w, [(0, 0)] * len(lead) + [(0, 0), (0, B_HEAD_PAD - B_HEAD_DIM)])
    return w.reshape(lead + (n * B_HEAD_PAD,))


def _regroup_w_in(w):
    sizes = (3 * A_WIDTH, A_HEADS, 2 * B_WIDTH, 2 * B_WIDTH, 2 * B_HEADS, 2 * C_WIDTH)
    parts, p = [], 0
    for sz in sizes:
        parts.append(w[..., p:p + sz])
        p += sz
    a_qkv, a_f, b_qk, b_vo, b_if, c_vg = parts
    lead = w.shape[:-1]
    main = jnp.concatenate([_pad_heads(b_qk), _pad_heads(b_vo), c_vg, a_qkv,
                            jnp.zeros(lead + (Z_WIDTH - Z_USED,), w.dtype)], axis=-1)
    gates = jnp.concatenate([a_f, b_if, jnp.zeros(lead + (GATE_LANES - A_HEADS - 2 * B_HEADS,), w.dtype)], axis=-1)
    return main.astype(BF16), gates


def _tile_sizes(t, seq):
    return dict(
        tm_in=min(1024, t), tn_in=1280,
        tb_gate=min(256, seq),
        tq=min(512, seq), tk=min(512, seq),
        ts_conf=min(256, seq),
        tm_out=min(512, t),
        tm_q=min(512, t),
        tt=min(256, t),
        tm_pd=min(1024, t), te=512, tc_pd=min(256, t), rb_pd=32,
    )


def kernel(x, w_in, fox_f_bias, mlstm_conv_w, mlstm_i_bias, mlstm_f_bias, conf_dw_w, conf_dw_b, conf_ln_g, conf_ln_b, conf_pw_w, conf_pw_b, out_norm_g, w_out, norm1_g, norm2_g, peer_wq, peer_keys, peer_u, peer_v, final_g):
    batch, seq, d = x.shape
    t = batch * seq
    depth = w_in.shape[0]
    ts = _tile_sizes(t, seq)
    xf = x.reshape(t, d)
    rows = lambda v: v[:, None, :]

    w_main, w_gate = _regroup_w_in(w_in)
    wgh, wgl = _split_bf16(w_gate)
    gate_bias = rows(jnp.concatenate([fox_f_bias, mlstm_i_bias, mlstm_f_bias,
                                      jnp.zeros((depth, GATE_LANES - A_HEADS - 2 * B_HEADS), F32)], axis=-1))
    g_a = rows(out_norm_g[:, :A_WIDTH])
    g_b = rows(_pad_heads(out_norm_g[:, A_WIDTH:A_WIDTH + B_WIDTH]))
    g_c = rows(out_norm_g[:, A_WIDTH + B_WIDTH:])
    cw = _pad_heads(mlstm_conv_w)
    wa = w_out[:, :A_WIDTH].astype(BF16)
    wb = jnp.pad(w_out[:, A_WIDTH:A_WIDTH + B_WIDTH].reshape(depth, B_HEADS, B_HEAD_DIM, d),
                 ((0, 0), (0, 0), (0, B_HEAD_PAD - B_HEAD_DIM), (0, 0))).reshape(depth, B_PAD_WIDTH, d).astype(BF16)
    wc = w_out[:, A_WIDTH + B_WIDTH:].astype(BF16)
    pw = conf_pw_w.astype(BF16)
    wq = peer_wq.astype(BF16)
    keys16 = peer_keys.reshape(depth, 2 * P_HEADS, N_KEYS, P_HALF)
    n1, n2 = rows(norm1_g), rows(norm2_g)
    db, lg, lb, pb = rows(conf_dw_b), rows(conf_ln_g), rows(conf_ln_b), rows(conf_pw_b)

    for l in range(depth):
        z, zg = _in_proj(xf, n1, w_main, wgh, wgl, l, tm=ts["tm_in"], tn=ts["tn_in"])
        gcol, grow = _gates(zg.reshape(batch, seq, GATE_LANES), gate_bias, l, tb=ts["tb_gate"])
        z3 = z.reshape(batch, seq, Z_WIDTH)
        ya = _attention(z, gcol, grow.reshape(batch, GATE_ROWS, 1, seq), g_a, l, batch=batch, seq=seq, tq=ts["tq"], tk=ts["tk"])
        yb = _mlstm(z3, gcol, grow, cw, g_b, l)
        yc = _conformer(z3, conf_dw_w, db, lg, lb, pw, pb, g_c, l, ts=ts["ts_conf"])
        xf = _out_proj(xf, ya, yb.reshape(t, B_PAD_WIDTH), yc.reshape(t, C_WIDTH), wa, wb, wc, l, tm=ts["tm_out"])

        q, xn = _peer_q(xf, n2, wq, l, tm=ts["tm_q"])
        c1, w1, r2, e2 = _peer_topk(q, keys16, l, tt=ts["tt"])
        xf = _peer_dense(xn, peer_u, peer_v, l, c1, w1, r2, e2, xf, final_g.reshape(1, d),
                         tm=ts["tm_pd"], te=ts["te"], tc=ts["tc_pd"], rb=ts["rb_pd"], final=(l == depth - 1))

    return xf.reshape(batch, seq, d)
```

```python
import functools

import jax
import jax.numpy as jnp
from jax import lax
from jax.experimental import pallas as pl
from jax.experimental.pallas import tpu as pltpu

F32 = jnp.float32
BF16 = jnp.bfloat16

EPS = 1e-6
LN_EPS = 1e-5

A_HEADS = 6
A_HEAD_DIM = 128
A_WIDTH = A_HEADS * A_HEAD_DIM
B_HEADS = 4
B_HEAD_DIM = 192
B_HEAD_PAD = 256
B_WIDTH = B_HEADS * B_HEAD_DIM
B_PAD_WIDTH = B_HEADS * B_HEAD_PAD
B_CHUNK = 64
B_CONV = 4
C_GROUPS = 4
C_WIDTH = 512
C_CONV = 31
P_HEADS = 8
P_HALF = 128
N_KEYS = 128
P_TOPK = 16
GATE_LANES = 128

COL_BQ = 0
COL_BK = B_PAD_WIDTH
COL_BV = 2 * B_PAD_WIDTH
COL_BO = 3 * B_PAD_WIDTH
COL_CA = 4 * B_PAD_WIDTH
COL_CG = COL_CA + C_WIDTH
COL_AQ = COL_CG + C_WIDTH
COL_AK = COL_AQ + A_WIDTH
COL_AV = COL_AK + A_WIDTH
Z_USED = COL_AV + A_WIDTH
Z_WIDTH = 7680
G_BI = A_HEADS
G_BF = A_HEADS + B_HEADS
GATE_ROWS = 16

VMEM_LIMIT = 60 * 1024 * 1024


def _cparams(sem):
    return pltpu.CompilerParams(dimension_semantics=sem, vmem_limit_bytes=VMEM_LIMIT)


def _split_bf16(a):
    hi = a.astype(BF16)
    lo = (a - hi.astype(F32)).astype(BF16)
    return hi, lo


def _dot(a, b):
    return jnp.dot(a, b, preferred_element_type=F32)


def _dot_nt(a, b):
    return lax.dot_general(a, b, (((1,), (1,)), ((), ())), preferred_element_type=F32)


def _dot_tn(a, b):
    return lax.dot_general(a, b, (((0,), (0,)), ((), ())), preferred_element_type=F32)


def _dot3(ah, al, bh, bl, dot):
    return dot(ah, bh) + dot(ah, bl) + dot(al, bh)


def _sigmoid(x):
    return 1.0 / (1.0 + jnp.exp(-x))


def _inproj_body(x_ref, g_ref, w_ref, wgh_ref, wgl_ref, z_ref, zg_ref, xn_ref):
    @pl.when(pl.program_id(1) == 0)
    def _():
        xf = x_ref[...]
        ms = jnp.mean(xf * xf, axis=-1, keepdims=True)
        xn = xf * lax.rsqrt(ms + EPS) * g_ref[...]
        hi, lo = _split_bf16(xn)
        xn_ref[...] = hi
        zg_ref[...] = _dot3(hi, lo, wgh_ref[...], wgl_ref[...], _dot)

    z_ref[...] = _dot(xn_ref[...], w_ref[...])


def _in_proj(x2d, g, w, wgh, wgl, layer, *, tm, tn):
    t, d = x2d.shape
    n = w.shape[2]
    return pl.pallas_call(
        _inproj_body,
        grid=(t // tm, n // tn),
        in_specs=[
            pl.BlockSpec((tm, d), lambda i, j: (i, 0), pipeline_mode=pl.Buffered(1)),
            pl.BlockSpec((None, 1, d), lambda i, j: (layer, 0, 0)),
            pl.BlockSpec((None, d, tn), lambda i, j: (layer, 0, j)),
            pl.BlockSpec((None, d, GATE_LANES), lambda i, j: (layer, 0, 0)),
            pl.BlockSpec((None, d, GATE_LANES), lambda i, j: (layer, 0, 0)),
        ],
        out_specs=[
            pl.BlockSpec((tm, tn), lambda i, j: (i, j)),
            pl.BlockSpec((tm, GATE_LANES), lambda i, j: (i, 0)),
        ],
        out_shape=[
            jax.ShapeDtypeStruct((t, n), F32),
            jax.ShapeDtypeStruct((t, GATE_LANES), F32),
        ],
        scratch_shapes=[pltpu.VMEM((tm, d), BF16)],
        compiler_params=_cparams(("parallel", "arbitrary")),
        name="in_proj",
    )(x2d, g, w, wgh, wgl)


def _gates_body(zg_ref, bias_ref, col_ref, row_ref, carry_ref, *, tb):
    @pl.when(pl.program_id(1) == 0)
    def _():
        carry_ref[...] = jnp.zeros_like(carry_ref)

    v = zg_ref[0] + bias_ref[...]
    ls = jnp.minimum(v, 0.0) - jnp.log1p(jnp.exp(-jnp.abs(v)))
    r = lax.broadcasted_iota(jnp.int32, (tb, tb), 0)
    c = lax.broadcasted_iota(jnp.int32, (tb, tb), 1)
    tri = (c <= r).astype(F32)
    tri_chunk = jnp.where((r // B_CHUNK) == (c // B_CHUNK), tri, 0.0)
    csum = jnp.dot(tri, ls, precision=lax.Precision.HIGHEST, preferred_element_type=F32) + carry_ref[...]
    bcum = jnp.dot(tri_chunk, ls, precision=lax.Precision.HIGHEST, preferred_element_type=F32)
    carry_ref[...] = csum[tb - 1:tb, :]
    lane = lax.broadcasted_iota(jnp.int32, (tb, GATE_LANES), 1)
    out = jnp.where(lane < G_BI, csum, jnp.where(lane < G_BF, v, bcum))
    col_ref[0] = out
    row_ref[0] = out.T[0:GATE_ROWS, :]


def _gates(zg3, bias, layer, *, tb):
    b, s, _ = zg3.shape
    return pl.pallas_call(
        functools.partial(_gates_body, tb=tb),
        grid=(b, s // tb),
        in_specs=[
            pl.BlockSpec((1, tb, GATE_LANES), lambda i, j: (i, j, 0)),
            pl.BlockSpec((None, 1, GATE_LANES), lambda i, j: (layer, 0, 0)),
        ],
        out_specs=[
            pl.BlockSpec((1, tb, GATE_LANES), lambda i, j: (i, j, 0)),
            pl.BlockSpec((1, GATE_ROWS, tb), lambda i, j: (i, 0, j)),
        ],
        out_shape=[
            jax.ShapeDtypeStruct((b, s, GATE_LANES), F32),
            jax.ShapeDtypeStruct((b, GATE_ROWS, s), F32),
        ],
        scratch_shapes=[pltpu.VMEM((1, GATE_LANES), F32)],
        compiler_params=_cparams(("parallel", "arbitrary")),
        name="gates",
    )(zg3, bias)


def _attn_body(q_ref, k_ref, v_ref, ccol_ref, crow_ref, g_ref, o_ref, kb_ref, vb_ref, m_ref, l_ref, acc_ref, *, tq, tk):
    h = pl.program_id(1)
    qi = pl.program_id(2)
    log2e = 1.4426950408889634
    scale2 = (A_HEAD_DIM ** -0.5) * log2e

    @pl.when(qi == 0)
    def _():
        kb_ref[...] = k_ref[...].astype(BF16)
        vb_ref[...] = v_ref[...].astype(BF16)

    q = q_ref[...].astype(BF16)
    lane = lax.broadcasted_iota(jnp.int32, (tq, GATE_LANES), 1)
    ct2 = jnp.sum(jnp.where(lane == h, ccol_ref[0], 0.0), axis=-1, keepdims=True) * log2e
    m_ref[...] = jnp.full(m_ref.shape, -jnp.inf, F32)
    l_ref[...] = jnp.zeros_like(l_ref)
    acc_ref[...] = jnp.zeros_like(acc_ref)

    def logits(j):
        k0 = pl.multiple_of(j * tk, tk)
        cs2 = crow_ref[0, 0, :, pl.ds(k0, tk)] * log2e
        return _dot_nt(q, kb_ref[pl.ds(k0, tk), :]) * scale2 + (ct2 - cs2)

    def softmax(s):
        m_prev = m_ref[...]
        m_new = jnp.maximum(m_prev, jnp.max(s, axis=-1, keepdims=True))
        p = jnp.exp2(s - m_new)
        alpha = jnp.exp2(m_prev - m_new)
        l_ref[...] = alpha * l_ref[...] + jnp.sum(p, axis=-1, keepdims=True)
        m_ref[...] = m_new
        return p.astype(BF16), alpha

    def values(j, p, alpha):
        k0 = pl.multiple_of(j * tk, tk)
        acc_ref[...] = alpha * acc_ref[...] + _dot(p, vb_ref[pl.ds(k0, tk), :])

    def body(j, s_cur):
        s_next = logits(j + 1)
        p, alpha = softmax(s_cur)
        values(j, p, alpha)
        return s_next

    per_q = tq // tk
    first = qi * per_q
    s_cur = lax.fori_loop(0, first, body, logits(0))
    row = lax.broadcasted_iota(jnp.int32, (tq, tk), 0)
    col = lax.broadcasted_iota(jnp.int32, (tq, tk), 1)
    for dd in range(per_q):
        s_next = logits(first + dd + 1) if dd + 1 < per_q else None
        p, alpha = softmax(jnp.where(col + dd * tk <= row, s_cur, -jnp.inf))
        values(first + dd, p, alpha)
        s_cur = s_next
    o = acc_ref[...] / l_ref[...]
    o = o * lax.rsqrt(jnp.mean(o * o, axis=-1, keepdims=True) + EPS) * g_ref[...]
    o_ref[...] = o.astype(BF16)


def _attention(z2d, gcol, grow, g_a, layer, *, batch, seq, tq, tk):
    nq = seq // tq
    qc, kc, vc = COL_AQ // 128, COL_AK // 128, COL_AV // 128
    return pl.pallas_call(
        functools.partial(_attn_body, tq=tq, tk=tk),
        grid=(batch, A_HEADS, nq),
        in_specs=[
            pl.BlockSpec((tq, 128), lambda b, h, i: (b * nq + i, qc + h)),
            pl.BlockSpec((seq, 128), lambda b, h, i: (b, kc + h)),
            pl.BlockSpec((seq, 128), lambda b, h, i: (b, vc + h)),
            pl.BlockSpec((1, tq, GATE_LANES), lambda b, h, i: (b, i, 0)),
            pl.BlockSpec((1, 1, 1, seq), lambda b, h, i: (b, h, 0, 0)),
            pl.BlockSpec((None, 1, 128), lambda b, h, i: (layer, 0, h)),
        ],
        out_specs=pl.BlockSpec((tq, 128), lambda b, h, i: (b * nq + i, h)),
        out_shape=jax.ShapeDtypeStruct((batch * seq, A_WIDTH), BF16),
        scratch_shapes=[pltpu.VMEM((seq, 128), BF16), pltpu.VMEM((seq, 128), BF16),
                        pltpu.VMEM((tq, 1), F32), pltpu.VMEM((tq, 1), F32), pltpu.VMEM((tq, 128), F32)],
        compiler_params=_cparams(("parallel", "parallel", "arbitrary")),
        name="fox_attention",
    )(z2d, z2d, z2d, gcol, grow, g_a)


def _mlstm_body(zb_ref, gcol_ref, grow_ref, cw_ref, g_ref, y_ref, ext_ref, c_ref, m_ref, *, nb):
    k = pl.program_id(0)
    ln = B_CHUNK
    hp = B_HEAD_PAD

    @pl.when(k == 0)
    def _():
        ext_ref[:, 0:8, :] = jnp.zeros((nb, 8, 2 * B_PAD_WIDTH), F32)
        c_ref[...] = jnp.zeros_like(c_ref)
        m_ref[...] = jnp.zeros_like(m_ref)

    first_half = (k % 2) == 0
    r = lax.broadcasted_iota(jnp.int32, (ln, ln), 0)
    c = lax.broadcasted_iota(jnp.int32, (ln, ln), 1)
    causal = c <= r
    lane = lax.broadcasted_iota(jnp.int32, (ln, hp), 1)
    qscale = B_HEAD_DIM ** -0.5

    for b in range(nb):
        ext_ref[b, 8:8 + ln, :] = zb_ref[b, :, 0:2 * B_PAD_WIDTH]
        conv = cw_ref[0:1, :] * ext_ref[b, 5:5 + ln, :]
        for j in range(1, B_CONV):
            conv = conv + cw_ref[j:j + 1, :] * ext_ref[b, 5 + j:5 + j + ln, :]
        ext_ref[b, 0:8, :] = ext_ref[b, ln:ln + 8, :]
        qk = conv * _sigmoid(conv)
        grow = grow_ref[b]
        grow = jnp.where(first_half, grow[:, 0:ln], grow[:, ln:2 * ln])
        gcol = gcol_ref[b]
        for h in range(B_HEADS):
            st = b * B_HEADS + h
            qq = (qk[:, COL_BQ + h * hp:COL_BQ + (h + 1) * hp] * qscale).astype(BF16)
            kk = qk[:, COL_BK + h * hp:COL_BK + (h + 1) * hp]
            vv = zb_ref[b, :, COL_BV + h * hp:COL_BV + (h + 1) * hp]
            vv = jnp.where(lane == B_HEAD_DIM, 1.0, vv).astype(BF16)
            og = zb_ref[b, :, COL_BO + h * hp:COL_BO + (h + 1) * hp]
            bc_col = gcol[:, G_BF + h:G_BF + h + 1]
            li_col = gcol[:, G_BI + h:G_BI + h + 1]
            bc_row = grow[G_BF + h:G_BF + h + 1, :]
            li_row = grow[G_BI + h:G_BI + h + 1, :]
            m = m_ref[st, 0:1, 0:1]

            d = jnp.where(causal, bc_col - bc_row + li_row, -jnp.inf)
            inter = bc_col + m
            m_t = jnp.maximum(inter, jnp.max(d, axis=-1, keepdims=True))
            w_inter = jnp.exp(inter - m_t)
            sc = _dot_nt(qq, kk.astype(BF16)) * jnp.exp(d - m_t)
            cm = c_ref[st]
            num = w_inter * _dot(qq, cm.astype(BF16)) + _dot(sc.astype(BF16), vv)
            den = num[:, B_HEAD_DIM:B_HEAD_DIM + 1]
            hh = num / jnp.maximum(jnp.abs(den), jnp.exp(-m_t))

            b_last = bc_col[ln - 1:ln, :]
            dec = b_last - bc_col + li_col
            m_new = jnp.maximum(b_last + m, jnp.max(dec, axis=0, keepdims=True))
            a = jnp.exp(b_last + m - m_new)
            wk = (jnp.exp(dec - m_new) * kk).astype(BF16)
            c_ref[st] = a * cm + _dot_tn(wk, vv)
            m_ref[st] = jnp.broadcast_to(m_new, (8, 128))

            y = jnp.where(lane < B_HEAD_DIM, _sigmoid(og) * hh, 0.0)
            ms = jnp.sum(y * y, axis=-1, keepdims=True) * (1.0 / B_HEAD_DIM)
            y = y * lax.rsqrt(ms + EPS) * g_ref[:, h * hp:(h + 1) * hp]
            y_ref[b, :, h * hp:(h + 1) * hp] = y.astype(BF16)


def _mlstm(z3, gcol, grow, cw, g_b, layer):
    nb, seq, _ = z3.shape
    nc = seq // B_CHUNK
    return pl.pallas_call(
        functools.partial(_mlstm_body, nb=nb),
        grid=(nc,),
        in_specs=[
            pl.BlockSpec((nb, B_CHUNK, 4 * B_PAD_WIDTH), lambda k: (0, k, 0)),
            pl.BlockSpec((nb, B_CHUNK, GATE_LANES), lambda k: (0, k, 0)),
            pl.BlockSpec((nb, GATE_ROWS, 2 * B_CHUNK), lambda k: (0, 0, k // 2)),
            pl.BlockSpec((None, B_CONV, 2 * B_PAD_WIDTH), lambda k: (layer, 0, 0)),
            pl.BlockSpec((None, 1, B_PAD_WIDTH), lambda k: (layer, 0, 0)),
        ],
        out_specs=pl.BlockSpec((nb, B_CHUNK, B_PAD_WIDTH), lambda k: (0, k, 0)),
        out_shape=jax.ShapeDtypeStruct((nb, seq, B_PAD_WIDTH), BF16),
        scratch_shapes=[
            pltpu.VMEM((nb, B_CHUNK + 8, 2 * B_PAD_WIDTH), F32),
            pltpu.VMEM((nb * B_HEADS, B_HEAD_PAD, B_HEAD_PAD), F32),
            pltpu.VMEM((nb * B_HEADS, 8, 128), F32),
        ],
        compiler_params=_cparams(("arbitrary",)),
        name="mlstm",
    )(z3, gcol, grow, cw, g_b)


def _conf_body(ca_ref, cg_ref, dw_ref, db_ref, lg_ref, lb_ref, pw_ref, pb_ref, g_ref, y_ref, ext_ref, *, ts):
    halo = 32

    @pl.when(pl.program_id(1) == 0)
    def _():
        ext_ref[0:halo, :] = jnp.zeros((halo, C_WIDTH), F32)

    ext_ref[halo:halo + ts, :] = ca_ref[0] * _sigmoid(cg_ref[0])
    off = halo - (C_CONV - 1)
    acc = None
    for r in range(8):
        taps = [(m, 8 * m + r - off) for m in range(halo // 8 + 1) if 0 <= 8 * m + r - off < C_CONV]
        ln = ts if r == 0 else ts + 8
        part = None
        for m, j in taps:
            term = dw_ref[j:j + 1, :] * ext_ref[8 * m:8 * m + ln, :]
            part = term if part is None else part + term
        part = part if r == 0 else part[r:r + ts, :]
        acc = part if acc is None else acc + part
    ext_ref[0:halo, :] = ext_ref[ts:ts + halo, :]
    y = acc + db_ref[...]
    mu = jnp.mean(y, axis=-1, keepdims=True)
    yc = y - mu
    y = yc * lax.rsqrt(jnp.mean(yc * yc, axis=-1, keepdims=True) + LN_EPS)
    y = y * lg_ref[...] + lb_ref[...]
    y = y * _sigmoid(y)
    y = _dot(y.astype(BF16), pw_ref[...]) + pb_ref[...]
    gw = C_WIDTH // C_GROUPS
    for gi in range(C_GROUPS):
        yg = y[:, gi * gw:(gi + 1) * gw]
        yg = yg * lax.rsqrt(jnp.mean(yg * yg, axis=-1, keepdims=True) + EPS) * g_ref[:, gi * gw:(gi + 1) * gw]
        y_ref[0, :, gi * gw:(gi + 1) * gw] = yg.astype(BF16)


def _conformer(z3, dw, db, lg, lb, pw, pb, g_c, layer, *, ts):
    nb, seq, _ = z3.shape
    ca, cg = COL_CA // C_WIDTH, COL_CG // C_WIDTH
    vec = pl.BlockSpec((None, 1, C_WIDTH), lambda b, i: (layer, 0, 0))
    return pl.pallas_call(
        functools.partial(_conf_body, ts=ts),
        grid=(nb, seq // ts),
        in_specs=[
            pl.BlockSpec((1, ts, C_WIDTH), lambda b, i: (b, i, ca)),
            pl.BlockSpec((1, ts, C_WIDTH), lambda b, i: (b, i, cg)),
            pl.BlockSpec((None, C_CONV, C_WIDTH), lambda b, i: (layer, 0, 0)),
            vec, vec, vec,
            pl.BlockSpec((None, C_WIDTH, C_WIDTH), lambda b, i: (layer, 0, 0)),
            vec, vec,
        ],
        out_specs=pl.BlockSpec((1, ts, C_WIDTH), lambda b, i: (b, i, 0)),
        out_shape=jax.ShapeDtypeStruct((nb, seq, C_WIDTH), BF16),
        scratch_shapes=[pltpu.VMEM((ts + 32, C_WIDTH), F32)],
        compiler_params=_cparams(("parallel", "arbitrary")),
        name="conformer",
    )(z3, z3, dw, db, lg, lb, pw, pb, g_c)


def _outproj_body(x_ref, ya_ref, yb_ref, yc_ref, wa_ref, wb_ref, wc_ref, o_ref):
    o_ref[...] = (x_ref[...] + _dot(ya_ref[...], wa_ref[...]) + _dot(yb_ref[...], wb_ref[...])
                  + _dot(yc_ref[...], wc_ref[...]))


def _out_proj(x2d, ya, yb, yc, wa, wb, wc, layer, *, tm):
    t, d = x2d.shape
    once = pl.Buffered(1)
    return pl.pallas_call(
        _outproj_body,
        grid=(t // tm,),
        in_specs=[
            pl.BlockSpec((tm, d), lambda i: (i, 0)),
            pl.BlockSpec((tm, ya.shape[1]), lambda i: (i, 0)),
            pl.BlockSpec((tm, yb.shape[1]), lambda i: (i, 0)),
            pl.BlockSpec((tm, yc.shape[1]), lambda i: (i, 0)),
            pl.BlockSpec((None,) + wa.shape[1:], lambda i: (layer, 0, 0), pipeline_mode=once),
            pl.BlockSpec((None,) + wb.shape[1:], lambda i: (layer, 0, 0), pipeline_mode=once),
            pl.BlockSpec((None,) + wc.shape[1:], lambda i: (layer, 0, 0), pipeline_mode=once),
        ],
        out_specs=pl.BlockSpec((tm, d), lambda i: (i, 0)),
        out_shape=jax.ShapeDtypeStruct((t, d), F32),
        compiler_params=_cparams(("parallel",)),
        name="out_proj",
    )(x2d, ya, yb, yc, wa, wb, wc)


def _peerq_body(x_ref, g_ref, w_ref, q_ref, xn_ref):
    xf = x_ref[...]
    ms = jnp.mean(xf * xf, axis=-1, keepdims=True)
    xn = (xf * lax.rsqrt(ms + EPS) * g_ref[...]).astype(BF16)
    xn_ref[...] = xn
    q_ref[...] = _dot(xn, w_ref[...])


def _peer_q(x2d, g, w, layer, *, tm):
    t, d = x2d.shape
    n = w.shape[2]
    return pl.pallas_call(
        _peerq_body,
        grid=(t // tm,),
        in_specs=[
            pl.BlockSpec((tm, d), lambda i: (i, 0)),
            pl.BlockSpec((None, 1, d), lambda i: (layer, 0, 0)),
            pl.BlockSpec((None, d, n), lambda i: (layer, 0, 0), pipeline_mode=pl.Buffered(1)),
        ],
        out_specs=[
            pl.BlockSpec((tm, n), lambda i: (i, 0)),
            pl.BlockSpec((tm, d), lambda i: (i, 0)),
        ],
        out_shape=[
            jax.ShapeDtypeStruct((t, n), F32),
            jax.ShapeDtypeStruct((t, d), BF16),
        ],
        compiler_params=_cparams(("parallel",)),
        name="peer_q",
    )(x2d, g, w)


def _extract_top(s, rowid, n_out):
    nrow = s.shape[0]
    rank = jnp.full(s.shape, float(n_out), F32)
    vals = []
    for it in range(n_out):
        m = jnp.max(s, axis=0, keepdims=True)
        idx = jnp.min(jnp.where(s == m, rowid, float(nrow)), axis=0, keepdims=True)
        hit = rowid == idx
        vals.append(m)
        rank = jnp.where(hit, float(it), rank)
        s = jnp.where(hit, -jnp.inf, s)
    return vals, rank, rank < float(n_out)


def _take_top_distinct(s, n_out):
    for _ in range(n_out):
        m = jnp.max(s, axis=0, keepdims=True)
        s = jnp.where(s == m, -jnp.inf, s)
    taken = s == -jnp.inf
    return taken, jnp.sum(jnp.where(taken, 1.0, 0.0), axis=0, keepdims=True)


def _top_sorted(s, n_out):
    nblk = s.shape[0] // 8
    assert nblk == n_out and nblk & (nblk - 1) == 0
    lst = [s[8 * k:8 * k + 8, :] for k in range(nblk)]

    def exchange(i, l, descending):
        hi, lo = jnp.maximum(lst[i], lst[l]), jnp.minimum(lst[i], lst[l])
        lst[i], lst[l] = (hi, lo) if descending else (lo, hi)

    def merge_bitonic():
        d = nblk // 2
        while d >= 1:
            for i in range(nblk):
                if (i // d) % 2 == 0:
                    exchange(i, i + d, True)
            d //= 2

    k = 2
    while k <= nblk:
        j = k // 2
        while j >= 1:
            for i in range(nblk):
                if i ^ j > i:
                    exchange(i, i ^ j, (i & k) == 0)
            j //= 2
        k *= 2
    for shift in (4, 2, 1):
        other = [pltpu.roll(x, shift, 0) for x in lst]
        lst = [jnp.maximum(lst[i], other[nblk - 1 - i]) for i in range(nblk)]
        merge_bitonic()
    return jnp.concatenate([x[0:1, :] for x in lst], axis=0)


_CAND_SHORT = 8


def _topk_body(q_ref, keys_ref, c1_ref, w1_ref, r2_ref, e2_ref, *, tt):
    rowid = lax.broadcasted_iota(jnp.int32, (N_KEYS, tt), 0).astype(F32)
    rowid2 = lax.broadcasted_iota(jnp.int32, (P_TOPK * P_TOPK, tt), 0).astype(F32)

    def pair_grid(x1, x2, op):
        parts = [op(x1[0:1, :], x2)]
        parts += [op(x1[a:a + 1, :], x2[0:_CAND_SHORT, :]) for a in range(1, _CAND_SHORT)]
        parts += [op(x1[_CAND_SHORT:, :], x2[0:1, :])]
        return jnp.concatenate(parts, axis=0)

    def finish(h, scores, v1, v2, is_rank, rank2, counts, zsum):
        cnt = jnp.zeros((N_KEYS, tt), F32)
        for a in range(P_TOPK):
            cnt = jnp.where(is_rank(a), counts[a], cnt)
        c1_ref[h] = cnt
        w1_ref[h] = jnp.exp(scores[0] - v1[0:1, :]) / zsum
        r2_ref[h] = rank2.astype(BF16)
        e2_ref[h] = jnp.exp(scores[1] - v2[0:1, :]).astype(BF16)

    def per_head(h, carry):
        scores = []
        for c in range(2):
            kh, kl = _split_bf16(keys_ref[2 * h + c])
            col = pl.multiple_of((2 * h + c) * P_HALF, P_HALF)
            qh, ql = _split_bf16(q_ref[:, pl.ds(col, P_HALF)])
            scores.append(_dot3(kh, kl, qh, ql, _dot_nt))

        tops, clean = [], None
        for c in range(2):
            v = _top_sorted(scores[c], P_TOPK)
            tops.append(v)
            repeats = jnp.max(jnp.where(v[0:P_TOPK - 1, :] == v[1:P_TOPK, :], 1.0, 0.0), axis=0, keepdims=True)
            reach = jnp.sum(jnp.where(scores[c] >= v[P_TOPK - 1:P_TOPK, :], 1.0, 0.0), axis=0, keepdims=True)
            ok = (repeats < 0.5) & (reach == float(P_TOPK))
            clean = ok if clean is None else (clean & ok)
        v1, v2 = tops
        rank2 = jnp.full((N_KEYS, tt), float(P_TOPK), F32)
        for a in range(P_TOPK):
            rank2 = jnp.where(scores[1] == v2[a:a + 1, :], float(a), rank2)
        cand = pair_grid(v1, v2, lambda x, y: x + y)
        picked, lost = _take_top_distinct(cand, P_TOPK)
        clean = clean & (lost == float(P_TOPK))
        ev1 = jnp.exp(v1 - v1[0:1, :])
        ev2 = jnp.exp(v2 - v2[0:1, :])
        prod = pair_grid(ev1, ev2, lambda x, y: x * y)
        zsum = jnp.sum(jnp.where(picked, prod, 0.0), axis=0, keepdims=True)
        npick = jnp.where(picked, 1.0, 0.0)
        counts = [jnp.sum(npick[0:P_TOPK, :], axis=0, keepdims=True)]
        for a in range(1, _CAND_SHORT):
            r0 = P_TOPK + (a - 1) * _CAND_SHORT
            counts.append(jnp.sum(npick[r0:r0 + _CAND_SHORT, :], axis=0, keepdims=True))
        r0 = P_TOPK + (_CAND_SHORT - 1) * _CAND_SHORT
        counts += [npick[r0 + a:r0 + a + 1, :] for a in range(P_TOPK - _CAND_SHORT)]
        finish(h, scores, v1, v2, lambda a: scores[0] == v1[a:a + 1, :], rank2, counts, zsum)

        @pl.when(jnp.min(jnp.where(clean, 1.0, 0.0)) < 0.5)
        def _():
            tops, ranks = [], []
            for c in range(2):
                vals, rank, _ = _extract_top(scores[c], rowid, P_TOPK)
                tops.append(jnp.concatenate(vals, axis=0))
                ranks.append(rank)
            v1, v2 = tops
            cand = jnp.concatenate([v1[a:a + 1, :] + v2 for a in range(P_TOPK)], axis=0)
            _, _, picked = _extract_top(cand, rowid2, P_TOPK)
            ev1 = jnp.exp(v1 - v1[0:1, :])
            ev2 = jnp.exp(v2 - v2[0:1, :])
            prod = jnp.concatenate([ev1[a:a + 1, :] * ev2 for a in range(P_TOPK)], axis=0)
            zsum = jnp.sum(jnp.where(picked, prod, 0.0), axis=0, keepdims=True)
            npick = jnp.where(picked, 1.0, 0.0)
            counts = [jnp.sum(npick[a * P_TOPK:(a + 1) * P_TOPK, :], axis=0, keepdims=True) for a in range(P_TOPK)]
            finish(h, scores, v1, v2, lambda a: ranks[0] == float(a), ranks[1], counts, zsum)

        return carry

    lax.fori_loop(0, P_HEADS, per_head, 0)


def _peer_topk(q, keys16, layer, *, tt):
    t = q.shape[0]
    st_shape = jax.ShapeDtypeStruct((P_HEADS, N_KEYS, t), F32)
    st_shape_b = jax.ShapeDtypeStruct((P_HEADS, N_KEYS, t), BF16)
    st_spec = pl.BlockSpec((P_HEADS, N_KEYS, tt), lambda i: (0, 0, i))
    return pl.pallas_call(
        functools.partial(_topk_body, tt=tt),
        grid=(t // tt,),
        in_specs=[
            pl.BlockSpec((tt, q.shape[1]), lambda i: (i, 0)),
            pl.BlockSpec((None,) + keys16.shape[1:], lambda i: (layer, 0, 0, 0)),
        ],
        out_specs=[st_spec] * 4,
        out_shape=[st_shape, st_shape, st_shape_b, st_shape_b],
        compiler_params=_cparams(("parallel",)),
        name="peer_topk",
    )(q, keys16)


def _peer_dense_body(xn_ref, u_ref, v_ref, c1_ref, w1_ref, r2_ref, e2_ref, x_ref, g_ref, o_ref, c1s_ref, w1s_ref, *hg_refs,
                     tm, te, tc, rb, final):
    e = pl.program_id(1)

    @pl.when(e == 0)
    def _():
        o_ref[...] = x_ref[...]

    ub = u_ref[...].astype(BF16)
    vb = v_ref[...].astype(BF16)

    ng = te // N_KEYS
    nchunk = tm // tc
    which = e % (8 // ng)

    for src, dst in ((c1_ref, c1s_ref), (w1_ref, w1s_ref)):
        rows = src[:, 0:ng, :]
        for g in range(1, 8 // ng):
            rows = jnp.where(which == g, src[:, g * ng:(g + 1) * ng, :], rows)
        dst[:, 0:ng, :] = rows

    def bf16_row(ref, h, ii, t0):
        return ref[h, ii:ii + 1, t0:t0 + tc].astype(BF16)

    def stage_a(ci):
        t0 = ci * tc
        ht = _dot_nt(ub, xn_ref[t0:t0 + tc, :])
        return ((0.5 * ht) * (1.0 + lax.erf(ht * (2.0 ** -0.5)))).astype(BF16)

    def stage_b(ci, act):
        t0 = ci * tc
        for ii in range(ng):
            rows = [(bf16_row(c1s_ref, h, ii, t0), bf16_row(w1s_ref, h, ii, t0)) for h in range(P_HEADS)]
            for j0 in range(0, N_KEYS, rb):
                r0 = ii * N_KEYS + j0
                gate = jnp.zeros((rb, tc), BF16)
                for h in range(P_HEADS):
                    cnt, w1 = rows[h]
                    gate = gate + jnp.where(r2_ref[h, j0:j0 + rb, t0:t0 + tc] < cnt,
                                            e2_ref[h, j0:j0 + rb, t0:t0 + tc] * w1, jnp.zeros((), BF16))
                hg_refs[ci][r0:r0 + rb, :] = act[r0:r0 + rb, :] * gate

    def stage_c(ci):
        t0 = ci * tc
        o_ref[t0:t0 + tc, :] += _dot_tn(hg_refs[ci][...], vb)

    ahead = 2
    hts = {ci: stage_a(ci) for ci in range(min(ahead, nchunk))}
    for ci in range(nchunk):
        if ci + ahead < nchunk:
            hts[ci + ahead] = stage_a(ci + ahead)
        stage_b(ci, hts.pop(ci))
        stage_c(ci)

    if final:
        @pl.when(e == pl.num_programs(1) - 1)
        def _():
            y = o_ref[...]
            o_ref[...] = y * lax.rsqrt(jnp.mean(y * y, axis=-1, keepdims=True) + EPS) * g_ref[...]


def _peer_dense(xn, u, v, layer, c1, w1, r2, e2, x2d, g, *, tm, te, tc, rb, final):
    t, d = xn.shape
    ne = u.shape[1]
    ng = te // N_KEYS
    once = pl.Buffered(1)
    row_spec = pl.BlockSpec((P_HEADS, 8, tm), lambda i, e: (0, e // (8 // ng), i))
    full_spec = pl.BlockSpec((P_HEADS, N_KEYS, tm), lambda i, e: (0, 0, i), pipeline_mode=once)
    tok_spec = pl.BlockSpec((tm, d), lambda i, e: (i, 0), pipeline_mode=once)
    return pl.pallas_call(
        functools.partial(_peer_dense_body, tm=tm, te=te, tc=tc, rb=rb, final=final),
        grid=(t // tm, ne // te),
        in_specs=[
            tok_spec,
            pl.BlockSpec((None, te, d), lambda i, e: (layer, e, 0)),
            pl.BlockSpec((None, te, d), lambda i, e: (layer, e, 0)),
            row_spec, row_spec, full_spec, full_spec,
            tok_spec,
            pl.BlockSpec((1, d), lambda i, e: (0, 0)),
        ],
        out_specs=pl.BlockSpec((tm, d), lambda i, e: (i, 0), pipeline_mode=once),
        out_shape=jax.ShapeDtypeStruct((t, d), F32),
        scratch_shapes=[pltpu.VMEM((P_HEADS, 8, tm), F32), pltpu.VMEM((P_HEADS, 8, tm), F32)]
        + [pltpu.VMEM((te, tc), BF16)] * (tm // tc),
        compiler_params=_cparams(("parallel", "arbitrary")),
        name="peer_dense",
    )(xn, u, v, c1, w1, r2, e2, x2d, g)


def _pad_heads(w):
    lead, n = w.shape[:-1], w.shape[-1] // B_HEAD_DIM
    w = w.reshape(lead + (n, B_HEAD_DIM))
    w = jnp.pad(w, [(0, 0)] * len(lead) + [(0, 0), (0, B_HEAD_PAD - B_HEAD_DIM)])
    return w.reshape(lead + (n * B_HEAD_PAD,))


_SRC_SIZES = (3 * A_WIDTH, A_HEADS, 2 * B_WIDTH, 2 * B_WIDTH, 2 * B_HEADS, 2 * C_WIDTH)
_SRC_STARTS = tuple(sum(_SRC_SIZES[:i]) for i in range(len(_SRC_SIZES)))
N_IN = sum(_SRC_SIZES)


def _regroup_body(w_ref, main_ref, gate_ref):
    w = w_ref[0]
    rows = w.shape[0]
    a_qkv, a_f, b_qk, b_vo, b_if, c_vg = _SRC_STARTS

    def sect(i):
        return w[:, _SRC_STARTS[i]:_SRC_STARTS[i] + _SRC_SIZES[i]]

    def padded_heads(start, n):
        out = []
        for h in range(n):
            out += [w[:, start + h * B_HEAD_DIM:start + (h + 1) * B_HEAD_DIM],
                    jnp.zeros((rows, B_HEAD_PAD - B_HEAD_DIM), F32)]
        return out

    main = jnp.concatenate(padded_heads(b_qk, 2 * B_HEADS) + padded_heads(b_vo, 2 * B_HEADS)
                           + [sect(5), sect(0), jnp.zeros((rows, Z_WIDTH - Z_USED), F32)], axis=1)
    main_ref[0] = main.astype(BF16)
    gate_ref[0] = jnp.concatenate([sect(1), sect(4), jnp.zeros((rows, GATE_LANES - A_HEADS - 2 * B_HEADS), F32)], axis=1)


def _regroup_w_in(w, *, rows):
    depth, d, n = w.shape
    assert n == N_IN
    return pl.pallas_call(
        _regroup_body,
        grid=(depth, d // rows),
        in_specs=[pl.BlockSpec((1, rows, n), lambda l, i: (l, i, 0))],
        out_specs=[pl.BlockSpec((1, rows, Z_WIDTH), lambda l, i: (l, i, 0)),
                   pl.BlockSpec((1, rows, GATE_LANES), lambda l, i: (l, i, 0))],
        out_shape=[jax.ShapeDtypeStruct((depth, d, Z_WIDTH), BF16),
                   jax.ShapeDtypeStruct((depth, d, GATE_LANES), F32)],
        compiler_params=_cparams(("parallel", "parallel")),
        name="regroup_w_in",
    )(w)


def _tile_sizes(t, seq):
    return dict(
        rows_rg=256,
        tm_in=min(1024, t), tn_in=1280,
        tb_gate=min(256, seq),
        tq=min(512, seq), tk=min(512, seq),
        ts_conf=min(256, seq),
        tm_out=min(512, t),
        tm_q=min(512, t),
        tt=min(256, t),
        tm_pd=min(1024, t), te=512, tc_pd=min(256, t), rb_pd=32,
    )


def kernel(x, w_in, fox_f_bias, mlstm_conv_w, mlstm_i_bias, mlstm_f_bias, conf_dw_w, conf_dw_b, conf_ln_g, conf_ln_b, conf_pw_w, conf_pw_b, out_norm_g, w_out, norm1_g, norm2_g, peer_wq, peer_keys, peer_u, peer_v, final_g):
    batch, seq, d = x.shape
    t = batch * seq
    depth = w_in.shape[0]
    ts = _tile_sizes(t, seq)
    xf = x.reshape(t, d)
    rows = lambda v: v[:, None, :]

    w_main, w_gate = _regroup_w_in(w_in, rows=ts["rows_rg"])
    wgh, wgl = _split_bf16(w_gate)
    gate_bias = rows(jnp.concatenate([fox_f_bias, mlstm_i_bias, mlstm_f_bias,
                                      jnp.zeros((depth, GATE_LANES - A_HEADS - 2 * B_HEADS), F32)], axis=-1))
    g_a = rows(out_norm_g[:, :A_WIDTH])
    g_b = rows(_pad_heads(out_norm_g[:, A_WIDTH:A_WIDTH + B_WIDTH]))
    g_c = rows(out_norm_g[:, A_WIDTH + B_WIDTH:])
    cw = _pad_heads(mlstm_conv_w)
    wa = w_out[:, :A_WIDTH].astype(BF16)
    wb = jnp.pad(w_out[:, A_WIDTH:A_WIDTH + B_WIDTH].reshape(depth, B_HEADS, B_HEAD_DIM, d),
                 ((0, 0), (0, 0), (0, B_HEAD_PAD - B_HEAD_DIM), (0, 0))).reshape(depth, B_PAD_WIDTH, d).astype(BF16)
    wc = w_out[:, A_WIDTH + B_WIDTH:].astype(BF16)
    pw = conf_pw_w.astype(BF16)
    wq = peer_wq.astype(BF16)
    keys16 = peer_keys.reshape(depth, 2 * P_HEADS, N_KEYS, P_HALF)
    n1, n2 = rows(norm1_g), rows(norm2_g)
    db, lg, lb, pb = rows(conf_dw_b), rows(conf_ln_g), rows(conf_ln_b), rows(conf_pw_b)

    for l in range(depth):
        z, zg = _in_proj(xf, n1, w_main, wgh, wgl, l, tm=ts["tm_in"], tn=ts["tn_in"])
        gcol, grow = _gates(zg.reshape(batch, seq, GATE_LANES), gate_bias, l, tb=ts["tb_gate"])
        z3 = z.reshape(batch, seq, Z_WIDTH)
        ya = _attention(z, gcol, grow.reshape(batch, GATE_ROWS, 1, seq), g_a, l, batch=batch, seq=seq, tq=ts["tq"], tk=ts["tk"])
        yb = _mlstm(z3, gcol, grow, cw, g_b, l)
        yc = _conformer(z3, conf_dw_w, db, lg, lb, pw, pb, g_c, l, ts=ts["ts_conf"])
        xf = _out_proj(xf, ya, yb.reshape(t, B_PAD_WIDTH), yc.reshape(t, C_WIDTH), wa, wb, wc, l, tm=ts["tm_out"])

        q, xn = _peer_q(xf, n2, wq, l, tm=ts["tm_q"])
        c1, w1, r2, e2 = _peer_topk(q, keys16, l, tt=ts["tt"])
        xf = _peer_dense(xn, peer_u, peer_v, l, c1, w1, r2, e2, xf, final_g.reshape(1, d),
                         tm=ts["tm_pd"], te=ts["te"], tc=ts["tc_pd"], rb=ts["rb_pd"], final=(l == depth - 1))

    return xf.reshape(batch, seq, d)
```
